```python
import math
import jax, jax.numpy as jnp
from jax import lax
import numpy as np

D_MODEL = 1024
BATCH = 2
SEQ = 8192
DEPTH = 1
DEC_BATCH = 4
DEC_SEQ = 8192
PAST_LEN = 128

PLE_DIM = 256
EPS = 1e-6
GLA_HEADS = 4
GLA_DK = 128
GLA_DV = 256
GLA_GATE_RANK = 16
GLA_GATE_NORM = 16.0
GLA_CHUNK = 64
MLA_HEADS = 8
MLA_Q_LORA = 384
MLA_KV_LORA = 256
MLA_NOPE = 128
MLA_ROPE = 64
MLA_V = 128
MLA_DQK = MLA_NOPE + MLA_ROPE
ROPE_THETA = 10000.0
ATTN_BLOCK = 128
PEER_HEADS = 8
PEER_NKEYS = 128
PEER_N = PEER_NKEYS * PEER_NKEYS
PEER_QDIM = 256
PEER_HALF = PEER_QDIM // 2
PEER_TOPK = 16
PEER_BLOCK = 128
IN_SIZES = (
    GLA_HEADS * GLA_DK,
    GLA_HEADS * GLA_DK,
    GLA_HEADS * GLA_DV,
    GLA_HEADS * GLA_DV,
    2 * GLA_GATE_RANK,
    MLA_Q_LORA,
    MLA_KV_LORA,
    MLA_ROPE,
    2 * D_MODEL,
)
W_IN_COLS = sum(IN_SIZES)

kernel_name = "hybrid_gla_mla_peer_encoder"

F32 = jnp.float32


def _split_points():
    pts, acc = [], 0
    for s in IN_SIZES[:-1]:
        acc += s
        pts.append(acc)
    return pts


def rmsnorm(x, g):
    xf = x.astype(F32)
    y = xf * lax.rsqrt(jnp.mean(xf * xf, axis=-1, keepdims=True) + EPS)
    return (y * g.astype(F32)).astype(x.dtype)


def rope(x, pos):
    r = x.shape[-1]
    half = r // 2
    inv = ROPE_THETA ** (-jnp.arange(half, dtype=F32) * 2.0 / r)
    ang = pos[:, None] * inv[None, :]
    cos = jnp.cos(ang)[None, :, None, :]
    sin = jnp.sin(ang)[None, :, None, :]
    xf = x.astype(F32)
    x1, x2 = xf[..., :half], xf[..., half:]
    return jnp.concatenate([x1 * cos - x2 * sin, x2 * cos + x1 * sin], axis=-1).astype(x.dtype)


def gla_direction(q, k, v, log_a, strict):
    B, S, H, DK = q.shape
    DV = v.shape[-1]
    C = GLA_CHUNK
    N = S // C
    to_chunks = lambda t: t.astype(F32).reshape(B, N, C, H, t.shape[-1]).transpose(0, 3, 1, 2, 4)
    q, k, v, la = to_chunks(q), to_chunks(k), to_chunks(v), to_chunks(log_a)
    b = jnp.cumsum(la, axis=3)
    q_e = q * jnp.exp(b)
    k_e = k * jnp.exp(-b)
    mask = jnp.tril(jnp.ones((C, C), dtype=bool), -1 if strict else 0)
    att = jnp.where(mask, jnp.einsum('bhnid,bhnjd->bhnij', q_e, k_e), 0.0)
    o_intra = jnp.einsum('bhnij,bhnje->bhnie', att, v)
    b_last = b[:, :, :, -1:, :]
    k_s = k * jnp.exp(b_last - b)
    decay = jnp.exp(b_last[:, :, :, 0, :])

    def step(state, xs):
        qn, kn, vn, dn = xs
        o = jnp.einsum('bhcd,bhde->bhce', qn, state)
        state = dn[..., None] * state + jnp.einsum('bhcd,bhce->bhde', kn, vn)
        return state, o

    mv = lambda t: jnp.moveaxis(t, 2, 0)
    s0 = jnp.zeros((B, H, DK, DV), F32)
    _, o_inter = lax.scan(step, s0, (mv(q_e), mv(k_s), mv(v), mv(decay)))
    o = o_intra + jnp.moveaxis(o_inter, 0, 2)
    return o.transpose(0, 2, 3, 1, 4).reshape(B, S, H, DV)


def gla_branch(q, k, v, r, a_lr, a_w_f, a_b_f, a_w_b, a_b_b, gla_norm, w_o):
    B, S, _ = q.shape
    q = q.reshape(B, S, GLA_HEADS, GLA_DK) * (GLA_DK ** -0.5)
    k = k.reshape(B, S, GLA_HEADS, GLA_DK)
    v = v.reshape(B, S, GLA_HEADS, GLA_DV)
    a_f, a_b = jnp.split(a_lr, 2, axis=-1)
    la_f = (jax.nn.log_sigmoid((a_f @ a_w_f + a_b_f).astype(F32)) / GLA_GATE_NORM).reshape(B, S, GLA_HEADS, GLA_DK)
    la_b = (jax.nn.log_sigmoid((a_b @ a_w_b + a_b_b).astype(F32)) / GLA_GATE_NORM).reshape(B, S, GLA_HEADS, GLA_DK)
    o_f = gla_direction(q, k, v, la_f, False)
    flip = lambda t: jnp.flip(t, axis=1)
    o_b = flip(gla_direction(flip(q), flip(k), flip(v), flip(la_b), True))
    o = rmsnorm(o_f + o_b, gla_norm).reshape(B, S, GLA_HEADS * GLA_DV).astype(r.dtype)
    return (o * jax.nn.silu(r)) @ w_o


def mla_branch(c_q, c_kv, k_r, q_norm, w_uq, kv_norm, w_ukv, w_o):
    B, S, _ = c_q.shape
    pos = jnp.arange(S, dtype=F32)
    q = (rmsnorm(c_q, q_norm) @ w_uq).reshape(B, S, MLA_HEADS, MLA_DQK)
    q = jnp.concatenate([q[..., :MLA_NOPE], rope(q[..., MLA_NOPE:], pos)], axis=-1) * (MLA_DQK ** -0.5)
    kv = (rmsnorm(c_kv, kv_norm) @ w_ukv).reshape(B, S, MLA_HEADS, MLA_NOPE + MLA_V)
    k_nope, v = kv[..., :MLA_NOPE], kv[..., MLA_NOPE:]
    k_pe = rope(k_r[:, :, None, :], pos)
    k = jnp.concatenate([k_nope, jnp.broadcast_to(k_pe, (B, S, MLA_HEADS, MLA_ROPE))], axis=-1)
    nb = S // ATTN_BLOCK
    qb = q.reshape(B, nb, ATTN_BLOCK, MLA_HEADS, MLA_DQK).transpose(1, 0, 2, 3, 4)

    def attend(qi):
        s = jnp.einsum('bqhd,bkhd->bhqk', qi, k, preferred_element_type=F32)
        p = jax.nn.softmax(s, axis=-1).astype(v.dtype)
        return jnp.einsum('bhqk,bkhd->bqhd', p, v)

    o = lax.map(attend, qb)
    o = o.transpose(1, 0, 2, 3, 4).reshape(B, S, MLA_HEADS * MLA_V)
    return o @ w_o


def peer(h, w_q, sub_k1, sub_k2, u, v):
    B, S, D = h.shape
    q = (h @ w_q).reshape(B, S, PEER_HEADS, 2, PEER_HALF)
    s1 = jnp.einsum('bshd,nd->bshn', q[..., 0, :], sub_k1, preferred_element_type=F32)
    s2 = jnp.einsum('bshd,nd->bshn', q[..., 1, :], sub_k2, preferred_element_type=F32)
    t1, i1 = lax.top_k(s1, PEER_TOPK)
    t2, i2 = lax.top_k(s2, PEER_TOPK)
    cand = (t1[..., :, None] + t2[..., None, :]).reshape(B, S, PEER_HEADS, PEER_TOPK * PEER_TOPK)
    ts, ic = lax.top_k(cand, PEER_TOPK)
    e1 = jnp.take_along_axis(i1, ic // PEER_TOPK, axis=-1)
    e2 = jnp.take_along_axis(i2, ic % PEER_TOPK, axis=-1)
    ids = e1 * PEER_NKEYS + e2
    g = jax.nn.softmax(ts, axis=-1).astype(h.dtype)
    nb = (B * S) // PEER_BLOCK
    hb = h.reshape(nb, PEER_BLOCK, D)
    idb = ids.reshape(nb, PEER_BLOCK, PEER_HEADS, PEER_TOPK)
    gb = g.reshape(nb, PEER_BLOCK, PEER_HEADS, PEER_TOPK)

    def block(args):
        hx, ix, gx = args
        a = jax.nn.gelu(jnp.einsum('td,thkd->thk', hx, u[ix]), approximate=False)
        return jnp.einsum('thk,thkd->td', gx * a, v[ix])

    out = lax.map(block, (hb, idb, gb))
    return out.reshape(B, S, D)


def _trunk(x, p, w):
    (norm1, w_in, gla_a_w_f, gla_a_b_f, gla_a_w_b, gla_a_b_b, gla_norm, gla_w_o,
     mla_q_norm, mla_w_uq, mla_kv_norm, mla_w_ukv, mla_w_o, w_out, norm2,
     peer_w_q, peer_k1, peer_k2, peer_u, peer_v, ple_proj, ple_gate, final_norm) = w
    pts = _split_points()
    for l in range(DEPTH):
        h = rmsnorm(x, norm1[l])
        proj = h @ w_in[l]
        q, k, v, r, a_lr, c_q, c_kv, k_r, gate_logits = jnp.split(proj, pts, axis=-1)
        y_a = gla_branch(q, k, v, r, a_lr, gla_a_w_f[l], gla_a_b_f[l], gla_a_w_b[l], gla_a_b_b[l],
                         gla_norm[l], gla_w_o[l])
        y_b = mla_branch(c_q, c_kv, k_r, mla_q_norm[l], mla_w_uq[l], mla_kv_norm[l], mla_w_ukv[l], mla_w_o[l])
        g_a, g_b = jnp.split(jax.nn.sigmoid(gate_logits), 2, axis=-1)
        x = x + (g_a * y_a + g_b * y_b) @ w_out[l]
        x = x + peer(rmsnorm(x, norm2[l]), peer_w_q[l], peer_k1[l], peer_k2[l], peer_u[l], peer_v[l])
        x = x + jax.nn.sigmoid(x @ ple_gate[l]) * (p[l] @ ple_proj[l])
    return rmsnorm(x, final_norm)


def setup_inputs(seed: int = 0) -> dict:
    key = jax.random.key(seed)
    ks = iter(jax.random.split(key, 40))
    nrm = lambda shape, scale: jax.random.normal(next(ks), shape, F32) * scale
    gain = lambda shape: 1.0 + nrm(shape, 0.01)
    L, D = DEPTH, D_MODEL
    return {
        'x_prompt': nrm((BATCH, SEQ, D), 1.0),
        'x_sample': nrm((DEC_BATCH, DEC_SEQ, D), 1.0),
        'p_prompt': nrm((L, BATCH, SEQ, PLE_DIM), 1.0),
        'p_sample': nrm((L, DEC_BATCH, DEC_SEQ, PLE_DIM), 1.0),
        'norm1': gain((L, D)),
        'w_in': nrm((L, D, W_IN_COLS), D ** -0.5),
        'gla_a_w_f': nrm((L, GLA_GATE_RANK, GLA_HEADS * GLA_DK), GLA_GATE_RANK ** -0.5),
        'gla_a_b_f': nrm((L, GLA_HEADS * GLA_DK), 0.1),
        'gla_a_w_b': nrm((L, GLA_GATE_RANK, GLA_HEADS * GLA_DK), GLA_GATE_RANK ** -0.5),
        'gla_a_b_b': nrm((L, GLA_HEADS * GLA_DK), 0.1),
        'gla_norm': gain((L, GLA_DV)),
        'gla_w_o': nrm((L, GLA_HEADS * GLA_DV, D), (GLA_HEADS * GLA_DV) ** -0.5),
        'mla_q_norm': gain((L, MLA_Q_LORA)),
        'mla_w_uq': nrm((L, MLA_Q_LORA, MLA_HEADS * MLA_DQK), MLA_Q_LORA ** -0.5),
        'mla_kv_norm': gain((L, MLA_KV_LORA)),
        'mla_w_ukv': nrm((L, MLA_KV_LORA, MLA_HEADS * (MLA_NOPE + MLA_V)), MLA_KV_LORA ** -0.5),
        'mla_w_o': nrm((L, MLA_HEADS * MLA_V, D), (MLA_HEADS * MLA_V) ** -0.5),
        'w_out': nrm((L, D, D), D ** -0.5),
        'norm2': gain((L, D)),
        'peer_w_q': nrm((L, D, PEER_HEADS * PEER_QDIM), D ** -0.5),
        'peer_k1': nrm((L, PEER_NKEYS, PEER_HALF), PEER_HALF ** -0.5),
        'peer_k2': nrm((L, PEER_NKEYS, PEER_HALF), PEER_HALF ** -0.5),
        'peer_u': nrm((L, PEER_N, D), D ** -0.5),
        'peer_v': nrm((L, PEER_N, D), PEER_HEADS ** -0.5),
        'ple_proj': nrm((L, PLE_DIM, D), PLE_DIM ** -0.5),
        'ple_gate': nrm((L, D, D), D ** -0.5),
        'final_norm': gain((D,)),
    }


def reference(x_prompt, x_sample, p_prompt, p_sample, norm1, w_in, gla_a_w_f, gla_a_b_f, gla_a_w_b,
              gla_a_b_b, gla_norm, gla_w_o, mla_q_norm, mla_w_uq, mla_kv_norm, mla_w_ukv, mla_w_o,
              w_out, norm2, peer_w_q, peer_k1, peer_k2, peer_u, peer_v, ple_proj, ple_gate, final_norm):
    weights = (norm1, w_in, gla_a_w_f, gla_a_b_f, gla_a_w_b, gla_a_b_b, gla_norm, gla_w_o,
               mla_q_norm, mla_w_uq, mla_kv_norm, mla_w_ukv, mla_w_o, w_out, norm2,
               peer_w_q, peer_k1, peer_k2, peer_u, peer_v, ple_proj, ple_gate, final_norm)
    y_prompt = _trunk(x_prompt, p_prompt, weights)
    y_sample = _trunk(x_sample, p_sample, weights)
    return (y_prompt, y_sample)
```

```python
import functools

import jax
import jax.numpy as jnp
from jax import lax
from jax.experimental import pallas as pl
from jax.experimental.pallas import tpu as pltpu

F32 = jnp.float32
BF16 = jnp.bfloat16
EPS = 1e-6

D_MODEL = 1024
PLE_DIM = 256
GLA_HEADS, GLA_DK, GLA_DV, GLA_RANK, GLA_GATE_NORM, GLA_CHUNK = 4, 128, 256, 16, 16.0, 64
MLA_HEADS, MLA_Q_LORA, MLA_KV_LORA, MLA_NOPE, MLA_ROPE, MLA_V = 8, 384, 256, 128, 64, 128
MLA_DQK = MLA_NOPE + MLA_ROPE
MLA_QK_PAD = 256
ROPE_THETA = 10000.0
PEER_HEADS, PEER_NKEYS, PEER_HALF, PEER_TOPK = 8, 128, 128, 16
PEER_N = PEER_NKEYS * PEER_NKEYS

P_COLS = 6144
COL_GATE, COL_V, COL_R, COL_Q, COL_K, COL_CQ, COL_CKV, COL_KR, COL_ALR = (
    0, 2048, 3072, 4096, 4608, 5120, 5632, 5888, 6016)
CQ_PAD = 512

VMEM_LIMIT = 56 * 1024 * 1024


def _cparams(sem):
    return pltpu.CompilerParams(dimension_semantics=sem, vmem_limit_bytes=VMEM_LIMIT)


def _dot(a, b):
    return jnp.dot(a, b, preferred_element_type=F32)


def _dot_nt(a, b):
    return lax.dot_general(a, b, (((1,), (1,)), ((), ())), preferred_element_type=F32)


def _dot_tn(a, b):
    return lax.dot_general(a, b, (((0,), (0,)), ((), ())), preferred_element_type=F32)


def _sigmoid(x):
    return 1.0 / (1.0 + jnp.exp(-x))


def _in_proj_kernel(x_ref, g_ref, w_ref, o_ref, h_ref):
    @pl.when(pl.program_id(1) == 0)
    def _():
        x = x_ref[...]
        ms = jnp.mean(x * x, axis=-1, keepdims=True)
        h_ref[...] = (x * lax.rsqrt(ms + EPS) * g_ref[...]).astype(BF16)

    o_ref[...] = _dot(h_ref[...], w_ref[...])


def _in_proj(x, g, w, tm, tn):
    T = x.shape[0]
    return pl.pallas_call(
        _in_proj_kernel,
        grid=(T // tm, P_COLS // tn),
        in_specs=[pl.BlockSpec((tm, D_MODEL), lambda i, j: (i, 0)),
                  pl.BlockSpec((1, D_MODEL), lambda i, j: (0, 0)),
                  pl.BlockSpec((D_MODEL, tn), lambda i, j: (0, j))],
        out_specs=pl.BlockSpec((tm, tn), lambda i, j: (i, j)),
        out_shape=jax.ShapeDtypeStruct((T, P_COLS), F32),
        scratch_shapes=[pltpu.VMEM((tm, D_MODEL), BF16)],
        compiler_params=_cparams(("parallel", "arbitrary")),
        name="in_proj",
    )(x, g, w)


def _gla_direction(q_ref, k_ref, v_ref, a_ref, aw_ref, ab_ref, st_ref, o_ref, a_off, backward, R):
    C = GLA_CHUNK
    nc = R // C
    row = lax.broadcasted_iota(jnp.int32, (R, R), 0)
    col = lax.broadcasted_iota(jnp.int32, (R, R), 1)
    same = (row // C) == (col // C)
    cum = same & ((col >= row) if backward else (col <= row))
    att_mask = same & ((col > row) if backward else (col <= row))
    tri = jnp.where(cum, 1.0, 0.0).astype(BF16)
    blk = jnp.where(same, 1.0, 0.0).astype(BF16)
    a = a_ref[:, a_off:a_off + GLA_RANK].astype(BF16)
    order = range(nc - 1, -1, -1) if backward else range(nc)
    for h in range(GLA_HEADS):
        ks = slice(h * GLA_DK, (h + 1) * GLA_DK)
        vs = slice(h * GLA_DV, (h + 1) * GLA_DV)
        z = _dot(a, aw_ref[:, ks].astype(BF16)) + ab_ref[:, ks]
        la = (jnp.minimum(z, 0.0) - jnp.log(1.0 + jnp.exp(-jnp.abs(z)))) / GLA_GATE_NORM
        la_hi = la.astype(BF16)
        la_lo = (la - la_hi.astype(F32)).astype(BF16)
        b = _dot(tri, la_hi) + _dot(tri, la_lo)
        tot = _dot(blk, la_hi) + _dot(blk, la_lo)
        qh = q_ref[:, ks] * (GLA_DK ** -0.5)
        kh = k_ref[:, ks]
        vh = v_ref[:, vs].astype(BF16)
        q_e = (qh * jnp.exp(b)).astype(BF16)
        k_e = (kh * jnp.exp(-b)).astype(BF16)
        k_s = (kh * jnp.exp(tot - b)).astype(BF16)
        dec = jnp.exp(tot)
        att = jnp.where(att_mask, _dot_nt(q_e, k_e), 0.0).astype(BF16)
        o_intra = _dot(att, vh)
        for c in order:
            rs = slice(c * C, (c + 1) * C)
            st = st_ref[h]
            o_ref[rs, vs] = o_intra[rs] + _dot_nt(q_e[rs], st.astype(BF16))
            st_ref[h] = dec[c * C:c * C + 1, :] * st + _dot_tn(vh[rs], k_s[rs])


def _gla_kernel(qf_ref, kf_ref, vf_ref, af_ref, qb_ref, kb_ref, vb_ref, ab_ref,
                awf_ref, abf_ref, awb_ref, abb_ref, of_ref, ob_ref, stf_ref, stb_ref, *, R):
    @pl.when(pl.program_id(1) == 0)
    def _():
        stf_ref[...] = jnp.zeros_like(stf_ref)
        stb_ref[...] = jnp.zeros_like(stb_ref)

    _gla_direction(qf_ref, kf_ref, vf_ref, af_ref, awf_ref, abf_ref, stf_ref, of_ref, 0, False, R)
    _gla_direction(qb_ref, kb_ref, vb_ref, ab_ref, awb_ref, abb_ref, stb_ref, ob_ref, GLA_RANK, True, R)


def _gla(P, awf, abf, awb, abb, Bt, S, R):
    T = Bt * S
    NB = S // R
    HK, HV = GLA_HEADS * GLA_DK, GLA_HEADS * GLA_DV
    fwd = lambda b, n: b * NB + n
    bwd = lambda b, n: b * NB + (NB - 1 - n)

    def specs(rowf):
        return [pl.BlockSpec((R, HK), lambda b, n: (rowf(b, n), COL_Q // HK)),
                pl.BlockSpec((R, HK), lambda b, n: (rowf(b, n), COL_K // HK)),
                pl.BlockSpec((R, HV), lambda b, n: (rowf(b, n), COL_V // HV)),
                pl.BlockSpec((R, 128), lambda b, n: (rowf(b, n), COL_ALR // 128))]

    wspec = lambda shape: pl.BlockSpec(shape, lambda b, n: (0, 0))
    return pl.pallas_call(
        functools.partial(_gla_kernel, R=R),
        grid=(Bt, NB),
        in_specs=specs(fwd) + specs(bwd) + [wspec((GLA_RANK, HK)), wspec((1, HK)),
                                            wspec((GLA_RANK, HK)), wspec((1, HK))],
        out_specs=[pl.BlockSpec((R, HV), lambda b, n: (fwd(b, n), 0)),
                   pl.BlockSpec((R, HV), lambda b, n: (bwd(b, n), 0))],
        out_shape=[jax.ShapeDtypeStruct((T, HV), F32), jax.ShapeDtypeStruct((T, HV), F32)],
        scratch_shapes=[pltpu.VMEM((GLA_HEADS, GLA_DV, GLA_DK), F32),
                        pltpu.VMEM((GLA_HEADS, GLA_DV, GLA_DK), F32)],
        compiler_params=_cparams(("parallel", "arbitrary")),
        name="gla",
    )(P, P, P, P, P, P, P, P, awf, abf, awb, abb)


def _mla_prep_kernel(cq_ref, ckv_ref, kr_ref, cos_ref, sin_ref, qn_ref, wuq_ref, kvn_ref, wukv_ref,
                     q_out, k_out, v_out):
    cq = cq_ref[...]
    ms = jnp.sum(cq * cq, axis=-1, keepdims=True) * (1.0 / MLA_Q_LORA)
    hq = (cq * lax.rsqrt(ms + EPS) * qn_ref[...]).astype(BF16)
    qq = _dot(hq, wuq_ref[...])
    ckv = ckv_ref[...]
    ms = jnp.mean(ckv * ckv, axis=-1, keepdims=True)
    hkv = (ckv * lax.rsqrt(ms + EPS) * kvn_ref[...]).astype(BF16)
    kv = _dot(hkv, wukv_ref[...])
    cos = cos_ref[...]
    sin = sin_ref[...]
    k_pe = (kr_ref[:, :MLA_ROPE] * cos + kr_ref[:, MLA_ROPE:] * sin).astype(BF16)
    scale = MLA_DQK ** -0.5
    zpad = jnp.zeros((cq.shape[0], MLA_QK_PAD - MLA_DQK), BF16)
    for h in range(MLA_HEADS):
        o = h * 256
        q_pe = qq[:, o + 128:o + 192] * cos + qq[:, o + 192:o + 256] * sin
        q_out[0, h, :, 0:MLA_NOPE] = (qq[:, o:o + MLA_NOPE] * scale).astype(BF16)
        q_out[0, h, :, MLA_NOPE:MLA_DQK] = (q_pe * scale).astype(BF16)
        q_out[0, h, :, MLA_DQK:MLA_QK_PAD] = zpad
        k_out[0, h, :, 0:MLA_NOPE] = kv[:, o:o + MLA_NOPE].astype(BF16)
        k_out[0, h, :, MLA_NOPE:MLA_DQK] = k_pe
        k_out[0, h, :, MLA_DQK:MLA_QK_PAD] = zpad
        v_out[0, h] = kv[:, o + MLA_NOPE:o + 256].astype(BF16)


def _mla_prep(P, cos2, sin2, qn, wuq, kvn, wukv, Bt, S, ts):
    NS = S // ts
    row = lambda b, i: b * NS + i
    wspec = lambda shape: pl.BlockSpec(shape, lambda b, i: (0, 0))
    H = MLA_HEADS
    return pl.pallas_call(
        _mla_prep_kernel,
        grid=(Bt, NS),
        in_specs=[pl.BlockSpec((ts, CQ_PAD), lambda b, i: (row(b, i), COL_CQ // CQ_PAD)),
                  pl.BlockSpec((ts, MLA_KV_LORA), lambda b, i: (row(b, i), COL_CKV // MLA_KV_LORA)),
                  pl.BlockSpec((ts, 128), lambda b, i: (row(b, i), COL_KR // 128)),
                  pl.BlockSpec((ts, MLA_ROPE), lambda b, i: (i, 0)),
                  pl.BlockSpec((ts, MLA_ROPE), lambda b, i: (i, 0)),
                  wspec((1, CQ_PAD)), wspec((CQ_PAD, H * 256)),
                  wspec((1, MLA_KV_LORA)), wspec((MLA_KV_LORA, H * 256))],
        out_specs=[pl.BlockSpec((1, H, ts, MLA_QK_PAD), lambda b, i: (b, 0, i, 0)),
                   pl.BlockSpec((1, H, ts, MLA_QK_PAD), lambda b, i: (b, 0, i, 0)),
                   pl.BlockSpec((1, H, ts, MLA_V), lambda b, i: (b, 0, i, 0))],
        out_shape=[jax.ShapeDtypeStruct((Bt, H, S, MLA_QK_PAD), BF16),
                   jax.ShapeDtypeStruct((Bt, H, S, MLA_QK_PAD), BF16),
                   jax.ShapeDtypeStruct((Bt, H, S, MLA_V), BF16)],
        compiler_params=_cparams(("parallel", "parallel")),
        name="mla_prep",
    )(P, P, P, cos2, sin2, qn, wuq, kvn, wukv)


def _attn_kernel(q_ref, k_ref, v_ref, o_ref, m_ref, l_ref, acc_ref):
    kv = pl.program_id(3)

    @pl.when(kv == 0)
    def _():
        m_ref[...] = jnp.full_like(m_ref, -jnp.inf)
        l_ref[...] = jnp.zeros_like(l_ref)
        acc_ref[...] = jnp.zeros_like(acc_ref)

    s = _dot_nt(q_ref[0, 0], k_ref[0, 0])
    m_prev = m_ref[...]
    m_new = jnp.maximum(m_prev, jnp.max(s, axis=-1, keepdims=True))
    alpha = jnp.exp(m_prev - m_new)
    p = jnp.exp(s - m_new)
    l_ref[...] = alpha * l_ref[...] + jnp.sum(p, axis=-1, keepdims=True)
    acc_ref[...] = alpha * acc_ref[...] + _dot(p.astype(BF16), v_ref[0, 0])
    m_ref[...] = m_new

    @pl.when(kv == pl.num_programs(3) - 1)
    def _():
        o_ref[0] = (acc_ref[...] / l_ref[...]).astype(o_ref.dtype)


def _attention(Q, K, V, tq, tk):
    Bt, H, S, _ = Q.shape
    return pl.pallas_call(
        _attn_kernel,
        grid=(Bt, H, S // tq, S // tk),
        in_specs=[pl.BlockSpec((1, 1, tq, MLA_QK_PAD), lambda b, h, i, j: (b, h, i, 0)),
                  pl.BlockSpec((1, 1, tk, MLA_QK_PAD), lambda b, h, i, j: (b, h, j, 0)),
                  pl.BlockSpec((1, 1, tk, MLA_V), lambda b, h, i, j: (b, h, j, 0))],
        out_specs=pl.BlockSpec((1, tq, MLA_V), lambda b, h, i, j: (b, i, h)),
        out_shape=jax.ShapeDtypeStruct((Bt, S, H * MLA_V), BF16),
        scratch_shapes=[pltpu.VMEM((tq, 1), F32), pltpu.VMEM((tq, 1), F32),
                        pltpu.VMEM((tq, MLA_V), F32)],
        compiler_params=_cparams(("parallel", "parallel", "parallel", "arbitrary")),
        name="attn",
    )(Q, K, V)


def _merge_kernel(x_ref, of_ref, ob_ref, r_ref, gate_ref, oat_ref, gn_ref, wgo_ref, wmo_ref, wout_ref,
                  n2_ref, wq_ref, x1_ref, h2_ref, qp_ref):
    o = of_ref[...] + ob_ref[...]
    gn = gn_ref[...]
    parts = []
    for h in range(GLA_HEADS):
        vs = slice(h * GLA_DV, (h + 1) * GLA_DV)
        oh = o[:, vs]
        ms = jnp.mean(oh * oh, axis=-1, keepdims=True)
        parts.append(oh * lax.rsqrt(ms + EPS) * gn[:, vs])
    on = jnp.concatenate(parts, axis=-1)
    r = r_ref[...]
    ya = _dot((on * (r * _sigmoid(r))).astype(BF16), wgo_ref[...])
    yb = _dot(oat_ref[...], wmo_ref[...])
    g = _sigmoid(gate_ref[...])
    mix = g[:, :D_MODEL] * ya + g[:, D_MODEL:] * yb
    x1 = x_ref[...] + _dot(mix.astype(BF16), wout_ref[...])
    x1_ref[...] = x1
    ms = jnp.mean(x1 * x1, axis=-1, keepdims=True)
    h2 = (x1 * lax.rsqrt(ms + EPS) * n2_ref[...]).astype(BF16)
    h2_ref[...] = h2
    qp_ref[...] = _dot(h2, wq_ref[...]).astype(BF16)


def _merge(x, o_f, o_b, P, o_att, gn, wgo, wmo, wout, n2, wq, tm):
    T = x.shape[0]
    D = D_MODEL
    QW = PEER_HEADS * 2 * PEER_HALF
    tile = lambda w, c=0: pl.BlockSpec((tm, w), lambda i: (i, c))
    wspec = lambda shape: pl.BlockSpec(shape, lambda i: (0, 0))
    return pl.pallas_call(
        _merge_kernel,
        grid=(T // tm,),
        in_specs=[tile(D), tile(D), tile(D), tile(D, COL_R // D), tile(2 * D, COL_GATE // (2 * D)), tile(D),
                  wspec((1, D)), wspec((D, D)), wspec((D, D)), wspec((D, D)), wspec((1, D)), wspec((D, QW))],
        out_specs=[tile(D), tile(D), tile(QW)],
        out_shape=[jax.ShapeDtypeStruct((T, D), F32), jax.ShapeDtypeStruct((T, D), BF16),
                   jax.ShapeDtypeStruct((T, QW), BF16)],
        compiler_params=_cparams(("parallel",)),
        name="merge",
    )(x, o_f, o_b, P, P, o_att, gn, wgo, wmo, wout, n2, wq)


def _top16(s):
    n, t = s.shape
    idx = lax.broadcasted_iota(jnp.int32, (n, t), 0)
    slot = lax.broadcasted_iota(jnp.int32, (PEER_TOPK, t), 0)
    vals = jnp.zeros((PEER_TOPK, t), F32)
    rank = jnp.full((n, t), PEER_TOPK, jnp.int32)
    cur = s
    for a in range(PEER_TOPK):
        m = jnp.max(cur, axis=0, keepdims=True)
        first = jnp.min(jnp.where(cur == m, idx, n), axis=0, keepdims=True)
        sel = idx == first
        vals = jnp.where(slot == a, m, vals)
        rank = jnp.where(sel, a, rank)
        cur = jnp.where(sel, -jnp.inf, cur)
    return vals, rank


def _peer_topk_kernel(qp_ref, k1_ref, k2_ref, r1_ref, f1_ref, k2r_ref, f2_ref):
    K = PEER_TOPK
    tt = qp_ref.shape[0]
    k1 = k1_ref[...]
    k2 = k2_ref[...]
    for h in range(PEER_HEADS):
        o = h * 2 * PEER_HALF
        s1 = _dot_nt(k1, qp_ref[:, o:o + PEER_HALF])
        s2 = _dot_nt(k2, qp_ref[:, o + PEER_HALF:o + 2 * PEER_HALF])
        t1, rank1 = _top16(s1)
        t2, rank2 = _top16(s2)
        slabs, flats, rows = [], [], []
        for a in range(K):
            nb = K // (a + 1)
            nr = 16 if a == 0 else 8
            brow = lax.broadcasted_iota(jnp.int32, (nr, tt), 0)
            c = t1[a:a + 1, :] + t2[0:nr, :]
            slabs.append(jnp.where(brow < nb, c, -jnp.inf))
            flats.append(brow + a * K)
            rows.append(nr)
        cand = jnp.concatenate(slabs, axis=0)
        flat = jnp.concatenate(flats, axis=0)
        chosen = jnp.zeros(cand.shape, F32)
        cur = cand
        for _ in range(K):
            m = jnp.max(cur, axis=0, keepdims=True)
            first = jnp.min(jnp.where(cur == m, flat, K * K), axis=0, keepdims=True)
            sel = flat == first
            chosen = jnp.where(sel, 1.0, chosen)
            cur = jnp.where(sel, -jnp.inf, cur)
        e1s = jnp.exp(t1 - t1[0:1, :])
        e2s = jnp.exp(t2 - t2[0:1, :])
        z = jnp.zeros((1, tt), F32)
        r1 = jnp.zeros(rank1.shape, F32)
        off = 0
        for a in range(K):
            nr = rows[a]
            ch = chosen[off:off + nr, :]
            off += nr
            cnt = jnp.sum(ch, axis=0, keepdims=True)
            z = z + e1s[a:a + 1, :] * jnp.sum(ch * e2s[0:nr, :], axis=0, keepdims=True)
            r1 = jnp.where(rank1 == a, cnt, r1)
        r1_ref[h] = r1
        f1_ref[h] = jnp.exp(s1 - t1[0:1, :]) / z
        k2r_ref[h] = rank2.astype(F32)
        f2_ref[h] = jnp.exp(s2 - t2[0:1, :])


def _peer_topk(qp, k1, k2, tt):
    T = qp.shape[0]
    H, N = PEER_HEADS, PEER_NKEYS
    out = jax.ShapeDtypeStruct((H, N, T), F32)
    ospec = pl.BlockSpec((H, N, tt), lambda i: (0, 0, i))
    return pl.pallas_call(
        _peer_topk_kernel,
        grid=(T // tt,),
        in_specs=[pl.BlockSpec((tt, H * 2 * PEER_HALF), lambda i: (i, 0)),
                  pl.BlockSpec((N, PEER_HALF), lambda i: (0, 0)),
                  pl.BlockSpec((N, PEER_HALF), lambda i: (0, 0))],
        out_specs=[ospec, ospec, ospec, ospec],
        out_shape=[out, out, out, out],
        compiler_params=_cparams(("parallel",)),
        name="peer_topk",
    )(qp, k1, k2)


def _peer_main_kernel(h2_ref, u_ref, vt_ref, r1_ref, f1_ref, k2r_ref, f2_ref, x1_ref, o_ref,
                      acc_ref, g_ref, *, ec):
    j = pl.program_id(1)

    @pl.when(j == 0)
    def _():
        acc_ref[...] = jnp.zeros_like(acc_ref)

    N = PEER_NKEYS
    s = _dot_nt(u_ref[...], h2_ref[...])
    for e in range(ec):
        se = s[e * N:(e + 1) * N, :]
        w = jnp.zeros(se.shape, F32)
        for h in range(PEER_HEADS):
            sel = k2r_ref[h] < r1_ref[h, e:e + 1, :]
            w = w + jnp.where(sel, f2_ref[h], 0.0) * f1_ref[h, e:e + 1, :]
        act = 0.5 * se * (1.0 + lax.erf(se * (2.0 ** -0.5)))
        g_ref[e * N:(e + 1) * N, :] = (act * w).astype(BF16)
    acc_ref[...] += _dot(vt_ref[...], g_ref[...])

    @pl.when(j == pl.num_programs(1) - 1)
    def _():
        o_ref[...] = x1_ref[...] + acc_ref[...].T


def _peer_main(h2, u, vt, r1, f1, k2r, f2, x1, tt, ec):
    T = h2.shape[0]
    D, H, N = D_MODEL, PEER_HEADS, PEER_NKEYS
    return pl.pallas_call(
        functools.partial(_peer_main_kernel, ec=ec),
        grid=(T // tt, N // ec),
        in_specs=[pl.BlockSpec((tt, D), lambda i, j: (i, 0)),
                  pl.BlockSpec((ec * N, D), lambda i, j: (j, 0)),
                  pl.BlockSpec((D, ec * N), lambda i, j: (0, j)),
                  pl.BlockSpec((H, ec, tt), lambda i, j: (0, j, i)),
                  pl.BlockSpec((H, ec, tt), lambda i, j: (0, j, i)),
                  pl.BlockSpec((H, N, tt), lambda i, j: (0, 0, i)),
                  pl.BlockSpec((H, N, tt), lambda i, j: (0, 0, i)),
                  pl.BlockSpec((tt, D), lambda i, j: (i, 0))],
        out_specs=pl.BlockSpec((tt, D), lambda i, j: (i, 0)),
        out_shape=jax.ShapeDtypeStruct((T, D), F32),
        scratch_shapes=[pltpu.VMEM((D, tt), F32), pltpu.VMEM((ec * N, tt), BF16)],
        compiler_params=_cparams(("parallel", "arbitrary")),
        name="peer_main",
    )(h2, u, vt, r1, f1, k2r, f2, x1)


def _final_kernel(x_ref, p_ref, wg_ref, wp_ref, fn_ref, o_ref):
    x = x_ref[...]
    gate = _sigmoid(_dot(x.astype(BF16), wg_ref[...]))
    x = x + gate * _dot(p_ref[...].astype(BF16), wp_ref[...])
    ms = jnp.mean(x * x, axis=-1, keepdims=True)
    o_ref[...] = x * lax.rsqrt(ms + EPS) * fn_ref[...]


def _final(x, p, wg, wp, fn, tm):
    T = x.shape[0]
    D = D_MODEL
    return pl.pallas_call(
        _final_kernel,
        grid=(T // tm,),
        in_specs=[pl.BlockSpec((tm, D), lambda i: (i, 0)),
                  pl.BlockSpec((tm, PLE_DIM), lambda i: (i, 0)),
                  pl.BlockSpec((D, D), lambda i: (0, 0)),
                  pl.BlockSpec((PLE_DIM, D), lambda i: (0, 0)),
                  pl.BlockSpec((1, D), lambda i: (0, 0))],
        out_specs=pl.BlockSpec((tm, D), lambda i: (i, 0)),
        out_shape=jax.ShapeDtypeStruct((T, D), F32),
        compiler_params=_cparams(("parallel",)),
        name="final",
    )(x, p, wg, wp, fn)


def _rot_cols(w):
    half = MLA_ROPE // 2
    return jnp.concatenate([-w[..., half:], w[..., :half]], axis=-1)


def _pack_w_in(w_in):
    pts, acc = [], 0
    for sz in (512, 512, 1024, 1024, 2 * GLA_RANK, MLA_Q_LORA, MLA_KV_LORA, MLA_ROPE):
        acc += sz
        pts.append(acc)
    q, k, v, r, a_lr, c_q, c_kv, k_r, gate = jnp.split(w_in, pts, axis=-1)
    z = lambda n: jnp.zeros((D_MODEL, n), w_in.dtype)
    packed = jnp.concatenate([gate, v, r, q, k, c_q, z(CQ_PAD - MLA_Q_LORA), c_kv, k_r, _rot_cols(k_r),
                              a_lr, z(128 - 2 * GLA_RANK)], axis=-1)
    assert packed.shape[-1] == P_COLS
    return packed.astype(BF16)


def _pack_w_uq(w_uq):
    w = w_uq.reshape(MLA_Q_LORA, MLA_HEADS, MLA_DQK)
    pe = w[..., MLA_NOPE:]
    w = jnp.concatenate([w, _rot_cols(pe)], axis=-1).reshape(MLA_Q_LORA, MLA_HEADS * 256)
    return jnp.concatenate([w, jnp.zeros((CQ_PAD - MLA_Q_LORA, MLA_HEADS * 256), w.dtype)], axis=0).astype(BF16)


def _tile(n, pref):
    return pref if n % pref == 0 else n


def kernel(x_prompt, x_sample, p_prompt, p_sample, norm1, w_in, gla_a_w_f, gla_a_b_f, gla_a_w_b, gla_a_b_b, gla_norm, gla_w_o, mla_q_norm, mla_w_uq, mla_kv_norm, mla_w_ukv, mla_w_o, w_out, norm2, peer_w_q, peer_k1, peer_k2, peer_u, peer_v, ple_proj, ple_gate, final_norm):
    B1, S, D = x_prompt.shape
    B2 = x_sample.shape[0]
    assert x_sample.shape[1] == S and norm1.shape[0] == 1, "one layer, equal sequence lengths"
    Bt = B1 + B2
    T = Bt * S
    x = jnp.concatenate([x_prompt, x_sample], axis=0).reshape(T, D)
    p = jnp.concatenate([p_prompt[0], p_sample[0]], axis=0).reshape(T, PLE_DIM)
    row = lambda v: v.reshape(1, -1).astype(F32)

    P = _in_proj(x, row(norm1[0]), _pack_w_in(w_in[0]), _tile(T, 512), 1024)

    o_f, o_b = _gla(P, gla_a_w_f[0], row(gla_a_b_f[0]), gla_a_w_b[0], row(gla_a_b_b[0]), Bt, S, _tile(S, 256))

    pos = jnp.arange(S, dtype=F32)
    inv = ROPE_THETA ** (-jnp.arange(MLA_ROPE // 2, dtype=F32) * 2.0 / MLA_ROPE)
    ang = pos[:, None] * inv[None, :]
    cos2 = jnp.concatenate([jnp.cos(ang), jnp.cos(ang)], axis=-1)
    sin2 = jnp.concatenate([jnp.sin(ang), jnp.sin(ang)], axis=-1)
    qn = jnp.concatenate([mla_q_norm[0], jnp.zeros((CQ_PAD - MLA_Q_LORA,), F32)]).reshape(1, CQ_PAD)
    Q, K, V = _mla_prep(P, cos2, sin2, qn, _pack_w_uq(mla_w_uq[0]), row(mla_kv_norm[0]),
                        mla_w_ukv[0].astype(BF16), Bt, S, _tile(S, 512))
    o_att = _attention(Q, K, V, _tile(S, 512), _tile(S, 1024)).reshape(T, MLA_HEADS * MLA_V)

    gn = jnp.tile(gla_norm[0], GLA_HEADS).reshape(1, D)
    x1, h2, qp = _merge(x, o_f, o_b, P, o_att, gn, gla_w_o[0].astype(BF16), mla_w_o[0].astype(BF16),
                        w_out[0].astype(BF16), row(norm2[0]), peer_w_q[0].astype(BF16), _tile(T, 256))

    r1, f1, k2r, f2 = _peer_topk(qp, peer_k1[0].astype(BF16), peer_k2[0].astype(BF16), _tile(T, 256))
    x2 = _peer_main(h2, peer_u[0].astype(BF16), peer_v[0].T.astype(BF16), r1, f1, k2r, f2, x1,
                    _tile(T, 512), 8)

    y = _final(x2, p, ple_gate[0].astype(BF16), ple_proj[0].astype(BF16), row(final_norm), _tile(T, 512))
    y = y.reshape(Bt, S, D)
    return (y[:B1], y[B1:])
```

```python
import functools

import jax
import jax.numpy as jnp
from jax import lax
from jax.experimental import pallas as pl
from jax.experimental.pallas import tpu as pltpu

F32 = jnp.float32
BF16 = jnp.bfloat16
EPS = 1e-6

D_MODEL = 1024
PLE_DIM = 256
GLA_HEADS, GLA_DK, GLA_DV, GLA_RANK, GLA_GATE_NORM, GLA_CHUNK = 4, 128, 256, 16, 16.0, 64
MLA_HEADS, MLA_Q_LORA, MLA_KV_LORA, MLA_NOPE, MLA_ROPE, MLA_V = 8, 384, 256, 128, 64, 128
MLA_DQK = MLA_NOPE + MLA_ROPE
MLA_QK_PAD = 256
ROPE_THETA = 10000.0
LOG2E = 1.4426950408889634
PEER_HEADS, PEER_NKEYS, PEER_HALF, PEER_TOPK = 8, 128, 128, 16
PEER_N = PEER_NKEYS * PEER_NKEYS

P_COLS = 6144
COL_GATE, COL_V, COL_R, COL_Q, COL_K, COL_CQ, COL_CKV, COL_KR, COL_ALR = (
    0, 2048, 3072, 4096, 4608, 5120, 5632, 5888, 6016)
CQ_PAD = 512

VMEM_LIMIT = 56 * 1024 * 1024


def _cparams(sem):
    return pltpu.CompilerParams(dimension_semantics=sem, vmem_limit_bytes=VMEM_LIMIT)


def _dot(a, b):
    return jnp.dot(a, b, preferred_element_type=F32)


def _dot_nt(a, b):
    return lax.dot_general(a, b, (((1,), (1,)), ((), ())), preferred_element_type=F32)


def _dot_tn(a, b):
    return lax.dot_general(a, b, (((0,), (0,)), ((), ())), preferred_element_type=F32)


def _sigmoid(x):
    return 1.0 / (1.0 + jnp.exp(-x))


def _in_proj_kernel(x_ref, g_ref, w_ref, o_ref, h_ref):
    @pl.when(pl.program_id(1) == 0)
    def _():
        x = x_ref[...]
        ms = jnp.mean(x * x, axis=-1, keepdims=True)
        h_ref[...] = (x * lax.rsqrt(ms + EPS) * g_ref[...]).astype(BF16)

    o_ref[...] = _dot(h_ref[...], w_ref[...])


def _in_proj(x, g, w, tm, tn):
    T = x.shape[0]
    return pl.pallas_call(
        _in_proj_kernel,
        grid=(T // tm, P_COLS // tn),
        in_specs=[pl.BlockSpec((tm, D_MODEL), lambda i, j: (i, 0)),
                  pl.BlockSpec((1, D_MODEL), lambda i, j: (0, 0)),
                  pl.BlockSpec((D_MODEL, tn), lambda i, j: (0, j))],
        out_specs=pl.BlockSpec((tm, tn), lambda i, j: (i, j)),
        out_shape=jax.ShapeDtypeStruct((T, P_COLS), F32),
        scratch_shapes=[pltpu.VMEM((tm, D_MODEL), BF16)],
        compiler_params=_cparams(("parallel", "arbitrary")),
        name="in_proj",
    )(x, g, w)


def _gla_direction(q_ref, k_ref, v_ref, a_ref, aw_ref, ab_ref, st_ref, o_ref, a_off, backward, R):
    C = GLA_CHUNK
    nc = R // C
    row = lax.broadcasted_iota(jnp.int32, (R, R), 0)
    col = lax.broadcasted_iota(jnp.int32, (R, R), 1)
    same = (row // C) == (col // C)
    cum = same & ((col >= row) if backward else (col <= row))
    att_mask = same & ((col > row) if backward else (col <= row))
    tri = jnp.where(cum, 1.0, 0.0).astype(BF16)
    blk = jnp.where(same, 1.0, 0.0).astype(BF16)
    a = a_ref[:, a_off:a_off + GLA_RANK].astype(BF16)
    order = range(nc - 1, -1, -1) if backward else range(nc)
    for h in range(GLA_HEADS):
        ks = slice(h * GLA_DK, (h + 1) * GLA_DK)
        vs = slice(h * GLA_DV, (h + 1) * GLA_DV)
        z = _dot(a, aw_ref[:, ks].astype(BF16)) + ab_ref[:, ks]
        la = (jnp.minimum(z, 0.0) - jnp.log(1.0 + jnp.exp(-jnp.abs(z)))) / GLA_GATE_NORM
        la_hi = la.astype(BF16)
        la_lo = (la - la_hi.astype(F32)).astype(BF16)
        b = _dot(tri, la_hi) + _dot(tri, la_lo)
        tot = _dot(blk, la_hi) + _dot(blk, la_lo)
        qh = q_ref[:, ks] * (GLA_DK ** -0.5)
        kh = k_ref[:, ks]
        vh = v_ref[:, vs].astype(BF16)
        q_e = (qh * jnp.exp(b)).astype(BF16)
        k_e = (kh * jnp.exp(-b)).astype(BF16)
        k_s = (kh * jnp.exp(tot - b)).astype(BF16)
        dec = jnp.exp(tot)
        att = jnp.where(att_mask, _dot_nt(q_e, k_e), 0.0).astype(BF16)
        o_intra = _dot(att, vh)
        for c in order:
            rs = slice(c * C, (c + 1) * C)
            st = st_ref[h]
            o_ref[rs, vs] = o_intra[rs] + _dot_nt(q_e[rs], st.astype(BF16))
            st_ref[h] = dec[c * C:c * C + 1, :] * st + _dot_tn(vh[rs], k_s[rs])


def _gla_kernel(qf_ref, kf_ref, vf_ref, af_ref, qb_ref, kb_ref, vb_ref, ab_ref,
                awf_ref, abf_ref, awb_ref, abb_ref, of_ref, ob_ref, stf_ref, stb_ref, *, R):
    @pl.when(pl.program_id(1) == 0)
    def _():
        stf_ref[...] = jnp.zeros_like(stf_ref)
        stb_ref[...] = jnp.zeros_like(stb_ref)

    _gla_direction(qf_ref, kf_ref, vf_ref, af_ref, awf_ref, abf_ref, stf_ref, of_ref, 0, False, R)
    _gla_direction(qb_ref, kb_ref, vb_ref, ab_ref, awb_ref, abb_ref, stb_ref, ob_ref, GLA_RANK, True, R)


def _gla(P, awf, abf, awb, abb, Bt, S, R):
    T = Bt * S
    NB = S // R
    HK, HV = GLA_HEADS * GLA_DK, GLA_HEADS * GLA_DV
    fwd = lambda b, n: b * NB + n
    bwd = lambda b, n: b * NB + (NB - 1 - n)

    def specs(rowf):
        return [pl.BlockSpec((R, HK), lambda b, n: (rowf(b, n), COL_Q // HK)),
                pl.BlockSpec((R, HK), lambda b, n: (rowf(b, n), COL_K // HK)),
                pl.BlockSpec((R, HV), lambda b, n: (rowf(b, n), COL_V // HV)),
                pl.BlockSpec((R, 128), lambda b, n: (rowf(b, n), COL_ALR // 128))]

    wspec = lambda shape: pl.BlockSpec(shape, lambda b, n: (0, 0))
    return pl.pallas_call(
        functools.partial(_gla_kernel, R=R),
        grid=(Bt, NB),
        in_specs=specs(fwd) + specs(bwd) + [wspec((GLA_RANK, HK)), wspec((1, HK)),
                                            wspec((GLA_RANK, HK)), wspec((1, HK))],
        out_specs=[pl.BlockSpec((R, HV), lambda b, n: (fwd(b, n), 0)),
                   pl.BlockSpec((R, HV), lambda b, n: (bwd(b, n), 0))],
        out_shape=[jax.ShapeDtypeStruct((T, HV), F32), jax.ShapeDtypeStruct((T, HV), F32)],
        scratch_shapes=[pltpu.VMEM((GLA_HEADS, GLA_DV, GLA_DK), F32),
                        pltpu.VMEM((GLA_HEADS, GLA_DV, GLA_DK), F32)],
        compiler_params=_cparams(("parallel", "arbitrary")),
        name="gla",
    )(P, P, P, P, P, P, P, P, awf, abf, awb, abb)


def _mla_prep_kernel(cq_ref, ckv_ref, kr_ref, cos_ref, sin_ref, qn_ref, wuq_ref, kvn_ref, wukv_ref,
                     q_out, k_out, v_out):
    cq = cq_ref[...]
    ms = jnp.sum(cq * cq, axis=-1, keepdims=True) * (1.0 / MLA_Q_LORA)
    hq = (cq * lax.rsqrt(ms + EPS) * qn_ref[...]).astype(BF16)
    qq = _dot(hq, wuq_ref[...])
    ckv = ckv_ref[...]
    ms = jnp.mean(ckv * ckv, axis=-1, keepdims=True)
    hkv = (ckv * lax.rsqrt(ms + EPS) * kvn_ref[...]).astype(BF16)
    kv = _dot(hkv, wukv_ref[...])
    cos = cos_ref[...]
    sin = sin_ref[...]
    k_pe = (kr_ref[:, :MLA_ROPE] * cos + kr_ref[:, MLA_ROPE:] * sin).astype(BF16)
    scale = MLA_DQK ** -0.5 * LOG2E
    zpad = jnp.zeros((cq.shape[0], MLA_QK_PAD - MLA_DQK), BF16)
    for h in range(MLA_HEADS):
        o = h * 256
        q_pe = qq[:, o + 128:o + 192] * cos + qq[:, o + 192:o + 256] * sin
        q_out[0, h, :, 0:MLA_NOPE] = (qq[:, o:o + MLA_NOPE] * scale).astype(BF16)
        q_out[0, h, :, MLA_NOPE:MLA_DQK] = (q_pe * scale).astype(BF16)
        q_out[0, h, :, MLA_DQK:MLA_QK_PAD] = zpad
        k_out[0, h, :, 0:MLA_NOPE] = kv[:, o:o + MLA_NOPE].astype(BF16)
        k_out[0, h, :, MLA_NOPE:MLA_DQK] = k_pe
        k_out[0, h, :, MLA_DQK:MLA_QK_PAD] = zpad
        v_out[0, h] = kv[:, o + MLA_NOPE:o + 256].astype(BF16)


def _mla_prep(P, cos2, sin2, qn, wuq, kvn, wukv, Bt, S, ts):
    NS = S // ts
    row = lambda b, i: b * NS + i
    wspec = lambda shape: pl.BlockSpec(shape, lambda b, i: (0, 0))
    H = MLA_HEADS
    return pl.pallas_call(
        _mla_prep_kernel,
        grid=(Bt, NS),
        in_specs=[pl.BlockSpec((ts, CQ_PAD), lambda b, i: (row(b, i), COL_CQ // CQ_PAD)),
                  pl.BlockSpec((ts, MLA_KV_LORA), lambda b, i: (row(b, i), COL_CKV // MLA_KV_LORA)),
                  pl.BlockSpec((ts, 128), lambda b, i: (row(b, i), COL_KR // 128)),
                  pl.BlockSpec((ts, MLA_ROPE), lambda b, i: (i, 0)),
                  pl.BlockSpec((ts, MLA_ROPE), lambda b, i: (i, 0)),
                  wspec((1, CQ_PAD)), wspec((CQ_PAD, H * 256)),
                  wspec((1, MLA_KV_LORA)), wspec((MLA_KV_LORA, H * 256))],
        out_specs=[pl.BlockSpec((1, H, ts, MLA_QK_PAD), lambda b, i: (b, 0, i, 0)),
                   pl.BlockSpec((1, H, ts, MLA_QK_PAD), lambda b, i: (b, 0, i, 0)),
                   pl.BlockSpec((1, H, ts, MLA_V), lambda b, i: (b, 0, i, 0))],
        out_shape=[jax.ShapeDtypeStruct((Bt, H, S, MLA_QK_PAD), BF16),
                   jax.ShapeDtypeStruct((Bt, H, S, MLA_QK_PAD), BF16),
                   jax.ShapeDtypeStruct((Bt, H, S, MLA_V), BF16)],
        compiler_params=_cparams(("parallel", "parallel")),
        name="mla_prep",
    )(P, P, P, cos2, sin2, qn, wuq, kvn, wukv)


ATTN_HEADS_PER_STEP = 4


def _attn_kernel(q_ref, k_ref, v_ref, o_ref, m_ref, l_ref, acc_ref):
    kv = pl.program_id(3)
    nl = k_ref.shape[2] // 128

    @pl.when(kv == 0)
    def _():
        m_ref[...] = jnp.full_like(m_ref, -jnp.inf)
        l_ref[...] = jnp.zeros_like(l_ref)
        acc_ref[...] = jnp.zeros_like(acc_ref)

    for h in range(ATTN_HEADS_PER_STEP):
        s = _dot_nt(q_ref[0, h], k_ref[0, h])
        m_prev = m_ref[h]
        m_new = jnp.maximum(m_prev, jnp.max(s, axis=-1, keepdims=True))
        alpha = jnp.exp2(m_prev - m_new)
        ps = [jnp.exp2(s[:, c * 128:(c + 1) * 128] - m_new) for c in range(nl)]
        psum = ps[0]
        for c in range(1, nl):
            psum = psum + ps[c]
        p = jnp.concatenate([x.astype(BF16) for x in ps], axis=-1)
        l_ref[h] = alpha * l_ref[h] + psum
        acc_ref[h] = alpha * acc_ref[h] + _dot(p, v_ref[0, h])
        m_ref[h] = m_new

    @pl.when(kv == pl.num_programs(3) - 1)
    def _():
        for h in range(ATTN_HEADS_PER_STEP):
            l = jnp.sum(l_ref[h], axis=-1, keepdims=True)
            o_ref[0, :, h * MLA_V:(h + 1) * MLA_V] = (acc_ref[h] / l).astype(o_ref.dtype)


def _attention(Q, K, V, tq, tk):
    Bt, H, S, _ = Q.shape
    G = ATTN_HEADS_PER_STEP
    return pl.pallas_call(
        _attn_kernel,
        grid=(Bt, H // G, S // tq, S // tk),
        in_specs=[pl.BlockSpec((1, G, tq, MLA_QK_PAD), lambda b, h, i, j: (b, h, i, 0)),
                  pl.BlockSpec((1, G, tk, MLA_QK_PAD), lambda b, h, i, j: (b, h, j, 0)),
                  pl.BlockSpec((1, G, tk, MLA_V), lambda b, h, i, j: (b, h, j, 0))],
        out_specs=pl.BlockSpec((1, tq, G * MLA_V), lambda b, h, i, j: (b, i, h)),
        out_shape=jax.ShapeDtypeStruct((Bt, S, H * MLA_V), BF16),
        scratch_shapes=[pltpu.VMEM((G, tq, 128), F32), pltpu.VMEM((G, tq, 128), F32),
                        pltpu.VMEM((G, tq, MLA_V), F32)],
        compiler_params=_cparams(("parallel", "parallel", "parallel", "arbitrary")),
        name="attn",
    )(Q, K, V)


def _merge_kernel(x_ref, of_ref, ob_ref, r_ref, gate_ref, oat_ref, gn_ref, wgo_ref, wmo_ref, wout_ref,
                  n2_ref, wq_ref, x1_ref, h2t_ref, qp_ref):
    o = of_ref[...] + ob_ref[...]
    gn = gn_ref[...]
    parts = []
    for h in range(GLA_HEADS):
        vs = slice(h * GLA_DV, (h + 1) * GLA_DV)
        oh = o[:, vs]
        ms = jnp.mean(oh * oh, axis=-1, keepdims=True)
        parts.append(oh * lax.rsqrt(ms + EPS) * gn[:, vs])
    on = jnp.concatenate(parts, axis=-1)
    r = r_ref[...]
    ya = _dot((on * (r * _sigmoid(r))).astype(BF16), wgo_ref[...])
    yb = _dot(oat_ref[...], wmo_ref[...])
    g = _sigmoid(gate_ref[...])
    mix = g[:, :D_MODEL] * ya + g[:, D_MODEL:] * yb
    x1 = x_ref[...] + _dot(mix.astype(BF16), wout_ref[...])
    x1_ref[...] = x1
    ms = jnp.mean(x1 * x1, axis=-1, keepdims=True)
    h2 = x1 * lax.rsqrt(ms + EPS) * n2_ref[...]
    h2t_ref[...] = h2.T.astype(BF16)
    qp_ref[...] = _dot(h2.astype(BF16), wq_ref[...]).astype(BF16)


def _merge(x, o_f, o_b, P, o_att, gn, wgo, wmo, wout, n2, wq, tm):
    T = x.shape[0]
    D = D_MODEL
    QW = PEER_HEADS * 2 * PEER_HALF
    tile = lambda w, c=0: pl.BlockSpec((tm, w), lambda i: (i, c))
    wspec = lambda shape: pl.BlockSpec(shape, lambda i: (0, 0))
    return pl.pallas_call(
        _merge_kernel,
        grid=(T // tm,),
        in_specs=[tile(D), tile(D), tile(D), tile(D, COL_R // D), tile(2 * D, COL_GATE // (2 * D)), tile(D),
                  wspec((1, D)), wspec((D, D)), wspec((D, D)), wspec((D, D)), wspec((1, D)), wspec((D, QW))],
        out_specs=[tile(D), pl.BlockSpec((D, tm), lambda i: (0, i)), tile(QW)],
        out_shape=[jax.ShapeDtypeStruct((T, D), F32), jax.ShapeDtypeStruct((D, T), BF16),
                   jax.ShapeDtypeStruct((T, QW), BF16)],
        compiler_params=_cparams(("parallel",)),
        name="merge",
    )(x, o_f, o_b, P, P, o_att, gn, wgo, wmo, wout, n2, wq)


def _top16(s):
    n, t = s.shape
    idx = lax.broadcasted_iota(jnp.int32, (n, t), 0)
    slot = lax.broadcasted_iota(jnp.int32, (PEER_TOPK, t), 0)
    vals = jnp.zeros((PEER_TOPK, t), F32)
    rank = jnp.full((n, t), PEER_TOPK, jnp.int32)
    cur = s
    for a in range(PEER_TOPK):
        m = jnp.max(cur, axis=0, keepdims=True)
        first = jnp.min(jnp.where(cur == m, idx, n), axis=0, keepdims=True)
        sel = idx == first
        vals = jnp.where(slot == a, m, vals)
        rank = jnp.where(sel, a, rank)
        cur = jnp.where(sel, -jnp.inf, cur)
    return vals, rank


def _peer_topk_kernel(qp_ref, k1_ref, k2_ref, r1_ref, f1_ref, k2r_ref, f2_ref):
    K = PEER_TOPK
    tt = qp_ref.shape[0]
    k1 = k1_ref[...]
    k2 = k2_ref[...]
    for h in range(PEER_HEADS):
        o = h * 2 * PEER_HALF
        s1 = _dot_nt(k1, qp_ref[:, o:o + PEER_HALF])
        s2 = _dot_nt(k2, qp_ref[:, o + PEER_HALF:o + 2 * PEER_HALF])
        t1, rank1 = _top16(s1)
        t2, rank2 = _top16(s2)
        slabs, flats, rows = [], [], []
        for a in range(K):
            nb = K // (a + 1)
            nr = 16 if a == 0 else 8
            brow = lax.broadcasted_iota(jnp.int32, (nr, tt), 0)
            c = t1[a:a + 1, :] + t2[0:nr, :]
            slabs.append(jnp.where(brow < nb, c, -jnp.inf))
            flats.append(brow + a * K)
            rows.append(nr)
        cand = jnp.concatenate(slabs, axis=0)
        flat = jnp.concatenate(flats, axis=0)
        chosen = jnp.zeros(cand.shape, F32)
        cur = cand
        for _ in range(K):
            m = jnp.max(cur, axis=0, keepdims=True)
            first = jnp.min(jnp.where(cur == m, flat, K * K), axis=0, keepdims=True)
            sel = flat == first
            chosen = jnp.where(sel, 1.0, chosen)
            cur = jnp.where(sel, -jnp.inf, cur)
        e1s = jnp.exp(t1 - t1[0:1, :])
        e2s = jnp.exp(t2 - t2[0:1, :])
        z = jnp.zeros((1, tt), F32)
        r1 = jnp.zeros(rank1.shape, F32)
        off = 0
        for a in range(K):
            nr = rows[a]
            ch = chosen[off:off + nr, :]
            off += nr
            cnt = jnp.sum(ch, axis=0, keepdims=True)
            z = z + e1s[a:a + 1, :] * jnp.sum(ch * e2s[0:nr, :], axis=0, keepdims=True)
            r1 = jnp.where(rank1 == a, cnt, r1)
        r1_ref[h] = r1
        f1_ref[h] = jnp.exp(s1 - t1[0:1, :]) * (0.5 / z)
        k2r_ref[h] = rank2.astype(F32).astype(BF16)
        f2_ref[h] = jnp.exp(s2 - t2[0:1, :]).astype(BF16)


def _peer_topk(qp, k1, k2, tt):
    T = qp.shape[0]
    H, N = PEER_HEADS, PEER_NKEYS
    out = jax.ShapeDtypeStruct((H, N, T), F32)
    outb = jax.ShapeDtypeStruct((H, N, T), BF16)
    ospec = pl.BlockSpec((H, N, tt), lambda i: (0, 0, i))
    return pl.pallas_call(
        _peer_topk_kernel,
        grid=(T // tt,),
        in_specs=[pl.BlockSpec((tt, H * 2 * PEER_HALF), lambda i: (i, 0)),
                  pl.BlockSpec((N, PEER_HALF), lambda i: (0, 0)),
                  pl.BlockSpec((N, PEER_HALF), lambda i: (0, 0))],
        out_specs=[ospec, ospec, ospec, ospec],
        out_shape=[out, out, outb, outb],
        compiler_params=_cparams(("parallel",)),
        name="peer_topk",
    )(qp, k1, k2)


def _peer_main_kernel(h2t_ref, u_ref, vt_ref, r1_ref, f1_ref, k2r_ref, f2_ref, x1_ref, o_ref,
                      acc_ref, s_ref, w_ref, *, ec):
    j = pl.program_id(1)
    N = PEER_NKEYS

    @pl.when(j == 0)
    def _():
        acc_ref[...] = jnp.zeros_like(acc_ref)
        s_ref[1] = jnp.zeros(s_ref.shape[1:], s_ref.dtype)
        w_ref[1] = jnp.zeros(w_ref.shape[1:], w_ref.dtype)

    def step(rd, wr):
        s_ref[wr] = _dot(u_ref[...], h2t_ref[...])
        s = s_ref[rd]
        act = s * (1.0 + lax.erf(s * (2.0 ** -0.5)))
        acc_ref[...] += _dot(vt_ref[...], act.astype(BF16) * w_ref[rd])
        for e in range(ec):
            w = jnp.zeros((N, s.shape[1]), BF16)
            for h in range(PEER_HEADS):
                r1 = r1_ref[h, e:e + 1, :].astype(BF16)
                f1 = f1_ref[h, e:e + 1, :].astype(BF16)
                w = w + jnp.where(k2r_ref[h] < r1, f2_ref[h], jnp.zeros((), BF16)) * f1
            w_ref[wr, e * N:(e + 1) * N, :] = w

    @pl.when(j % 2 == 0)
    def _():
        step(1, 0)

    @pl.when(j % 2 == 1)
    def _():
        step(0, 1)

    @pl.when(j == pl.num_programs(1) - 1)
    def _():
        o_ref[...] = x1_ref[...] + acc_ref[...].T


def _peer_main(h2t, u, vt, r1, f1, k2r, f2, x1, tt, ec):
    T = h2t.shape[1]
    D, H, N = D_MODEL, PEER_HEADS, PEER_NKEYS
    nc = N // ec
    cur = lambda j: jnp.minimum(j, nc - 1)
    prev = lambda j: jnp.maximum(j - 1, 0)
    return pl.pallas_call(
        functools.partial(_peer_main_kernel, ec=ec),
        grid=(T // tt, nc + 1),
        in_specs=[pl.BlockSpec((D, tt), lambda i, j: (0, i)),
                  pl.BlockSpec((ec * N, D), lambda i, j: (cur(j), 0)),
                  pl.BlockSpec((D, ec * N), lambda i, j: (0, prev(j))),
                  pl.BlockSpec((H, ec, tt), lambda i, j: (0, cur(j), i)),
                  pl.BlockSpec((H, ec, tt), lambda i, j: (0, cur(j), i)),
                  pl.BlockSpec((H, N, tt), lambda i, j: (0, 0, i)),
                  pl.BlockSpec((H, N, tt), lambda i, j: (0, 0, i)),
                  pl.BlockSpec((tt, D), lambda i, j: (i, 0))],
        out_specs=pl.BlockSpec((tt, D), lambda i, j: (i, 0)),
        out_shape=jax.ShapeDtypeStruct((T, D), F32),
        scratch_shapes=[pltpu.VMEM((D, tt), F32), pltpu.VMEM((2, ec * N, tt), F32),
                        pltpu.VMEM((2, ec * N, tt), BF16)],
        compiler_params=_cparams(("parallel", "arbitrary")),
        name="peer_main",
    )(h2t, u, vt, r1, f1, k2r, f2, x1)


def _final_kernel(x_ref, p_ref, wg_ref, wp_ref, fn_ref, o_ref):
    x = x_ref[...]
    gate = _sigmoid(_dot(x.astype(BF16), wg_ref[...]))
    x = x + gate * _dot(p_ref[...].astype(BF16), wp_ref[...])
    ms = jnp.mean(x * x, axis=-1, keepdims=True)
    o_ref[...] = x * lax.rsqrt(ms + EPS) * fn_ref[...]


def _final(x, p, wg, wp, fn, tm):
    T = x.shape[0]
    D = D_MODEL
    return pl.pallas_call(
        _final_kernel,
        grid=(T // tm,),
        in_specs=[pl.BlockSpec((tm, D), lambda i: (i, 0)),
                  pl.BlockSpec((tm, PLE_DIM), lambda i: (i, 0)),
                  pl.BlockSpec((D, D), lambda i: (0, 0)),
                  pl.BlockSpec((PLE_DIM, D), lambda i: (0, 0)),
                  pl.BlockSpec((1, D), lambda i: (0, 0))],
        out_specs=pl.BlockSpec((tm, D), lambda i: (i, 0)),
        out_shape=jax.ShapeDtypeStruct((T, D), F32),
        compiler_params=_cparams(("parallel",)),
        name="final",
    )(x, p, wg, wp, fn)


def _rot_cols(w):
    half = MLA_ROPE // 2
    return jnp.concatenate([-w[..., half:], w[..., :half]], axis=-1)


def _pack_w_in(w_in):
    pts, acc = [], 0
    for sz in (512, 512, 1024, 1024, 2 * GLA_RANK, MLA_Q_LORA, MLA_KV_LORA, MLA_ROPE):
        acc += sz
        pts.append(acc)
    q, k, v, r, a_lr, c_q, c_kv, k_r, gate = jnp.split(w_in, pts, axis=-1)
    z = lambda n: jnp.zeros((D_MODEL, n), w_in.dtype)
    packed = jnp.concatenate([gate, v, r, q, k, c_q, z(CQ_PAD - MLA_Q_LORA), c_kv, k_r, _rot_cols(k_r),
                              a_lr, z(128 - 2 * GLA_RANK)], axis=-1)
    assert packed.shape[-1] == P_COLS
    return packed.astype(BF16)


def _pack_w_uq(w_uq):
    w = w_uq.reshape(MLA_Q_LORA, MLA_HEADS, MLA_DQK)
    pe = w[..., MLA_NOPE:]
    w = jnp.concatenate([w, _rot_cols(pe)], axis=-1).reshape(MLA_Q_LORA, MLA_HEADS * 256)
    return jnp.concatenate([w, jnp.zeros((CQ_PAD - MLA_Q_LORA, MLA_HEADS * 256), w.dtype)], axis=0).astype(BF16)


def _tile(n, pref):
    return pref if n % pref == 0 else n


def kernel(x_prompt, x_sample, p_prompt, p_sample, norm1, w_in, gla_a_w_f, gla_a_b_f, gla_a_w_b, gla_a_b_b, gla_norm, gla_w_o, mla_q_norm, mla_w_uq, mla_kv_norm, mla_w_ukv, mla_w_o, w_out, norm2, peer_w_q, peer_k1, peer_k2, peer_u, peer_v, ple_proj, ple_gate, final_norm):
    B1, S, D = x_prompt.shape
    B2 = x_sample.shape[0]
    assert x_sample.shape[1] == S and norm1.shape[0] == 1, "one layer, equal sequence lengths"
    Bt = B1 + B2
    T = Bt * S
    x = jnp.concatenate([x_prompt, x_sample], axis=0).reshape(T, D)
    p = jnp.concatenate([p_prompt[0], p_sample[0]], axis=0).reshape(T, PLE_DIM)
    row = lambda v: v.reshape(1, -1).astype(F32)

    P = _in_proj(x, row(norm1[0]), _pack_w_in(w_in[0]), _tile(T, 512), 1024)

    o_f, o_b = _gla(P, gla_a_w_f[0], row(gla_a_b_f[0]), gla_a_w_b[0], row(gla_a_b_b[0]), Bt, S, _tile(S, 256))

    pos = jnp.arange(S, dtype=F32)
    inv = ROPE_THETA ** (-jnp.arange(MLA_ROPE // 2, dtype=F32) * 2.0 / MLA_ROPE)
    ang = pos[:, None] * inv[None, :]
    cos2 = jnp.concatenate([jnp.cos(ang), jnp.cos(ang)], axis=-1)
    sin2 = jnp.concatenate([jnp.sin(ang), jnp.sin(ang)], axis=-1)
    qn = jnp.concatenate([mla_q_norm[0], jnp.zeros((CQ_PAD - MLA_Q_LORA,), F32)]).reshape(1, CQ_PAD)
    Q, K, V = _mla_prep(P, cos2, sin2, qn, _pack_w_uq(mla_w_uq[0]), row(mla_kv_norm[0]),
                        mla_w_ukv[0].astype(BF16), Bt, S, _tile(S, 512))
    o_att = _attention(Q, K, V, _tile(S, 512), _tile(S, 1024)).reshape(T, MLA_HEADS * MLA_V)

    gn = jnp.tile(gla_norm[0], GLA_HEADS).reshape(1, D)
    x1, h2t, qp = _merge(x, o_f, o_b, P, o_att, gn, gla_w_o[0].astype(BF16), mla_w_o[0].astype(BF16),
                        w_out[0].astype(BF16), row(norm2[0]), peer_w_q[0].astype(BF16), _tile(T, 256))

    r1, f1, k2r, f2 = _peer_topk(qp, peer_k1[0].astype(BF16), peer_k2[0].astype(BF16), _tile(T, 256))
    x2 = _peer_main(h2t, peer_u[0].astype(BF16), peer_v[0].T.astype(BF16), r1, f1, k2r, f2, x1,
                    _tile(T, 512), 8)

    y = _final(x2, p, ple_gate[0].astype(BF16), ple_proj[0].astype(BF16), row(final_norm), _tile(T, 512))
    y = y.reshape(Bt, S, D)
    return (y[:B1], y[B1:])
```

```python
import functools

import jax
import jax.numpy as jnp
from jax import lax
from jax.experimental import pallas as pl
from jax.experimental.pallas import tpu as pltpu

F32 = jnp.float32
BF16 = jnp.bfloat16
EPS = 1e-6

D_MODEL = 1024
PLE_DIM = 256
GLA_HEADS, GLA_DK, GLA_DV, GLA_RANK, GLA_GATE_NORM, GLA_CHUNK = 4, 128, 256, 16, 16.0, 64
MLA_HEADS, MLA_Q_LORA, MLA_KV_LORA, MLA_NOPE, MLA_ROPE, MLA_V = 8, 384, 256, 128, 64, 128
MLA_DQK = MLA_NOPE + MLA_ROPE
MLA_QK_PAD = 256
ROPE_THETA = 10000.0
LOG2E = 1.4426950408889634
PEER_HEADS, PEER_NKEYS, PEER_HALF, PEER_TOPK = 8, 128, 128, 16
PEER_N = PEER_NKEYS * PEER_NKEYS

P_COLS = 6144
COL_GATE, COL_V, COL_R, COL_Q, COL_K, COL_CQ, COL_CKV, COL_KR, COL_ALR = (
    0, 2048, 3072, 4096, 4608, 5120, 5632, 5888, 6016)
CQ_PAD = 512

VMEM_LIMIT = 56 * 1024 * 1024


def _cparams(sem):
    return pltpu.CompilerParams(dimension_semantics=sem, vmem_limit_bytes=VMEM_LIMIT)


def _dot(a, b):
    return jnp.dot(a, b, preferred_element_type=F32)


def _dot_nt(a, b):
    return lax.dot_general(a, b, (((1,), (1,)), ((), ())), preferred_element_type=F32)


def _dot_tn(a, b):
    return lax.dot_general(a, b, (((0,), (0,)), ((), ())), preferred_element_type=F32)


def _sigmoid(x):
    return 1.0 / (1.0 + jnp.exp(-x))


def _in_proj_kernel(x_ref, g_ref, w_ref, o_ref, h_ref):
    @pl.when(pl.program_id(1) == 0)
    def _():
        x = x_ref[...]
        ms = jnp.mean(x * x, axis=-1, keepdims=True)
        h_ref[...] = (x * lax.rsqrt(ms + EPS) * g_ref[...]).astype(BF16)

    o_ref[...] = _dot(h_ref[...], w_ref[...]).astype(o_ref.dtype)


def _in_proj(x, g, w, tm, tn):
    T = x.shape[0]
    return pl.pallas_call(
        _in_proj_kernel,
        grid=(T // tm, P_COLS // tn),
        in_specs=[pl.BlockSpec((tm, D_MODEL), lambda i, j: (i, 0)),
                  pl.BlockSpec((1, D_MODEL), lambda i, j: (0, 0)),
                  pl.BlockSpec((D_MODEL, tn), lambda i, j: (0, j))],
        out_specs=pl.BlockSpec((tm, tn), lambda i, j: (i, j)),
        out_shape=jax.ShapeDtypeStruct((T, P_COLS), BF16),
        scratch_shapes=[pltpu.VMEM((tm, D_MODEL), BF16)],
        compiler_params=_cparams(("parallel", "arbitrary")),
        name="in_proj",
    )(x, g, w)


def _gla_direction(q_ref, k_ref, v_ref, a_ref, aw_ref, ab_ref, st_ref, o_ref, a_off, backward, R):
    C = GLA_CHUNK
    nc = R // C
    row = lax.broadcasted_iota(jnp.int32, (R, R), 0)
    col = lax.broadcasted_iota(jnp.int32, (R, R), 1)
    same = (row // C) == (col // C)
    cum = same & ((col >= row) if backward else (col <= row))
    att_mask = same & ((col > row) if backward else (col <= row))
    tri = jnp.where(cum, 1.0, 0.0).astype(BF16)
    blk = jnp.where(same, 1.0, 0.0).astype(BF16)
    a = a_ref[:, a_off:a_off + GLA_RANK].astype(BF16)
    order = range(nc - 1, -1, -1) if backward else range(nc)
    for h in range(GLA_HEADS):
        ks = slice(h * GLA_DK, (h + 1) * GLA_DK)
        vs = slice(h * GLA_DV, (h + 1) * GLA_DV)
        z = _dot(a, aw_ref[:, ks].astype(BF16)) + ab_ref[:, ks]
        la = (jnp.minimum(z, 0.0) - jnp.log(1.0 + jnp.exp(-jnp.abs(z)))) / GLA_GATE_NORM
        la_hi = la.astype(BF16)
        la_lo = (la - la_hi.astype(F32)).astype(BF16)
        b = _dot(tri, la_hi) + _dot(tri, la_lo)
        tot = _dot(blk, la_hi) + _dot(blk, la_lo)
        qh = q_ref[:, ks].astype(F32) * (GLA_DK ** -0.5)
        kh = k_ref[:, ks].astype(F32)
        vh = v_ref[:, vs].astype(BF16)
        q_e = (qh * jnp.exp(b)).astype(BF16)
        k_e = (kh * jnp.exp(-b)).astype(BF16)
        k_s = (kh * jnp.exp(tot - b)).astype(BF16)
        dec = jnp.exp(tot)
        att = jnp.where(att_mask, _dot_nt(q_e, k_e), 0.0).astype(BF16)
        o_intra = _dot(att, vh)
        for c in order:
            rs = slice(c * C, (c + 1) * C)
            st = st_ref[h]
            o_ref[rs, vs] = o_intra[rs] + _dot_nt(q_e[rs], st.astype(BF16))
            st_ref[h] = dec[c * C:c * C + 1, :] * st + _dot_tn(vh[rs], k_s[rs])


def _gla_kernel(qf_ref, kf_ref, vf_ref, af_ref, qb_ref, kb_ref, vb_ref, ab_ref,
                awf_ref, abf_ref, awb_ref, abb_ref, of_ref, ob_ref, stf_ref, stb_ref, *, R):
    @pl.when(pl.program_id(1) == 0)
    def _():
        stf_ref[...] = jnp.zeros_like(stf_ref)
        stb_ref[...] = jnp.zeros_like(stb_ref)

    _gla_direction(qf_ref, kf_ref, vf_ref, af_ref, awf_ref, abf_ref, stf_ref, of_ref, 0, False, R)
    _gla_direction(qb_ref, kb_ref, vb_ref, ab_ref, awb_ref, abb_ref, stb_ref, ob_ref, GLA_RANK, True, R)


def _gla(P, awf, abf, awb, abb, Bt, S, R):
    T = Bt * S
    NB = S // R
    HK, HV = GLA_HEADS * GLA_DK, GLA_HEADS * GLA_DV
    fwd = lambda b, n: b * NB + n
    bwd = lambda b, n: b * NB + (NB - 1 - n)

    def specs(rowf):
        return [pl.BlockSpec((R, HK), lambda b, n: (rowf(b, n), COL_Q // HK)),
                pl.BlockSpec((R, HK), lambda b, n: (rowf(b, n), COL_K // HK)),
                pl.BlockSpec((R, HV), lambda b, n: (rowf(b, n), COL_V // HV)),
                pl.BlockSpec((R, 128), lambda b, n: (rowf(b, n), COL_ALR // 128))]

    wspec = lambda shape: pl.BlockSpec(shape, lambda b, n: (0, 0))
    return pl.pallas_call(
        functools.partial(_gla_kernel, R=R),
        grid=(Bt, NB),
        in_specs=specs(fwd) + specs(bwd) + [wspec((GLA_RANK, HK)), wspec((1, HK)),
                                            wspec((GLA_RANK, HK)), wspec((1, HK))],
        out_specs=[pl.BlockSpec((R, HV), lambda b, n: (fwd(b, n), 0)),
                   pl.BlockSpec((R, HV), lambda b, n: (bwd(b, n), 0))],
        out_shape=[jax.ShapeDtypeStruct((T, HV), F32), jax.ShapeDtypeStruct((T, HV), F32)],
        scratch_shapes=[pltpu.VMEM((GLA_HEADS, GLA_DV, GLA_DK), F32),
                        pltpu.VMEM((GLA_HEADS, GLA_DV, GLA_DK), F32)],
        compiler_params=_cparams(("parallel", "arbitrary")),
        name="gla",
    )(P, P, P, P, P, P, P, P, awf, abf, awb, abb)


def _mla_prep_kernel(cq_ref, ckv_ref, kr_ref, cos_ref, sin_ref, qn_ref, wuq_ref, kvn_ref, wukv_ref,
                     q_out, k_out, v_out):
    cq = cq_ref[...].astype(F32)
    ms = jnp.sum(cq * cq, axis=-1, keepdims=True) * (1.0 / MLA_Q_LORA)
    hq = (cq * lax.rsqrt(ms + EPS) * qn_ref[...]).astype(BF16)
    qq = _dot(hq, wuq_ref[...])
    ckv = ckv_ref[...].astype(F32)
    ms = jnp.mean(ckv * ckv, axis=-1, keepdims=True)
    hkv = (ckv * lax.rsqrt(ms + EPS) * kvn_ref[...]).astype(BF16)
    kv = _dot(hkv, wukv_ref[...])
    cos = cos_ref[...]
    sin = sin_ref[...]
    kr = kr_ref[...].astype(F32)
    k_pe = (kr[:, :MLA_ROPE] * cos + kr[:, MLA_ROPE:] * sin).astype(BF16)
    scale = MLA_DQK ** -0.5 * LOG2E
    zpad = jnp.zeros((cq.shape[0], MLA_QK_PAD - MLA_DQK), BF16)
    for h in range(MLA_HEADS):
        o = h * 256
        q_pe = qq[:, o + 128:o + 192] * cos + qq[:, o + 192:o + 256] * sin
        q_out[0, h, :, 0:MLA_NOPE] = (qq[:, o:o + MLA_NOPE] * scale).astype(BF16)
        q_out[0, h, :, MLA_NOPE:MLA_DQK] = (q_pe * scale).astype(BF16)
        q_out[0, h, :, MLA_DQK:MLA_QK_PAD] = zpad
        k_out[0, h, :, 0:MLA_NOPE] = kv[:, o:o + MLA_NOPE].astype(BF16)
        k_out[0, h, :, MLA_NOPE:MLA_DQK] = k_pe
        k_out[0, h, :, MLA_DQK:MLA_QK_PAD] = zpad
        v_out[0, h] = kv[:, o + MLA_NOPE:o + 256].astype(BF16)


def _mla_prep(P, cos2, sin2, qn, wuq, kvn, wukv, Bt, S, ts):
    NS = S // ts
    row = lambda b, i: b * NS + i
    wspec = lambda shape: pl.BlockSpec(shape, lambda b, i: (0, 0))
    H = MLA_HEADS
    return pl.pallas_call(
        _mla_prep_kernel,
        grid=(Bt, NS),
        in_specs=[pl.BlockSpec((ts, CQ_PAD), lambda b, i: (row(b, i), COL_CQ // CQ_PAD)),
                  pl.BlockSpec((ts, MLA_KV_LORA), lambda b, i: (row(b, i), COL_CKV // MLA_KV_LORA)),
                  pl.BlockSpec((ts, 128), lambda b, i: (row(b, i), COL_KR // 128)),
                  pl.BlockSpec((ts, MLA_ROPE), lambda b, i: (i, 0)),
                  pl.BlockSpec((ts, MLA_ROPE), lambda b, i: (i, 0)),
                  wspec((1, CQ_PAD)), wspec((CQ_PAD, H * 256)),
                  wspec((1, MLA_KV_LORA)), wspec((MLA_KV_LORA, H * 256))],
        out_specs=[pl.BlockSpec((1, H, ts, MLA_QK_PAD), lambda b, i: (b, 0, i, 0)),
                   pl.BlockSpec((1, H, ts, MLA_QK_PAD), lambda b, i: (b, 0, i, 0)),
                   pl.BlockSpec((1, H, ts, MLA_V), lambda b, i: (b, 0, i, 0))],
        out_shape=[jax.ShapeDtypeStruct((Bt, H, S, MLA_QK_PAD), BF16),
                   jax.ShapeDtypeStruct((Bt, H, S, MLA_QK_PAD), BF16),
                   jax.ShapeDtypeStruct((Bt, H, S, MLA_V), BF16)],
        compiler_params=_cparams(("parallel", "parallel")),
        name="mla_prep",
    )(P, P, P, cos2, sin2, qn, wuq, kvn, wukv)


ATTN_HEADS_PER_STEP = 4


def _attn_kernel(q_ref, k_ref, v_ref, o_ref, m_ref, l_ref, acc_ref):
    kv = pl.program_id(3)
    nl = k_ref.shape[2] // 128

    @pl.when(kv == 0)
    def _():
        m_ref[...] = jnp.full_like(m_ref, -jnp.inf)
        l_ref[...] = jnp.zeros_like(l_ref)
        acc_ref[...] = jnp.zeros_like(acc_ref)

    for h in range(ATTN_HEADS_PER_STEP):
        s = _dot_nt(q_ref[0, h], k_ref[0, h])
        m_prev = m_ref[h]
        m_new = jnp.maximum(m_prev, jnp.max(s, axis=-1, keepdims=True))
        alpha = jnp.exp2(m_prev - m_new)
        ps = [jnp.exp2(s[:, c * 128:(c + 1) * 128] - m_new) for c in range(nl)]
        psum = ps[0]
        for c in range(1, nl):
            psum = psum + ps[c]
        p = jnp.concatenate([x.astype(BF16) for x in ps], axis=-1)
        l_ref[h] = alpha * l_ref[h] + psum
        acc_ref[h] = alpha * acc_ref[h] + _dot(p, v_ref[0, h])
        m_ref[h] = m_new

    @pl.when(kv == pl.num_programs(3) - 1)
    def _():
        for h in range(ATTN_HEADS_PER_STEP):
            l = jnp.sum(l_ref[h], axis=-1, keepdims=True)
            o_ref[0, :, h * MLA_V:(h + 1) * MLA_V] = (acc_ref[h] / l).astype(o_ref.dtype)


def _attention(Q, K, V, tq, tk):
    Bt, H, S, _ = Q.shape
    G = ATTN_HEADS_PER_STEP
    return pl.pallas_call(
        _attn_kernel,
        grid=(Bt, H // G, S // tq, S // tk),
        in_specs=[pl.BlockSpec((1, G, tq, MLA_QK_PAD), lambda b, h, i, j: (b, h, i, 0)),
                  pl.BlockSpec((1, G, tk, MLA_QK_PAD), lambda b, h, i, j: (b, h, j, 0)),
                  pl.BlockSpec((1, G, tk, MLA_V), lambda b, h, i, j: (b, h, j, 0))],
        out_specs=pl.BlockSpec((1, tq, G * MLA_V), lambda b, h, i, j: (b, i, h)),
        out_shape=jax.ShapeDtypeStruct((Bt, S, H * MLA_V), BF16),
        scratch_shapes=[pltpu.VMEM((G, tq, 128), F32), pltpu.VMEM((G, tq, 128), F32),
                        pltpu.VMEM((G, tq, MLA_V), F32)],
        compiler_params=_cparams(("parallel", "parallel", "parallel", "arbitrary")),
        name="attn",
    )(Q, K, V)


def _merge_kernel(x_ref, of_ref, ob_ref, r_ref, gate_ref, oat_ref, gn_ref, wgo_ref, wmo_ref, wout_ref,
                  n2_ref, wq_ref, x1_ref, h2t_ref, qp_ref):
    o = of_ref[...] + ob_ref[...]
    gn = gn_ref[...]
    parts = []
    for h in range(GLA_HEADS):
        vs = slice(h * GLA_DV, (h + 1) * GLA_DV)
        oh = o[:, vs]
        ms = jnp.mean(oh * oh, axis=-1, keepdims=True)
        parts.append(oh * lax.rsqrt(ms + EPS) * gn[:, vs])
    on = jnp.concatenate(parts, axis=-1)
    r = r_ref[...].astype(F32)
    ya = _dot((on * (r * _sigmoid(r))).astype(BF16), wgo_ref[...])
    yb = _dot(oat_ref[...], wmo_ref[...])
    g = _sigmoid(gate_ref[...].astype(F32))
    mix = g[:, :D_MODEL] * ya + g[:, D_MODEL:] * yb
    x1 = x_ref[...] + _dot(mix.astype(BF16), wout_ref[...])
    x1_ref[...] = x1
    ms = jnp.mean(x1 * x1, axis=-1, keepdims=True)
    h2 = x1 * lax.rsqrt(ms + EPS) * n2_ref[...]
    h2t_ref[...] = h2.T.astype(BF16)
    qp_ref[...] = _dot(h2.astype(BF16), wq_ref[...]).astype(BF16)


def _merge(x, o_f, o_b, P, o_att, gn, wgo, wmo, wout, n2, wq, tm):
    T = x.shape[0]
    D = D_MODEL
    QW = PEER_HEADS * 2 * PEER_HALF
    tile = lambda w, c=0: pl.BlockSpec((tm, w), lambda i: (i, c))
    wspec = lambda shape: pl.BlockSpec(shape, lambda i: (0, 0))
    return pl.pallas_call(
        _merge_kernel,
        grid=(T // tm,),
        in_specs=[tile(D), tile(D), tile(D), tile(D, COL_R // D), tile(2 * D, COL_GATE // (2 * D)), tile(D),
                  wspec((1, D)), wspec((D, D)), wspec((D, D)), wspec((D, D)), wspec((1, D)), wspec((D, QW))],
        out_specs=[tile(D), pl.BlockSpec((D, tm), lambda i: (0, i)), tile(QW)],
        out_shape=[jax.ShapeDtypeStruct((T, D), F32), jax.ShapeDtypeStruct((D, T), BF16),
                   jax.ShapeDtypeStruct((T, QW), BF16)],
        compiler_params=_cparams(("parallel",)),
        name="merge",
    )(x, o_f, o_b, P, P, o_att, gn, wgo, wmo, wout, n2, wq)


def _extract16(cur, tie_index):
    n, t = cur.shape
    slot = lax.broadcasted_iota(jnp.int32, (PEER_TOPK, t), 0)
    vals = jnp.zeros((PEER_TOPK, t), F32)
    rank = jnp.full((n, t), float(PEER_TOPK), F32)
    for a in range(PEER_TOPK):
        m = jnp.max(cur, axis=0, keepdims=True)
        sel = cur == m
        if tie_index is not None:
            first = jnp.min(jnp.where(sel, tie_index, float(n * n)), axis=0, keepdims=True)
            sel = tie_index == first
        vals = jnp.where(slot == a, m, vals)
        rank = jnp.where(sel, float(a), rank)
        cur = jnp.where(sel, -jnp.inf, cur)
    return vals, rank


def _taken_error(rank):
    taken = jnp.sum(jnp.where(rank < float(PEER_TOPK), 1.0, 0.0), axis=0, keepdims=True)
    return jnp.abs(taken - float(PEER_TOPK))


_CAND_ROWS = [16] + [8] * (PEER_TOPK - 1)


def _peer_topk_kernel(qp_ref, k1_ref, k2_ref, r1_ref, f1_ref, k2r_ref, f2_ref, s_sc, t_sc, rk_sc, ch_sc):
    K, H = PEER_TOPK, PEER_HEADS
    tt = qp_ref.shape[0]
    k1 = k1_ref[...]
    k2 = k2_ref[...]
    for h in range(H):
        o = h * 2 * PEER_HALF
        s_sc[2 * h] = _dot_nt(k1, qp_ref[:, o:o + PEER_HALF])
        s_sc[2 * h + 1] = _dot_nt(k2, qp_ref[:, o + PEER_HALF:o + 2 * PEER_HALF])

    def stage1(exact):
        err = jnp.zeros((1, tt), F32)
        idx = lax.broadcasted_iota(jnp.int32, (PEER_NKEYS, tt), 0).astype(F32) if exact else None
        for i in range(2 * H):
            vals, rank = _extract16(s_sc[i], idx)
            t_sc[i] = vals
            rk_sc[i] = rank
            if not exact:
                err = jnp.maximum(err, _taken_error(rank))
        return err

    err1 = stage1(False)

    @pl.when(jnp.max(err1) > 0.0)
    def _():
        stage1(True)

    def stage2(exact):
        err = jnp.zeros((1, tt), F32)
        for h in range(H):
            t1 = t_sc[2 * h]
            t2 = t_sc[2 * h + 1]
            slabs, flats = [], []
            for a in range(K):
                nr = _CAND_ROWS[a]
                brow = lax.broadcasted_iota(jnp.int32, (nr, tt), 0)
                slabs.append(jnp.where(brow < K // (a + 1), t1[a:a + 1, :] + t2[0:nr, :], -jnp.inf))
                flats.append((brow + a * K).astype(F32))
            cand = jnp.concatenate(slabs, axis=0)
            _, rank = _extract16(cand, jnp.concatenate(flats, axis=0) if exact else None)
            ch_sc[h] = jnp.where(rank < float(K), 1.0, 0.0)
            if not exact:
                err = jnp.maximum(err, _taken_error(rank))
        return err

    err2 = stage2(False)

    @pl.when(jnp.max(err2) > 0.0)
    def _():
        stage2(True)

    for h in range(H):
        s1, s2 = s_sc[2 * h], s_sc[2 * h + 1]
        t1, t2 = t_sc[2 * h], t_sc[2 * h + 1]
        rank1 = rk_sc[2 * h]
        e1s = jnp.exp(t1 - t1[0:1, :])
        e2s = jnp.exp(t2 - t2[0:1, :])
        z = jnp.zeros((1, tt), F32)
        r1 = jnp.zeros(rank1.shape, F32)
        off = 0
        for a in range(K):
            nr = _CAND_ROWS[a]
            ch = ch_sc[h, off:off + nr, :]
            off += nr
            cnt = jnp.sum(ch, axis=0, keepdims=True)
            z = z + e1s[a:a + 1, :] * jnp.sum(ch * e2s[0:nr, :], axis=0, keepdims=True)
            r1 = jnp.where(rank1 == float(a), cnt, r1)
        r1_ref[h] = r1
        f1_ref[h] = jnp.exp(s1 - t1[0:1, :]) * (0.5 / z)
        k2r_ref[h] = rk_sc[2 * h + 1].astype(BF16)
        f2_ref[h] = jnp.exp(s2 - t2[0:1, :]).astype(BF16)


def _peer_topk(qp, k1, k2, tt):
    T = qp.shape[0]
    H, N = PEER_HEADS, PEER_NKEYS
    out = jax.ShapeDtypeStruct((H, N, T), F32)
    outb = jax.ShapeDtypeStruct((H, N, T), BF16)
    ospec = pl.BlockSpec((H, N, tt), lambda i: (0, 0, i))
    return pl.pallas_call(
        _peer_topk_kernel,
        grid=(T // tt,),
        in_specs=[pl.BlockSpec((tt, H * 2 * PEER_HALF), lambda i: (i, 0)),
                  pl.BlockSpec((N, PEER_HALF), lambda i: (0, 0)),
                  pl.BlockSpec((N, PEER_HALF), lambda i: (0, 0))],
        out_specs=[ospec, ospec, ospec, ospec],
        out_shape=[out, out, outb, outb],
        scratch_shapes=[pltpu.VMEM((2 * H, N, tt), F32), pltpu.VMEM((2 * H, PEER_TOPK, tt), F32),
                        pltpu.VMEM((2 * H, N, tt), F32), pltpu.VMEM((H, sum(_CAND_ROWS), tt), F32)],
        compiler_params=_cparams(("parallel",)),
        name="peer_topk",
    )(qp, k1, k2)


def _peer_main_kernel(h2t_ref, u_ref, vt_ref, r1_ref, f1_ref, k2r_ref, f2_ref, x1_ref, o_ref,
                      acc_ref, s_ref, w_ref, *, ec):
    j = pl.program_id(1)
    N = PEER_NKEYS

    @pl.when(j == 0)
    def _():
        acc_ref[...] = jnp.zeros_like(acc_ref)
        s_ref[1] = jnp.zeros(s_ref.shape[1:], s_ref.dtype)
        w_ref[1] = jnp.zeros(w_ref.shape[1:], w_ref.dtype)

    def step(rd, wr):
        s_ref[wr] = _dot(u_ref[...], h2t_ref[...])
        s = s_ref[rd]
        act = s * (1.0 + lax.erf(s * (2.0 ** -0.5)))
        acc_ref[...] += _dot(vt_ref[...], act.astype(BF16) * w_ref[rd])
        for e in range(ec):
            w = jnp.zeros((N, s.shape[1]), BF16)
            for h in range(PEER_HEADS):
                r1 = r1_ref[h, e:e + 1, :].astype(BF16)
                f1 = f1_ref[h, e:e + 1, :].astype(BF16)
                w = w + jnp.where(k2r_ref[h] < r1, f2_ref[h], jnp.zeros((), BF16)) * f1
            w_ref[wr, e * N:(e + 1) * N, :] = w

    @pl.when(j % 2 == 0)
    def _():
        step(1, 0)

    @pl.when(j % 2 == 1)
    def _():
        step(0, 1)

    @pl.when(j == pl.num_programs(1) - 1)
    def _():
        o_ref[...] = x1_ref[...] + acc_ref[...].T


def _peer_main(h2t, u, vt, r1, f1, k2r, f2, x1, tt, ec):
    T = h2t.shape[1]
    D, H, N = D_MODEL, PEER_HEADS, PEER_NKEYS
    nc = N // ec
    cur = lambda j: jnp.minimum(j, nc - 1)
    prev = lambda j: jnp.maximum(j - 1, 0)
    return pl.pallas_call(
        functools.partial(_peer_main_kernel, ec=ec),
        grid=(T // tt, nc + 1),
        in_specs=[pl.BlockSpec((D, tt), lambda i, j: (0, i)),
                  pl.BlockSpec((ec * N, D), lambda i, j: (cur(j), 0)),
                  pl.BlockSpec((D, ec * N), lambda i, j: (0, prev(j))),
                  pl.BlockSpec((H, ec, tt), lambda i, j: (0, cur(j), i)),
                  pl.BlockSpec((H, ec, tt), lambda i, j: (0, cur(j), i)),
                  pl.BlockSpec((H, N, tt), lambda i, j: (0, 0, i)),
                  pl.BlockSpec((H, N, tt), lambda i, j: (0, 0, i)),
                  pl.BlockSpec((tt, D), lambda i, j: (i, 0))],
        out_specs=pl.BlockSpec((tt, D), lambda i, j: (i, 0)),
        out_shape=jax.ShapeDtypeStruct((T, D), F32),
        scratch_shapes=[pltpu.VMEM((D, tt), F32), pltpu.VMEM((2, ec * N, tt), F32),
                        pltpu.VMEM((2, ec * N, tt), BF16)],
        compiler_params=_cparams(("parallel", "arbitrary")),
        name="peer_main",
    )(h2t, u, vt, r1, f1, k2r, f2, x1)


def _final_kernel(x_ref, p_ref, wg_ref, wp_ref, fn_ref, o_ref):
    x = x_ref[...]
    gate = _sigmoid(_dot(x.astype(BF16), wg_ref[...]))
    x = x + gate * _dot(p_ref[...].astype(BF16), wp_ref[...])
    ms = jnp.mean(x * x, axis=-1, keepdims=True)
    o_ref[...] = x * lax.rsqrt(ms + EPS) * fn_ref[...]


def _final(x, p, wg, wp, fn, tm):
    T = x.shape[0]
    D = D_MODEL
    return pl.pallas_call(
        _final_kernel,
        grid=(T // tm,),
        in_specs=[pl.BlockSpec((tm, D), lambda i: (i, 0)),
                  pl.BlockSpec((tm, PLE_DIM), lambda i: (i, 0)),
                  pl.BlockSpec((D, D), lambda i: (0, 0)),
                  pl.BlockSpec((PLE_DIM, D), lambda i: (0, 0)),
                  pl.BlockSpec((1, D), lambda i: (0, 0))],
        out_specs=pl.BlockSpec((tm, D), lambda i: (i, 0)),
        out_shape=jax.ShapeDtypeStruct((T, D), F32),
        compiler_params=_cparams(("parallel",)),
        name="final",
    )(x, p, wg, wp, fn)


def _rot_cols(w):
    half = MLA_ROPE // 2
    return jnp.concatenate([-w[..., half:], w[..., :half]], axis=-1)


def _pack_w_in(w_in):
    pts, acc = [], 0
    for sz in (512, 512, 1024, 1024, 2 * GLA_RANK, MLA_Q_LORA, MLA_KV_LORA, MLA_ROPE):
        acc += sz
        pts.append(acc)
    q, k, v, r, a_lr, c_q, c_kv, k_r, gate = jnp.split(w_in, pts, axis=-1)
    z = lambda n: jnp.zeros((D_MODEL, n), w_in.dtype)
    packed = jnp.concatenate([gate, v, r, q, k, c_q, z(CQ_PAD - MLA_Q_LORA), c_kv, k_r, _rot_cols(k_r),
                              a_lr, z(128 - 2 * GLA_RANK)], axis=-1)
    assert packed.shape[-1] == P_COLS
    return packed.astype(BF16)


def _pack_w_uq(w_uq):
    w = w_uq.reshape(MLA_Q_LORA, MLA_HEADS, MLA_DQK)
    pe = w[..., MLA_NOPE:]
    w = jnp.concatenate([w, _rot_cols(pe)], axis=-1).reshape(MLA_Q_LORA, MLA_HEADS * 256)
    return jnp.concatenate([w, jnp.zeros((CQ_PAD - MLA_Q_LORA, MLA_HEADS * 256), w.dtype)], axis=0).astype(BF16)


def _tile(n, pref):
    return pref if n % pref == 0 else n


def kernel(x_prompt, x_sample, p_prompt, p_sample, norm1, w_in, gla_a_w_f, gla_a_b_f, gla_a_w_b, gla_a_b_b, gla_norm, gla_w_o, mla_q_norm, mla_w_uq, mla_kv_norm, mla_w_ukv, mla_w_o, w_out, norm2, peer_w_q, peer_k1, peer_k2, peer_u, peer_v, ple_proj, ple_gate, final_norm):
    B1, S, D = x_prompt.shape
    B2 = x_sample.shape[0]
    assert x_sample.shape[1] == S and norm1.shape[0] == 1, "one layer, equal sequence lengths"
    Bt = B1 + B2
    T = Bt * S
    x = jnp.concatenate([x_prompt, x_sample], axis=0).reshape(T, D)
    p = jnp.concatenate([p_prompt[0], p_sample[0]], axis=0).reshape(T, PLE_DIM)
    row = lambda v: v.reshape(1, -1).astype(F32)

    P = _in_proj(x, row(norm1[0]), _pack_w_in(w_in[0]), _tile(T, 1024), 1024)

    o_f, o_b = _gla(P, gla_a_w_f[0], row(gla_a_b_f[0]), gla_a_w_b[0], row(gla_a_b_b[0]), Bt, S, _tile(S, 256))

    pos = jnp.arange(S, dtype=F32)
    inv = ROPE_THETA ** (-jnp.arange(MLA_ROPE // 2, dtype=F32) * 2.0 / MLA_ROPE)
    ang = pos[:, None] * inv[None, :]
    cos2 = jnp.concatenate([jnp.cos(ang), jnp.cos(ang)], axis=-1)
    sin2 = jnp.concatenate([jnp.sin(ang), jnp.sin(ang)], axis=-1)
    qn = jnp.concatenate([mla_q_norm[0], jnp.zeros((CQ_PAD - MLA_Q_LORA,), F32)]).reshape(1, CQ_PAD)
    Q, K, V = _mla_prep(P, cos2, sin2, qn, _pack_w_uq(mla_w_uq[0]), row(mla_kv_norm[0]),
                        mla_w_ukv[0].astype(BF16), Bt, S, _tile(S, 512))
    o_att = _attention(Q, K, V, _tile(S, 512), _tile(S, 1024)).reshape(T, MLA_HEADS * MLA_V)

    gn = jnp.tile(gla_norm[0], GLA_HEADS).reshape(1, D)
    x1, h2t, qp = _merge(x, o_f, o_b, P, o_att, gn, gla_w_o[0].astype(BF16), mla_w_o[0].astype(BF16),
                        w_out[0].astype(BF16), row(norm2[0]), peer_w_q[0].astype(BF16), _tile(T, 256))

    r1, f1, k2r, f2 = _peer_topk(qp, peer_k1[0].astype(BF16), peer_k2[0].astype(BF16), _tile(T, 256))
    x2 = _peer_main(h2t, peer_u[0].astype(BF16), peer_v[0].T.astype(BF16), r1, f1, k2r, f2, x1,
                    _tile(T, 512), 8)

    y = _final(x2, p, ple_gate[0].astype(BF16), ple_proj[0].astype(BF16), row(final_norm), _tile(T, 512))
    y = y.reshape(Bt, S, D)
    return (y[:B1], y[B1:])
```

```python
import functools

import jax
import jax.numpy as jnp
from jax import lax
from jax.experimental import pallas as pl
from jax.experimental.pallas import tpu as pltpu

F32 = jnp.float32
BF16 = jnp.bfloat16
EPS = 1e-6

D_MODEL = 1024
PLE_DIM = 256
GLA_HEADS, GLA_DK, GLA_DV, GLA_RANK, GLA_GATE_NORM, GLA_CHUNK = 4, 128, 256, 16, 16.0, 64
MLA_HEADS, MLA_Q_LORA, MLA_KV_LORA, MLA_NOPE, MLA_ROPE, MLA_V = 8, 384, 256, 128, 64, 128
MLA_DQK = MLA_NOPE + MLA_ROPE
MLA_QK_PAD = 256
ROPE_THETA = 10000.0
LOG2E = 1.4426950408889634
PEER_HEADS, PEER_NKEYS, PEER_HALF, PEER_TOPK = 8, 128, 128, 16
PEER_N = PEER_NKEYS * PEER_NKEYS

P_COLS = 6144
COL_GATE, COL_V, COL_R, COL_Q, COL_K, COL_CQ, COL_CKV, COL_KR, COL_ALR = (
    0, 2048, 3072, 4096, 4608, 5120, 5632, 5888, 6016)
CQ_PAD = 512

VMEM_LIMIT = 56 * 1024 * 1024


def _cparams(sem):
    return pltpu.CompilerParams(dimension_semantics=sem, vmem_limit_bytes=VMEM_LIMIT)


def _dot(a, b):
    return jnp.dot(a, b, preferred_element_type=F32)


def _dot_nt(a, b):
    return lax.dot_general(a, b, (((1,), (1,)), ((), ())), preferred_element_type=F32)


def _dot_tn(a, b):
    return lax.dot_general(a, b, (((0,), (0,)), ((), ())), preferred_element_type=F32)


def _sigmoid(x):
    return 1.0 / (1.0 + jnp.exp(-x))


def _in_proj_kernel(x_ref, g_ref, w_ref, o_ref, h_ref):
    @pl.when(pl.program_id(1) == 0)
    def _():
        x = x_ref[...]
        ms = jnp.mean(x * x, axis=-1, keepdims=True)
        h_ref[...] = (x * lax.rsqrt(ms + EPS) * g_ref[...]).astype(BF16)

    o_ref[...] = _dot(h_ref[...], w_ref[...]).astype(o_ref.dtype)


def _in_proj(x, g, w, tm, tn):
    T = x.shape[0]
    return pl.pallas_call(
        _in_proj_kernel,
        grid=(T // tm, P_COLS // tn),
        in_specs=[pl.BlockSpec((tm, D_MODEL), lambda i, j: (i, 0)),
                  pl.BlockSpec((1, D_MODEL), lambda i, j: (0, 0)),
                  pl.BlockSpec((D_MODEL, tn), lambda i, j: (0, j))],
        out_specs=pl.BlockSpec((tm, tn), lambda i, j: (i, j)),
        out_shape=jax.ShapeDtypeStruct((T, P_COLS), BF16),
        scratch_shapes=[pltpu.VMEM((tm, D_MODEL), BF16)],
        compiler_params=_cparams(("parallel", "arbitrary")),
        name="in_proj",
    )(x, g, w)


def _gla_direction(q_ref, k_ref, v_ref, a_ref, aw_ref, ab_ref, st_ref, o_ref, a_off, backward, R):
    C = GLA_CHUNK
    nc = R // C
    row = lax.broadcasted_iota(jnp.int32, (R, R), 0)
    col = lax.broadcasted_iota(jnp.int32, (R, R), 1)
    same = (row // C) == (col // C)
    cum = same & ((col >= row) if backward else (col <= row))
    att_mask = same & ((col > row) if backward else (col <= row))
    tri = jnp.where(cum, 1.0, 0.0).astype(BF16)
    blk = jnp.where(same, 1.0, 0.0).astype(BF16)
    a = a_ref[:, a_off:a_off + GLA_RANK].astype(BF16)
    order = range(nc - 1, -1, -1) if backward else range(nc)
    for h in range(GLA_HEADS):
        ks = slice(h * GLA_DK, (h + 1) * GLA_DK)
        vs = slice(h * GLA_DV, (h + 1) * GLA_DV)
        z = _dot(a, aw_ref[:, ks].astype(BF16)) + ab_ref[:, ks]
        la = (jnp.minimum(z, 0.0) - jnp.log(1.0 + jnp.exp(-jnp.abs(z)))) / GLA_GATE_NORM
        la_hi = la.astype(BF16)
        la_lo = (la - la_hi.astype(F32)).astype(BF16)
        b = _dot(tri, la_hi) + _dot(tri, la_lo)
        tot = _dot(blk, la_hi) + _dot(blk, la_lo)
        qh = q_ref[:, ks].astype(F32) * (GLA_DK ** -0.5)
        kh = k_ref[:, ks].astype(F32)
        vh = v_ref[:, vs].astype(BF16)
        q_e = (qh * jnp.exp(b)).astype(BF16)
        k_e = (kh * jnp.exp(-b)).astype(BF16)
        k_s = (kh * jnp.exp(tot - b)).astype(BF16)
        dec = jnp.exp(tot)
        att = jnp.where(att_mask, _dot_nt(q_e, k_e), 0.0).astype(BF16)
        o_intra = _dot(att, vh)
        for c in order:
            rs = slice(c * C, (c + 1) * C)
            st = st_ref[h]
            o_ref[rs, vs] = o_intra[rs] + _dot_nt(q_e[rs], st.astype(BF16))
            st_ref[h] = dec[c * C:c * C + 1, :] * st + _dot_tn(vh[rs], k_s[rs])


def _gla_kernel(qf_ref, kf_ref, vf_ref, af_ref, qb_ref, kb_ref, vb_ref, ab_ref,
                awf_ref, abf_ref, awb_ref, abb_ref, of_ref, ob_ref, stf_ref, stb_ref, *, R):
    @pl.when(pl.program_id(1) == 0)
    def _():
        stf_ref[...] = jnp.zeros_like(stf_ref)
        stb_ref[...] = jnp.zeros_like(stb_ref)

    _gla_direction(qf_ref, kf_ref, vf_ref, af_ref, awf_ref, abf_ref, stf_ref, of_ref, 0, False, R)
    _gla_direction(qb_ref, kb_ref, vb_ref, ab_ref, awb_ref, abb_ref, stb_ref, ob_ref, GLA_RANK, True, R)


def _gla(P, awf, abf, awb, abb, Bt, S, R):
    T = Bt * S
    NB = S // R
    HK, HV = GLA_HEADS * GLA_DK, GLA_HEADS * GLA_DV
    fwd = lambda b, n: b * NB + n
    bwd = lambda b, n: b * NB + (NB - 1 - n)

    def specs(rowf):
        return [pl.BlockSpec((R, HK), lambda b, n: (rowf(b, n), COL_Q // HK)),
                pl.BlockSpec((R, HK), lambda b, n: (rowf(b, n), COL_K // HK)),
                pl.BlockSpec((R, HV), lambda b, n: (rowf(b, n), COL_V // HV)),
                pl.BlockSpec((R, 128), lambda b, n: (rowf(b, n), COL_ALR // 128))]

    wspec = lambda shape: pl.BlockSpec(shape, lambda b, n: (0, 0))
    return pl.pallas_call(
        functools.partial(_gla_kernel, R=R),
        grid=(Bt, NB),
        in_specs=specs(fwd) + specs(bwd) + [wspec((GLA_RANK, HK)), wspec((1, HK)),
                                            wspec((GLA_RANK, HK)), wspec((1, HK))],
        out_specs=[pl.BlockSpec((R, HV), lambda b, n: (fwd(b, n), 0)),
                   pl.BlockSpec((R, HV), lambda b, n: (bwd(b, n), 0))],
        out_shape=[jax.ShapeDtypeStruct((T, HV), F32), jax.ShapeDtypeStruct((T, HV), F32)],
        scratch_shapes=[pltpu.VMEM((GLA_HEADS, GLA_DV, GLA_DK), F32),
                        pltpu.VMEM((GLA_HEADS, GLA_DV, GLA_DK), F32)],
        compiler_params=_cparams(("parallel", "arbitrary")),
        name="gla",
    )(P, P, P, P, P, P, P, P, awf, abf, awb, abb)


def _mla_prep_kernel(cq_ref, ckv_ref, kr_ref, cos_ref, sin_ref, qn_ref, wuq_ref, kvn_ref, wukv_ref,
                     q_out, k_out, v_out):
    cq = cq_ref[...].astype(F32)
    ms = jnp.sum(cq * cq, axis=-1, keepdims=True) * (1.0 / MLA_Q_LORA)
    hq = (cq * lax.rsqrt(ms + EPS) * qn_ref[...]).astype(BF16)
    qq = _dot(hq, wuq_ref[...])
    ckv = ckv_ref[...].astype(F32)
    ms = jnp.mean(ckv * ckv, axis=-1, keepdims=True)
    hkv = (ckv * lax.rsqrt(ms + EPS) * kvn_ref[...]).astype(BF16)
    kv = _dot(hkv, wukv_ref[...])
    cos = cos_ref[...]
    sin = sin_ref[...]
    kr = kr_ref[...].astype(F32)
    k_pe = (kr[:, :MLA_ROPE] * cos + kr[:, MLA_ROPE:] * sin).astype(BF16)
    scale = MLA_DQK ** -0.5 * LOG2E
    zpad = jnp.zeros((cq.shape[0], MLA_QK_PAD - MLA_DQK), BF16)
    for h in range(MLA_HEADS):
        o = h * 256
        q_pe = qq[:, o + 128:o + 192] * cos + qq[:, o + 192:o + 256] * sin
        q_out[0, h, :, 0:MLA_NOPE] = (qq[:, o:o + MLA_NOPE] * scale).astype(BF16)
        q_out[0, h, :, MLA_NOPE:MLA_DQK] = (q_pe * scale).astype(BF16)
        q_out[0, h, :, MLA_DQK:MLA_QK_PAD] = zpad
        k_out[0, h, :, 0:MLA_NOPE] = kv[:, o:o + MLA_NOPE].astype(BF16)
        k_out[0, h, :, MLA_NOPE:MLA_DQK] = k_pe
        k_out[0, h, :, MLA_DQK:MLA_QK_PAD] = zpad
        v_out[0, h] = kv[:, o + MLA_NOPE:o + 256].astype(BF16)


def _mla_prep(P, cos2, sin2, qn, wuq, kvn, wukv, Bt, S, ts):
    NS = S // ts
    row = lambda b, i: b * NS + i
    wspec = lambda shape: pl.BlockSpec(shape, lambda b, i: (0, 0))
    H = MLA_HEADS
    return pl.pallas_call(
        _mla_prep_kernel,
        grid=(Bt, NS),
        in_specs=[pl.BlockSpec((ts, CQ_PAD), lambda b, i: (row(b, i), COL_CQ // CQ_PAD)),
                  pl.BlockSpec((ts, MLA_KV_LORA), lambda b, i: (row(b, i), COL_CKV // MLA_KV_LORA)),
                  pl.BlockSpec((ts, 128), lambda b, i: (row(b, i), COL_KR // 128)),
                  pl.BlockSpec((ts, MLA_ROPE), lambda b, i: (i, 0)),
                  pl.BlockSpec((ts, MLA_ROPE), lambda b, i: (i, 0)),
                  wspec((1, CQ_PAD)), wspec((CQ_PAD, H * 256)),
                  wspec((1, MLA_KV_LORA)), wspec((MLA_KV_LORA, H * 256))],
        out_specs=[pl.BlockSpec((1, H, ts, MLA_QK_PAD), lambda b, i: (b, 0, i, 0)),
                   pl.BlockSpec((1, H, ts, MLA_QK_PAD), lambda b, i: (b, 0, i, 0)),
                   pl.BlockSpec((1, H, ts, MLA_V), lambda b, i: (b, 0, i, 0))],
        out_shape=[jax.ShapeDtypeStruct((Bt, H, S, MLA_QK_PAD), BF16),
                   jax.ShapeDtypeStruct((Bt, H, S, MLA_QK_PAD), BF16),
                   jax.ShapeDtypeStruct((Bt, H, S, MLA_V), BF16)],
        compiler_params=_cparams(("parallel", "parallel")),
        name="mla_prep",
    )(P, P, P, cos2, sin2, qn, wuq, kvn, wukv)


ATTN_HEADS_PER_STEP = 4


def _attn_kernel(q_ref, k_ref, v_ref, o_ref, m_ref, l_ref, acc_ref):
    kv = pl.program_id(3)
    nl = k_ref.shape[2] // 128

    @pl.when(kv == 0)
    def _():
        m_ref[...] = jnp.full_like(m_ref, -jnp.inf)
        l_ref[...] = jnp.zeros_like(l_ref)
        acc_ref[...] = jnp.zeros_like(acc_ref)

    for h in range(ATTN_HEADS_PER_STEP):
        s = _dot_nt(q_ref[0, h], k_ref[0, h])
        m_prev = m_ref[h]
        m_new = jnp.maximum(m_prev, jnp.max(s, axis=-1, keepdims=True))
        alpha = jnp.exp2(m_prev - m_new)
        ps = [jnp.exp2(s[:, c * 128:(c + 1) * 128] - m_new) for c in range(nl)]
        psum = ps[0]
        for c in range(1, nl):
            psum = psum + ps[c]
        p = jnp.concatenate([x.astype(BF16) for x in ps], axis=-1)
        l_ref[h] = alpha * l_ref[h] + psum
        acc_ref[h] = alpha * acc_ref[h] + _dot(p, v_ref[0, h])
        m_ref[h] = m_new

    @pl.when(kv == pl.num_programs(3) - 1)
    def _():
        for h in range(ATTN_HEADS_PER_STEP):
            l = jnp.sum(l_ref[h], axis=-1, keepdims=True)
            o_ref[0, :, h * MLA_V:(h + 1) * MLA_V] = (acc_ref[h] / l).astype(o_ref.dtype)


def _attention(Q, K, V, tq, tk):
    Bt, H, S, _ = Q.shape
    G = ATTN_HEADS_PER_STEP
    return pl.pallas_call(
        _attn_kernel,
        grid=(Bt, H // G, S // tq, S // tk),
        in_specs=[pl.BlockSpec((1, G, tq, MLA_QK_PAD), lambda b, h, i, j: (b, h, i, 0)),
                  pl.BlockSpec((1, G, tk, MLA_QK_PAD), lambda b, h, i, j: (b, h, j, 0)),
                  pl.BlockSpec((1, G, tk, MLA_V), lambda b, h, i, j: (b, h, j, 0))],
        out_specs=pl.BlockSpec((1, tq, G * MLA_V), lambda b, h, i, j: (b, i, h)),
        out_shape=jax.ShapeDtypeStruct((Bt, S, H * MLA_V), BF16),
        scratch_shapes=[pltpu.VMEM((G, tq, 128), F32), pltpu.VMEM((G, tq, 128), F32),
                        pltpu.VMEM((G, tq, MLA_V), F32)],
        compiler_params=_cparams(("parallel", "parallel", "parallel", "arbitrary")),
        name="attn",
    )(Q, K, V)


def _merge_kernel(x_ref, of_ref, ob_ref, r_ref, gate_ref, oat_ref, gn_ref, wgo_ref, wmo_ref, wout_ref,
                  n2_ref, wq_ref, x1_ref, h2t_ref, qp_ref):
    o = of_ref[...] + ob_ref[...]
    gn = gn_ref[...]
    parts = []
    for h in range(GLA_HEADS):
        vs = slice(h * GLA_DV, (h + 1) * GLA_DV)
        oh = o[:, vs]
        ms = jnp.mean(oh * oh, axis=-1, keepdims=True)
        parts.append(oh * lax.rsqrt(ms + EPS) * gn[:, vs])
    on = jnp.concatenate(parts, axis=-1)
    r = r_ref[...].astype(F32)
    ya = _dot((on * (r * _sigmoid(r))).astype(BF16), wgo_ref[...])
    yb = _dot(oat_ref[...], wmo_ref[...])
    g = _sigmoid(gate_ref[...].astype(F32))
    mix = g[:, :D_MODEL] * ya + g[:, D_MODEL:] * yb
    x1 = x_ref[...] + _dot(mix.astype(BF16), wout_ref[...])
    x1_ref[...] = x1
    ms = jnp.mean(x1 * x1, axis=-1, keepdims=True)
    h2 = x1 * lax.rsqrt(ms + EPS) * n2_ref[...]
    h2t_ref[...] = h2.T.astype(BF16)
    qp_ref[...] = _dot(h2.astype(BF16), wq_ref[...]).astype(BF16)


def _merge(x, o_f, o_b, P, o_att, gn, wgo, wmo, wout, n2, wq, tm):
    T = x.shape[0]
    D = D_MODEL
    QW = PEER_HEADS * 2 * PEER_HALF
    tile = lambda w, c=0: pl.BlockSpec((tm, w), lambda i: (i, c))
    wspec = lambda shape: pl.BlockSpec(shape, lambda i: (0, 0))
    return pl.pallas_call(
        _merge_kernel,
        grid=(T // tm,),
        in_specs=[tile(D), tile(D), tile(D), tile(D, COL_R // D), tile(2 * D, COL_GATE // (2 * D)), tile(D),
                  wspec((1, D)), wspec((D, D)), wspec((D, D)), wspec((D, D)), wspec((1, D)), wspec((D, QW))],
        out_specs=[tile(D), pl.BlockSpec((D, tm), lambda i: (0, i)), tile(QW)],
        out_shape=[jax.ShapeDtypeStruct((T, D), F32), jax.ShapeDtypeStruct((D, T), BF16),
                   jax.ShapeDtypeStruct((T, QW), BF16)],
        compiler_params=_cparams(("parallel",)),
        name="merge",
    )(x, o_f, o_b, P, P, o_att, gn, wgo, wmo, wout, n2, wq)


def _extract16(cur, tie_index):
    n, t = cur.shape
    slot = lax.broadcasted_iota(jnp.int32, (PEER_TOPK, t), 0)
    vals = jnp.zeros((PEER_TOPK, t), F32)
    rank = jnp.full((n, t), float(PEER_TOPK), F32)
    for a in range(PEER_TOPK):
        m = jnp.max(cur, axis=0, keepdims=True)
        sel = cur == m
        if tie_index is not None:
            first = jnp.min(jnp.where(sel, tie_index, float(n * n)), axis=0, keepdims=True)
            sel = tie_index == first
        vals = jnp.where(slot == a, m, vals)
        rank = jnp.where(sel, float(a), rank)
        cur = jnp.where(sel, -jnp.inf, cur)
    return vals, rank


def _taken_error(rank):
    taken = jnp.sum(jnp.where(rank < float(PEER_TOPK), 1.0, 0.0), axis=0, keepdims=True)
    return jnp.abs(taken - float(PEER_TOPK))


_CAND_ROWS = [16] + [8] * (PEER_TOPK - 1)


def _peer_topk_kernel(qp_ref, k1_ref, k2_ref, r1_ref, f1_ref, k2r_ref, f2_ref, s_sc, t_sc, rk_sc, ch_sc):
    K, H = PEER_TOPK, PEER_HEADS
    tt = qp_ref.shape[0]
    k1 = k1_ref[...]
    k2 = k2_ref[...]
    for h in range(H):
        o = h * 2 * PEER_HALF
        s_sc[2 * h] = _dot_nt(k1, qp_ref[:, o:o + PEER_HALF])
        s_sc[2 * h + 1] = _dot_nt(k2, qp_ref[:, o + PEER_HALF:o + 2 * PEER_HALF])

    def stage1(exact):
        err = jnp.zeros((1, tt), F32)
        idx = lax.broadcasted_iota(jnp.int32, (PEER_NKEYS, tt), 0).astype(F32) if exact else None
        for i in range(2 * H):
            vals, rank = _extract16(s_sc[i], idx)
            t_sc[i] = vals
            rk_sc[i] = rank
            if not exact:
                err = jnp.maximum(err, _taken_error(rank))
        return err

    err1 = stage1(False)

    @pl.when(jnp.max(err1) > 0.0)
    def _():
        stage1(True)

    def stage2(exact):
        err = jnp.zeros((1, tt), F32)
        for h in range(H):
            t1 = t_sc[2 * h]
            t2 = t_sc[2 * h + 1]
            slabs, flats = [], []
            for a in range(K):
                nr = _CAND_ROWS[a]
                brow = lax.broadcasted_iota(jnp.int32, (nr, tt), 0)
                slabs.append(jnp.where(brow < K // (a + 1), t1[a:a + 1, :] + t2[0:nr, :], -jnp.inf))
                flats.append((brow + a * K).astype(F32))
            cand = jnp.concatenate(slabs, axis=0)
            _, rank = _extract16(cand, jnp.concatenate(flats, axis=0) if exact else None)
            ch_sc[h] = jnp.where(rank < float(K), 1.0, 0.0)
            if not exact:
                err = jnp.maximum(err, _taken_error(rank))
        return err

    err2 = stage2(False)

    @pl.when(jnp.max(err2) > 0.0)
    def _():
        stage2(True)

    for h in range(H):
        s1, s2 = s_sc[2 * h], s_sc[2 * h + 1]
        t1, t2 = t_sc[2 * h], t_sc[2 * h + 1]
        rank1 = rk_sc[2 * h]
        e1s = jnp.exp(t1 - t1[0:1, :])
        e2s = jnp.exp(t2 - t2[0:1, :])
        z = jnp.zeros((1, tt), F32)
        r1 = jnp.zeros(rank1.shape, F32)
        off = 0
        for a in range(K):
            nr = _CAND_ROWS[a]
            ch = ch_sc[h, off:off + nr, :]
            off += nr
            cnt = jnp.sum(ch, axis=0, keepdims=True)
            z = z + e1s[a:a + 1, :] * jnp.sum(ch * e2s[0:nr, :], axis=0, keepdims=True)
            r1 = jnp.where(rank1 == float(a), cnt, r1)
        r1_ref[h] = r1
        f1_ref[h] = jnp.exp(s1 - t1[0:1, :]) * (0.5 / z)
        k2r_ref[h] = rk_sc[2 * h + 1].astype(BF16)
        f2_ref[h] = jnp.exp(s2 - t2[0:1, :]).astype(BF16)


def _peer_topk(qp, k1, k2, tt):
    T = qp.shape[0]
    H, N = PEER_HEADS, PEER_NKEYS
    out = jax.ShapeDtypeStruct((H, N, T), F32)
    outb = jax.ShapeDtypeStruct((H, N, T), BF16)
    ospec = pl.BlockSpec((H, N, tt), lambda i: (0, 0, i))
    return pl.pallas_call(
        _peer_topk_kernel,
        grid=(T // tt,),
        in_specs=[pl.BlockSpec((tt, H * 2 * PEER_HALF), lambda i: (i, 0)),
                  pl.BlockSpec((N, PEER_HALF), lambda i: (0, 0)),
                  pl.BlockSpec((N, PEER_HALF), lambda i: (0, 0))],
        out_specs=[ospec, ospec, ospec, ospec],
        out_shape=[out, out, outb, outb],
        scratch_shapes=[pltpu.VMEM((2 * H, N, tt), F32), pltpu.VMEM((2 * H, PEER_TOPK, tt), F32),
                        pltpu.VMEM((2 * H, N, tt), F32), pltpu.VMEM((H, sum(_CAND_ROWS), tt), F32)],
        compiler_params=_cparams(("parallel",)),
        name="peer_topk",
    )(qp, k1, k2)


def _peer_main_kernel(h2t_ref, u_ref, vt_ref, r1_ref, f1_ref, k2r_ref, f2_ref, x1_ref, o_ref,
                      acc_ref, s_ref, g_ref, *, ec):
    j = pl.program_id(1)
    N = PEER_NKEYS
    tt = s_ref.shape[2]
    lw = 256

    @pl.when(j == 0)
    def _():
        acc_ref[...] = jnp.zeros_like(acc_ref)
        s_ref[1] = jnp.zeros(s_ref.shape[1:], s_ref.dtype)

    def step(rd, wr):
        s_ref[wr] = _dot(u_ref[...], h2t_ref[...])
        for e in range(ec):
            rows = slice(e * N, (e + 1) * N)
            for c in range(tt // lw):
                cs = slice(c * lw, (c + 1) * lw)
                w = jnp.zeros((N // 16, 16, lw), BF16)
                for h in range(PEER_HEADS):
                    r1 = jnp.broadcast_to(r1_ref[h, e:e + 1, cs], (16, lw)).astype(BF16)[None]
                    f1 = jnp.broadcast_to(f1_ref[h, e:e + 1, cs], (16, lw)).astype(BF16)[None]
                    k2 = k2r_ref[h, :, cs].reshape(N // 16, 16, lw)
                    f2 = f2_ref[h, :, cs].reshape(N // 16, 16, lw)
                    w = w + jnp.where(k2 < r1, f2, jnp.zeros((), BF16)) * f1
                se = s_ref[rd, rows, cs]
                act = se * (1.0 + lax.erf(se * (2.0 ** -0.5)))
                g_ref[rows, cs] = act.astype(BF16) * w.reshape(N, lw)
        acc_ref[...] += _dot(vt_ref[...], g_ref[...])

    @pl.when(j % 2 == 0)
    def _():
        step(1, 0)

    @pl.when(j % 2 == 1)
    def _():
        step(0, 1)

    @pl.when(j == pl.num_programs(1) - 1)
    def _():
        o_ref[...] = x1_ref[...] + acc_ref[...].T


def _peer_main(h2t, u, vt, r1, f1, k2r, f2, x1, tt, ec):
    T = h2t.shape[1]
    D, H, N = D_MODEL, PEER_HEADS, PEER_NKEYS
    nc = N // ec
    lag = lambda j, k: jnp.clip(j - k, 0, nc - 1)
    return pl.pallas_call(
        functools.partial(_peer_main_kernel, ec=ec),
        grid=(T // tt, nc + 1),
        in_specs=[pl.BlockSpec((D, tt), lambda i, j: (0, i)),
                  pl.BlockSpec((ec * N, D), lambda i, j: (lag(j, 0), 0)),
                  pl.BlockSpec((D, ec * N), lambda i, j: (0, lag(j, 1))),
                  pl.BlockSpec((H, ec, tt), lambda i, j: (0, lag(j, 1), i)),
                  pl.BlockSpec((H, ec, tt), lambda i, j: (0, lag(j, 1), i)),
                  pl.BlockSpec((H, N, tt), lambda i, j: (0, 0, i)),
                  pl.BlockSpec((H, N, tt), lambda i, j: (0, 0, i)),
                  pl.BlockSpec((tt, D), lambda i, j: (i, 0))],
        out_specs=pl.BlockSpec((tt, D), lambda i, j: (i, 0)),
        out_shape=jax.ShapeDtypeStruct((T, D), F32),
        scratch_shapes=[pltpu.VMEM((D, tt), F32), pltpu.VMEM((2, ec * N, tt), F32),
                        pltpu.VMEM((ec * N, tt), BF16)],
        compiler_params=_cparams(("parallel", "arbitrary")),
        name="peer_main",
    )(h2t, u, vt, r1, f1, k2r, f2, x1)


def _final_kernel(x_ref, p_ref, wg_ref, wp_ref, fn_ref, o_ref):
    x = x_ref[...]
    gate = _sigmoid(_dot(x.astype(BF16), wg_ref[...]))
    x = x + gate * _dot(p_ref[...].astype(BF16), wp_ref[...])
    ms = jnp.mean(x * x, axis=-1, keepdims=True)
    o_ref[...] = x * lax.rsqrt(ms + EPS) * fn_ref[...]


def _final(x, p, wg, wp, fn, tm):
    T = x.shape[0]
    D = D_MODEL
    return pl.pallas_call(
        _final_kernel,
        grid=(T // tm,),
        in_specs=[pl.BlockSpec((tm, D), lambda i: (i, 0)),
                  pl.BlockSpec((tm, PLE_DIM), lambda i: (i, 0)),
                  pl.BlockSpec((D, D), lambda i: (0, 0)),
                  pl.BlockSpec((PLE_DIM, D), lambda i: (0, 0)),
                  pl.BlockSpec((1, D), lambda i: (0, 0))],
        out_specs=pl.BlockSpec((tm, D), lambda i: (i, 0)),
        out_shape=jax.ShapeDtypeStruct((T, D), F32),
        compiler_params=_cparams(("parallel",)),
        name="final",
    )(x, p, wg, wp, fn)


def _rot_cols(w):
    half = MLA_ROPE // 2
    return jnp.concatenate([-w[..., half:], w[..., :half]], axis=-1)


def _pack_w_in(w_in):
    pts, acc = [], 0
    for sz in (512, 512, 1024, 1024, 2 * GLA_RANK, MLA_Q_LORA, MLA_KV_LORA, MLA_ROPE):
        acc += sz
        pts.append(acc)
    q, k, v, r, a_lr, c_q, c_kv, k_r, gate = jnp.split(w_in, pts, axis=-1)
    z = lambda n: jnp.zeros((D_MODEL, n), w_in.dtype)
    packed = jnp.concatenate([gate, v, r, q, k, c_q, z(CQ_PAD - MLA_Q_LORA), c_kv, k_r, _rot_cols(k_r),
                              a_lr, z(128 - 2 * GLA_RANK)], axis=-1)
    assert packed.shape[-1] == P_COLS
    return packed.astype(BF16)


def _pack_w_uq(w_uq):
    w = w_uq.reshape(MLA_Q_LORA, MLA_HEADS, MLA_DQK)
    pe = w[..., MLA_NOPE:]
    w = jnp.concatenate([w, _rot_cols(pe)], axis=-1).reshape(MLA_Q_LORA, MLA_HEADS * 256)
    return jnp.concatenate([w, jnp.zeros((CQ_PAD - MLA_Q_LORA, MLA_HEADS * 256), w.dtype)], axis=0).astype(BF16)


def _tile(n, pref):
    return pref if n % pref == 0 else n


def kernel(x_prompt, x_sample, p_prompt, p_sample, norm1, w_in, gla_a_w_f, gla_a_b_f, gla_a_w_b, gla_a_b_b, gla_norm, gla_w_o, mla_q_norm, mla_w_uq, mla_kv_norm, mla_w_ukv, mla_w_o, w_out, norm2, peer_w_q, peer_k1, peer_k2, peer_u, peer_v, ple_proj, ple_gate, final_norm):
    B1, S, D = x_prompt.shape
    B2 = x_sample.shape[0]
    assert x_sample.shape[1] == S and norm1.shape[0] == 1, "one layer, equal sequence lengths"
    Bt = B1 + B2
    T = Bt * S
    x = jnp.concatenate([x_prompt, x_sample], axis=0).reshape(T, D)
    p = jnp.concatenate([p_prompt[0], p_sample[0]], axis=0).reshape(T, PLE_DIM)
    row = lambda v: v.reshape(1, -1).astype(F32)

    P = _in_proj(x, row(norm1[0]), _pack_w_in(w_in[0]), _tile(T, 1024), 1024)

    o_f, o_b = _gla(P, gla_a_w_f[0], row(gla_a_b_f[0]), gla_a_w_b[0], row(gla_a_b_b[0]), Bt, S, _tile(S, 256))

    pos = jnp.arange(S, dtype=F32)
    inv = ROPE_THETA ** (-jnp.arange(MLA_ROPE // 2, dtype=F32) * 2.0 / MLA_ROPE)
    ang = pos[:, None] * inv[None, :]
    cos2 = jnp.concatenate([jnp.cos(ang), jnp.cos(ang)], axis=-1)
    sin2 = jnp.concatenate([jnp.sin(ang), jnp.sin(ang)], axis=-1)
    qn = jnp.concatenate([mla_q_norm[0], jnp.zeros((CQ_PAD - MLA_Q_LORA,), F32)]).reshape(1, CQ_PAD)
    Q, K, V = _mla_prep(P, cos2, sin2, qn, _pack_w_uq(mla_w_uq[0]), row(mla_kv_norm[0]),
                        mla_w_ukv[0].astype(BF16), Bt, S, _tile(S, 512))
    o_att = _attention(Q, K, V, _tile(S, 512), _tile(S, 1024)).reshape(T, MLA_HEADS * MLA_V)

    gn = jnp.tile(gla_norm[0], GLA_HEADS).reshape(1, D)
    x1, h2t, qp = _merge(x, o_f, o_b, P, o_att, gn, gla_w_o[0].astype(BF16), mla_w_o[0].astype(BF16),
                        w_out[0].astype(BF16), row(norm2[0]), peer_w_q[0].astype(BF16), _tile(T, 256))

    r1, f1, k2r, f2 = _peer_topk(qp, peer_k1[0].astype(BF16), peer_k2[0].astype(BF16), _tile(T, 256))
    x2 = _peer_main(h2t, peer_u[0].astype(BF16), peer_v[0].T.astype(BF16), r1, f1, k2r, f2, x1,
                    _tile(T, 512), 8)

    y = _final(x2, p, ple_gate[0].astype(BF16), ple_proj[0].astype(BF16), row(final_norm), _tile(T, 512))
    y = y.reshape(Bt, S, D)
    return (y[:B1], y[B1:])
```

```python
import functools

import jax
import jax.numpy as jnp
from jax import lax
from jax.experimental import pallas as pl
from jax.experimental.pallas import tpu as pltpu

F32 = jnp.float32
BF16 = jnp.bfloat16
EPS = 1e-6

D_MODEL = 1024
PLE_DIM = 256
GLA_HEADS, GLA_DK, GLA_DV, GLA_RANK, GLA_GATE_NORM, GLA_CHUNK = 4, 128, 256, 16, 16.0, 64
MLA_HEADS, MLA_Q_LORA, MLA_KV_LORA, MLA_NOPE, MLA_ROPE, MLA_V = 8, 384, 256, 128, 64, 128
MLA_DQK = MLA_NOPE + MLA_ROPE
MLA_QK_PAD = 256
ROPE_THETA = 10000.0
LOG2E = 1.4426950408889634
PEER_HEADS, PEER_NKEYS, PEER_HALF, PEER_TOPK = 8, 128, 128, 16
PEER_N = PEER_NKEYS * PEER_NKEYS

P_COLS = 6144
COL_GATE, COL_V, COL_R, COL_Q, COL_K, COL_CQ, COL_CKV, COL_KR, COL_ALR = (
    0, 2048, 3072, 4096, 4608, 5120, 5632, 5888, 6016)
CQ_PAD = 512

VMEM_LIMIT = 56 * 1024 * 1024


def _cparams(sem):
    return pltpu.CompilerParams(dimension_semantics=sem, vmem_limit_bytes=VMEM_LIMIT)


def _dot(a, b):
    return jnp.dot(a, b, preferred_element_type=F32)


def _dot_nt(a, b):
    return lax.dot_general(a, b, (((1,), (1,)), ((), ())), preferred_element_type=F32)


def _dot_tn(a, b):
    return lax.dot_general(a, b, (((0,), (0,)), ((), ())), preferred_element_type=F32)


def _sigmoid(x):
    return 1.0 / (1.0 + jnp.exp(-x))


def _in_proj_kernel(x_ref, g_ref, w_ref, o_ref, h_ref):
    @pl.when(pl.program_id(1) == 0)
    def _():
        x = x_ref[...]
        ms = jnp.mean(x * x, axis=-1, keepdims=True)
        h_ref[...] = (x * lax.rsqrt(ms + EPS) * g_ref[...]).astype(BF16)

    o_ref[...] = _dot(h_ref[...], w_ref[...]).astype(o_ref.dtype)


def _in_proj(x, g, w, tm, tn):
    T = x.shape[0]
    return pl.pallas_call(
        _in_proj_kernel,
        grid=(T // tm, P_COLS // tn),
        in_specs=[pl.BlockSpec((tm, D_MODEL), lambda i, j: (i, 0)),
                  pl.BlockSpec((1, D_MODEL), lambda i, j: (0, 0)),
                  pl.BlockSpec((D_MODEL, tn), lambda i, j: (0, j))],
        out_specs=pl.BlockSpec((tm, tn), lambda i, j: (i, j)),
        out_shape=jax.ShapeDtypeStruct((T, P_COLS), BF16),
        scratch_shapes=[pltpu.VMEM((tm, D_MODEL), BF16)],
        compiler_params=_cparams(("parallel", "arbitrary")),
        name="in_proj",
    )(x, g, w)


def _gla_direction(q_ref, k_ref, v_ref, a_ref, aw_ref, ab_ref, st_ref, o_ref, a_off, backward, R):
    C = GLA_CHUNK
    nc = R // C
    row = lax.broadcasted_iota(jnp.int32, (R, R), 0)
    col = lax.broadcasted_iota(jnp.int32, (R, R), 1)
    same = (row // C) == (col // C)
    cum = same & ((col >= row) if backward else (col <= row))
    att_mask = same & ((col > row) if backward else (col <= row))
    tri = jnp.where(cum, 1.0, 0.0).astype(BF16)
    blk = jnp.where(same, 1.0, 0.0).astype(BF16)
    a = a_ref[:, a_off:a_off + GLA_RANK].astype(BF16)
    order = range(nc - 1, -1, -1) if backward else range(nc)
    for h in range(GLA_HEADS):
        ks = slice(h * GLA_DK, (h + 1) * GLA_DK)
        vs = slice(h * GLA_DV, (h + 1) * GLA_DV)
        z = _dot(a, aw_ref[:, ks].astype(BF16)) + ab_ref[:, ks]
        la = (jnp.minimum(z, 0.0) - jnp.log(1.0 + jnp.exp(-jnp.abs(z)))) / GLA_GATE_NORM
        la_hi = la.astype(BF16)
        la_lo = (la - la_hi.astype(F32)).astype(BF16)
        b = _dot(tri, la_hi) + _dot(tri, la_lo)
        tot = _dot(blk, la_hi) + _dot(blk, la_lo)
        qh = q_ref[:, ks].astype(F32) * (GLA_DK ** -0.5)
        kh = k_ref[:, ks].astype(F32)
        vh = v_ref[:, vs].astype(BF16)
        q_e = (qh * jnp.exp(b)).astype(BF16)
        k_e = (kh * jnp.exp(-b)).astype(BF16)
        k_s = (kh * jnp.exp(tot - b)).astype(BF16)
        dec = jnp.exp(tot)
        att = jnp.where(att_mask, _dot_nt(q_e, k_e), 0.0).astype(BF16)
        o_intra = _dot(att, vh)
        for c in order:
            rs = slice(c * C, (c + 1) * C)
            st = st_ref[h]
            o_ref[rs, vs] = o_intra[rs] + _dot_nt(q_e[rs], st.astype(BF16))
            st_ref[h] = dec[c * C:c * C + 1, :] * st + _dot_tn(vh[rs], k_s[rs])


def _gla_kernel(qf_ref, kf_ref, vf_ref, af_ref, qb_ref, kb_ref, vb_ref, ab_ref,
                awf_ref, abf_ref, awb_ref, abb_ref, of_ref, ob_ref, stf_ref, stb_ref, *, R):
    @pl.when(pl.program_id(1) == 0)
    def _():
        stf_ref[...] = jnp.zeros_like(stf_ref)
        stb_ref[...] = jnp.zeros_like(stb_ref)

    _gla_direction(qf_ref, kf_ref, vf_ref, af_ref, awf_ref, abf_ref, stf_ref, of_ref, 0, False, R)
    _gla_direction(qb_ref, kb_ref, vb_ref, ab_ref, awb_ref, abb_ref, stb_ref, ob_ref, GLA_RANK, True, R)


def _gla(P, awf, abf, awb, abb, Bt, S, R):
    T = Bt * S
    NB = S // R
    HK, HV = GLA_HEADS * GLA_DK, GLA_HEADS * GLA_DV
    fwd = lambda b, n: b * NB + n
    bwd = lambda b, n: b * NB + (NB - 1 - n)

    def specs(rowf):
        return [pl.BlockSpec((R, HK), lambda b, n: (rowf(b, n), COL_Q // HK)),
                pl.BlockSpec((R, HK), lambda b, n: (rowf(b, n), COL_K // HK)),
                pl.BlockSpec((R, HV), lambda b, n: (rowf(b, n), COL_V // HV)),
                pl.BlockSpec((R, 128), lambda b, n: (rowf(b, n), COL_ALR // 128))]

    wspec = lambda shape: pl.BlockSpec(shape, lambda b, n: (0, 0))
    return pl.pallas_call(
        functools.partial(_gla_kernel, R=R),
        grid=(Bt, NB),
        in_specs=specs(fwd) + specs(bwd) + [wspec((GLA_RANK, HK)), wspec((1, HK)),
                                            wspec((GLA_RANK, HK)), wspec((1, HK))],
        out_specs=[pl.BlockSpec((R, HV), lambda b, n: (fwd(b, n), 0)),
                   pl.BlockSpec((R, HV), lambda b, n: (bwd(b, n), 0))],
        out_shape=[jax.ShapeDtypeStruct((T, HV), F32), jax.ShapeDtypeStruct((T, HV), F32)],
        scratch_shapes=[pltpu.VMEM((GLA_HEADS, GLA_DV, GLA_DK), F32),
                        pltpu.VMEM((GLA_HEADS, GLA_DV, GLA_DK), F32)],
        compiler_params=_cparams(("parallel", "arbitrary")),
        name="gla",
    )(P, P, P, P, P, P, P, P, awf, abf, awb, abb)


def _mla_prep_kernel(cq_ref, ckv_ref, kr_ref, cos_ref, sin_ref, qn_ref, wuq_ref, kvn_ref, wukv_ref,
                     q_out, k_out, v_out):
    cq = cq_ref[...].astype(F32)
    ms = jnp.sum(cq * cq, axis=-1, keepdims=True) * (1.0 / MLA_Q_LORA)
    hq = (cq * lax.rsqrt(ms + EPS) * qn_ref[...]).astype(BF16)
    qq = _dot(hq, wuq_ref[...])
    ckv = ckv_ref[...].astype(F32)
    ms = jnp.mean(ckv * ckv, axis=-1, keepdims=True)
    hkv = (ckv * lax.rsqrt(ms + EPS) * kvn_ref[...]).astype(BF16)
    kv = _dot(hkv, wukv_ref[...])
    cos = cos_ref[...]
    sin = sin_ref[...]
    kr = kr_ref[...].astype(F32)
    k_pe = (kr[:, :MLA_ROPE] * cos + kr[:, MLA_ROPE:] * sin).astype(BF16)
    scale = MLA_DQK ** -0.5 * LOG2E
    zpad = jnp.zeros((cq.shape[0], MLA_QK_PAD - MLA_DQK), BF16)
    for h in range(MLA_HEADS):
        o = h * 256
        q_pe = qq[:, o + 128:o + 192] * cos + qq[:, o + 192:o + 256] * sin
        q_out[0, h, :, 0:MLA_NOPE] = (qq[:, o:o + MLA_NOPE] * scale).astype(BF16)
        q_out[0, h, :, MLA_NOPE:MLA_DQK] = (q_pe * scale).astype(BF16)
        q_out[0, h, :, MLA_DQK:MLA_QK_PAD] = zpad
        k_out[0, h, :, 0:MLA_NOPE] = kv[:, o:o + MLA_NOPE].astype(BF16)
        k_out[0, h, :, MLA_NOPE:MLA_DQK] = k_pe
        k_out[0, h, :, MLA_DQK:MLA_QK_PAD] = zpad
        v_out[0, h] = kv[:, o + MLA_NOPE:o + 256].astype(BF16)


def _mla_prep(P, cos2, sin2, qn, wuq, kvn, wukv, Bt, S, ts):
    NS = S // ts
    row = lambda b, i: b * NS + i
    wspec = lambda shape: pl.BlockSpec(shape, lambda b, i: (0, 0))
    H = MLA_HEADS
    return pl.pallas_call(
        _mla_prep_kernel,
        grid=(Bt, NS),
        in_specs=[pl.BlockSpec((ts, CQ_PAD), lambda b, i: (row(b, i), COL_CQ // CQ_PAD)),
                  pl.BlockSpec((ts, MLA_KV_LORA), lambda b, i: (row(b, i), COL_CKV // MLA_KV_LORA)),
                  pl.BlockSpec((ts, 128), lambda b, i: (row(b, i), COL_KR // 128)),
                  pl.BlockSpec((ts, MLA_ROPE), lambda b, i: (i, 0)),
                  pl.BlockSpec((ts, MLA_ROPE), lambda b, i: (i, 0)),
                  wspec((1, CQ_PAD)), wspec((CQ_PAD, H * 256)),
                  wspec((1, MLA_KV_LORA)), wspec((MLA_KV_LORA, H * 256))],
        out_specs=[pl.BlockSpec((1, H, ts, MLA_QK_PAD), lambda b, i: (b, 0, i, 0)),
                   pl.BlockSpec((1, H, ts, MLA_QK_PAD), lambda b, i: (b, 0, i, 0)),
                   pl.BlockSpec((1, H, ts, MLA_V), lambda b, i: (b, 0, i, 0))],
        out_shape=[jax.ShapeDtypeStruct((Bt, H, S, MLA_QK_PAD), BF16),
                   jax.ShapeDtypeStruct((Bt, H, S, MLA_QK_PAD), BF16),
                   jax.ShapeDtypeStruct((Bt, H, S, MLA_V), BF16)],
        compiler_params=_cparams(("parallel", "parallel")),
        name="mla_prep",
    )(P, P, P, cos2, sin2, qn, wuq, kvn, wukv)


ATTN_HEADS_PER_STEP = 2


def _attn_kernel(q_ref, k_ref, v_ref, o_ref, m_ref, l_ref, acc_ref):
    kv = pl.program_id(3)
    nl = k_ref.shape[2] // 128

    @pl.when(kv == 0)
    def _():
        m_ref[...] = jnp.full_like(m_ref, -jnp.inf)
        l_ref[...] = jnp.zeros_like(l_ref)
        acc_ref[...] = jnp.zeros_like(acc_ref)

    for h in range(ATTN_HEADS_PER_STEP):
        s = _dot_nt(q_ref[0, h], k_ref[0, h])
        m_prev = m_ref[h]
        m_new = jnp.maximum(m_prev, jnp.max(s, axis=-1, keepdims=True))
        alpha = jnp.exp2(m_prev - m_new)
        ps = [jnp.exp2(s[:, c * 128:(c + 1) * 128] - m_new) for c in range(nl)]
        psum = ps[0]
        for c in range(1, nl):
            psum = psum + ps[c]
        p = jnp.concatenate([x.astype(BF16) for x in ps], axis=-1)
        l_ref[h] = alpha * l_ref[h] + psum
        acc_ref[h] = alpha * acc_ref[h] + _dot(p, v_ref[0, h])
        m_ref[h] = m_new

    @pl.when(kv == pl.num_programs(3) - 1)
    def _():
        for h in range(ATTN_HEADS_PER_STEP):
            l = jnp.sum(l_ref[h], axis=-1, keepdims=True)
            o_ref[0, :, h * MLA_V:(h + 1) * MLA_V] = (acc_ref[h] / l).astype(o_ref.dtype)


def _attention(Q, K, V, tq, tk):
    Bt, H, S, _ = Q.shape
    G = ATTN_HEADS_PER_STEP
    return pl.pallas_call(
        _attn_kernel,
        grid=(Bt, H // G, S // tq, S // tk),
        in_specs=[pl.BlockSpec((1, G, tq, MLA_QK_PAD), lambda b, h, i, j: (b, h, i, 0)),
                  pl.BlockSpec((1, G, tk, MLA_QK_PAD), lambda b, h, i, j: (b, h, j, 0)),
                  pl.BlockSpec((1, G, tk, MLA_V), lambda b, h, i, j: (b, h, j, 0))],
        out_specs=pl.BlockSpec((1, tq, G * MLA_V), lambda b, h, i, j: (b, i, h)),
        out_shape=jax.ShapeDtypeStruct((Bt, S, H * MLA_V), BF16),
        scratch_shapes=[pltpu.VMEM((G, tq, 128), F32), pltpu.VMEM((G, tq, 128), F32),
                        pltpu.VMEM((G, tq, MLA_V), F32)],
        compiler_params=_cparams(("parallel", "parallel", "parallel", "arbitrary")),
        name="attn",
    )(Q, K, V)


def _merge_kernel(x_ref, of_ref, ob_ref, r_ref, gate_ref, oat_ref, gn_ref, wgo_ref, wmo_ref, wout_ref,
                  n2_ref, wq_ref, x1_ref, h2t_ref, qp_ref):
    o = of_ref[...] + ob_ref[...]
    gn = gn_ref[...]
    parts = []
    for h in range(GLA_HEADS):
        vs = slice(h * GLA_DV, (h + 1) * GLA_DV)
        oh = o[:, vs]
        ms = jnp.mean(oh * oh, axis=-1, keepdims=True)
        parts.append(oh * lax.rsqrt(ms + EPS) * gn[:, vs])
    on = jnp.concatenate(parts, axis=-1)
    r = r_ref[...].astype(F32)
    ya = _dot((on * (r * _sigmoid(r))).astype(BF16), wgo_ref[...])
    yb = _dot(oat_ref[...], wmo_ref[...])
    g = _sigmoid(gate_ref[...].astype(F32))
    mix = g[:, :D_MODEL] * ya + g[:, D_MODEL:] * yb
    x1 = x_ref[...] + _dot(mix.astype(BF16), wout_ref[...])
    x1_ref[...] = x1
    ms = jnp.mean(x1 * x1, axis=-1, keepdims=True)
    h2 = x1 * lax.rsqrt(ms + EPS) * n2_ref[...]
    h2t_ref[...] = h2.T.astype(BF16)
    qp_ref[...] = _dot(h2.astype(BF16), wq_ref[...]).astype(BF16)


def _merge(x, o_f, o_b, P, o_att, gn, wgo, wmo, wout, n2, wq, tm):
    T = x.shape[0]
    D = D_MODEL
    QW = PEER_HEADS * 2 * PEER_HALF
    tile = lambda w, c=0: pl.BlockSpec((tm, w), lambda i: (i, c))
    wspec = lambda shape: pl.BlockSpec(shape, lambda i: (0, 0))
    return pl.pallas_call(
        _merge_kernel,
        grid=(T // tm,),
        in_specs=[tile(D), tile(D), tile(D), tile(D, COL_R // D), tile(2 * D, COL_GATE // (2 * D)), tile(D),
                  wspec((1, D)), wspec((D, D)), wspec((D, D)), wspec((D, D)), wspec((1, D)), wspec((D, QW))],
        out_specs=[tile(D), pl.BlockSpec((D, tm), lambda i: (0, i)), tile(QW)],
        out_shape=[jax.ShapeDtypeStruct((T, D), F32), jax.ShapeDtypeStruct((D, T), BF16),
                   jax.ShapeDtypeStruct((T, QW), BF16)],
        compiler_params=_cparams(("parallel",)),
        name="merge",
    )(x, o_f, o_b, P, P, o_att, gn, wgo, wmo, wout, n2, wq)


def _extract16(cur, tie_index):
    n, t = cur.shape
    slot = lax.broadcasted_iota(jnp.int32, (PEER_TOPK, t), 0)
    vals = jnp.zeros((PEER_TOPK, t), F32)
    rank = jnp.full((n, t), float(PEER_TOPK), F32)
    for a in range(PEER_TOPK):
        m = jnp.max(cur, axis=0, keepdims=True)
        sel = cur == m
        if tie_index is not None:
            first = jnp.min(jnp.where(sel, tie_index, float(n * n)), axis=0, keepdims=True)
            sel = tie_index == first
        vals = jnp.where(slot == a, m, vals)
        rank = jnp.where(sel, float(a), rank)
        cur = jnp.where(sel, -jnp.inf, cur)
    return vals, rank


def _taken_error(rank):
    taken = jnp.sum(jnp.where(rank < float(PEER_TOPK), 1.0, 0.0), axis=0, keepdims=True)
    return jnp.abs(taken - float(PEER_TOPK))


_CAND_ROWS = [16] + [8] * (PEER_TOPK - 1)


def _peer_topk_kernel(qp_ref, k1_ref, k2_ref, r1_ref, f1_ref, k2r_ref, f2_ref, s_sc, t_sc, rk_sc, ch_sc):
    K, H = PEER_TOPK, PEER_HEADS
    tt = qp_ref.shape[0]
    k1 = k1_ref[...]
    k2 = k2_ref[...]
    for h in range(H):
        o = h * 2 * PEER_HALF
        s_sc[2 * h] = _dot_nt(k1, qp_ref[:, o:o + PEER_HALF])
        s_sc[2 * h + 1] = _dot_nt(k2, qp_ref[:, o + PEER_HALF:o + 2 * PEER_HALF])

    def stage1(exact):
        err = jnp.zeros((1, tt), F32)
        idx = lax.broadcasted_iota(jnp.int32, (PEER_NKEYS, tt), 0).astype(F32) if exact else None
        for i in range(2 * H):
            vals, rank = _extract16(s_sc[i], idx)
            t_sc[i] = vals
            rk_sc[i] = rank
            if not exact:
                err = jnp.maximum(err, _taken_error(rank))
        return err

    err1 = stage1(False)

    @pl.when(jnp.max(err1) > 0.0)
    def _():
        stage1(True)

    def stage2(exact):
        err = jnp.zeros((1, tt), F32)
        for h in range(H):
            t1 = t_sc[2 * h]
            t2 = t_sc[2 * h + 1]
            slabs, flats = [], []
            for a in range(K):
                nr = _CAND_ROWS[a]
                brow = lax.broadcasted_iota(jnp.int32, (nr, tt), 0)
                slabs.append(jnp.where(brow < K // (a + 1), t1[a:a + 1, :] + t2[0:nr, :], -jnp.inf))
                flats.append((brow + a * K).astype(F32))
            cand = jnp.concatenate(slabs, axis=0)
            _, rank = _extract16(cand, jnp.concatenate(flats, axis=0) if exact else None)
            ch_sc[h] = jnp.where(rank < float(K), 1.0, 0.0)
            if not exact:
                err = jnp.maximum(err, _taken_error(rank))
        return err

    err2 = stage2(False)

    @pl.when(jnp.max(err2) > 0.0)
    def _():
        stage2(True)

    for h in range(H):
        s1, s2 = s_sc[2 * h], s_sc[2 * h + 1]
        t1, t2 = t_sc[2 * h], t_sc[2 * h + 1]
        rank1 = rk_sc[2 * h]
        e1s = jnp.exp(t1 - t1[0:1, :])
        e2s = jnp.exp(t2 - t2[0:1, :])
        z = jnp.zeros((1, tt), F32)
        r1 = jnp.zeros(rank1.shape, F32)
        off = 0
        for a in range(K):
            nr = _CAND_ROWS[a]
            ch = ch_sc[h, off:off + nr, :]
            off += nr
            cnt = jnp.sum(ch, axis=0, keepdims=True)
            z = z + e1s[a:a + 1, :] * jnp.sum(ch * e2s[0:nr, :], axis=0, keepdims=True)
            r1 = jnp.where(rank1 == float(a), cnt, r1)
        r1_ref[h] = r1
        f1_ref[h] = jnp.exp(s1 - t1[0:1, :]) * (0.5 / z)
        k2r_ref[h] = rk_sc[2 * h + 1].astype(BF16)
        f2_ref[h] = jnp.exp(s2 - t2[0:1, :]).astype(BF16)


def _peer_topk(qp, k1, k2, tt):
    T = qp.shape[0]
    H, N = PEER_HEADS, PEER_NKEYS
    out = jax.ShapeDtypeStruct((H, N, T), F32)
    outb = jax.ShapeDtypeStruct((H, N, T), BF16)
    ospec = pl.BlockSpec((H, N, tt), lambda i: (0, 0, i))
    return pl.pallas_call(
        _peer_topk_kernel,
        grid=(T // tt,),
        in_specs=[pl.BlockSpec((tt, H * 2 * PEER_HALF), lambda i: (i, 0)),
                  pl.BlockSpec((N, PEER_HALF), lambda i: (0, 0)),
                  pl.BlockSpec((N, PEER_HALF), lambda i: (0, 0))],
        out_specs=[ospec, ospec, ospec, ospec],
        out_shape=[out, out, outb, outb],
        scratch_shapes=[pltpu.VMEM((2 * H, N, tt), F32), pltpu.VMEM((2 * H, PEER_TOPK, tt), F32),
                        pltpu.VMEM((2 * H, N, tt), F32), pltpu.VMEM((H, sum(_CAND_ROWS), tt), F32)],
        compiler_params=_cparams(("parallel",)),
        name="peer_topk",
    )(qp, k1, k2)


def _peer_main_kernel(h2t_ref, u_ref, vt_ref, r1_ref, f1_ref, k2r_ref, f2_ref, o_ref,
                      acc_ref, s_ref, g_ref, *, ec):
    j = pl.program_id(1)
    N = PEER_NKEYS
    tt = s_ref.shape[2]
    lw = 256

    @pl.when(j == 0)
    def _():
        acc_ref[...] = jnp.zeros_like(acc_ref)
        s_ref[1] = jnp.zeros(s_ref.shape[1:], s_ref.dtype)

    def step(rd, wr):
        s_ref[wr] = _dot(u_ref[...], h2t_ref[...])
        for e in range(ec):
            rows = slice(e * N, (e + 1) * N)
            for c in range(tt // lw):
                cs = slice(c * lw, (c + 1) * lw)
                w = jnp.zeros((N // 16, 16, lw), BF16)
                for h in range(PEER_HEADS):
                    r1 = jnp.broadcast_to(r1_ref[h, e:e + 1, cs], (16, lw)).astype(BF16)[None]
                    f1 = jnp.broadcast_to(f1_ref[h, e:e + 1, cs], (16, lw)).astype(BF16)[None]
                    k2 = k2r_ref[h, :, cs].reshape(N // 16, 16, lw)
                    f2 = f2_ref[h, :, cs].reshape(N // 16, 16, lw)
                    w = w + jnp.where(k2 < r1, f2, jnp.zeros((), BF16)) * f1
                se = s_ref[rd, rows, cs]
                act = se * (1.0 + lax.erf(se * (2.0 ** -0.5)))
                g_ref[rows, cs] = act.astype(BF16) * w.reshape(N, lw)
        acc_ref[...] += _dot(vt_ref[...], g_ref[...])

    @pl.when(j % 2 == 0)
    def _():
        step(1, 0)

    @pl.when(j % 2 == 1)
    def _():
        step(0, 1)

    @pl.when(j == pl.num_programs(1) - 1)
    def _():
        o_ref[...] = acc_ref[...].T


def _peer_main(h2t, u, vt, r1, f1, k2r, f2, tt, ec):
    T = h2t.shape[1]
    D, H, N = D_MODEL, PEER_HEADS, PEER_NKEYS
    nc = N // ec
    lag = lambda j, k: jnp.clip(j - k, 0, nc - 1)
    return pl.pallas_call(
        functools.partial(_peer_main_kernel, ec=ec),
        grid=(T // tt, nc + 1),
        in_specs=[pl.BlockSpec((D, tt), lambda i, j: (0, i)),
                  pl.BlockSpec((ec * N, D), lambda i, j: (lag(j, 0), 0)),
                  pl.BlockSpec((D, ec * N), lambda i, j: (0, lag(j, 1))),
                  pl.BlockSpec((H, ec, tt), lambda i, j: (0, lag(j, 1), i)),
                  pl.BlockSpec((H, ec, tt), lambda i, j: (0, lag(j, 1), i)),
                  pl.BlockSpec((H, N, tt), lambda i, j: (0, 0, i)),
                  pl.BlockSpec((H, N, tt), lambda i, j: (0, 0, i))],
        out_specs=pl.BlockSpec((tt, D), lambda i, j: (i, 0)),
        out_shape=jax.ShapeDtypeStruct((T, D), F32),
        scratch_shapes=[pltpu.VMEM((D, tt), F32), pltpu.VMEM((2, ec * N, tt), F32),
                        pltpu.VMEM((ec * N, tt), BF16)],
        compiler_params=_cparams(("parallel", "arbitrary")),
        name="peer_main",
    )(h2t, u, vt, r1, f1, k2r, f2)


def _final_kernel(x_ref, peer_ref, p_ref, wg_ref, wp_ref, fn_ref, o_ref):
    x = x_ref[...] + peer_ref[...]
    gate = _sigmoid(_dot(x.astype(BF16), wg_ref[...]))
    x = x + gate * _dot(p_ref[...].astype(BF16), wp_ref[...])
    ms = jnp.mean(x * x, axis=-1, keepdims=True)
    o_ref[...] = x * lax.rsqrt(ms + EPS) * fn_ref[...]


def _final(x, peer, p, wg, wp, fn, tm):
    T = x.shape[0]
    D = D_MODEL
    return pl.pallas_call(
        _final_kernel,
        grid=(T // tm,),
        in_specs=[pl.BlockSpec((tm, D), lambda i: (i, 0)),
                  pl.BlockSpec((tm, D), lambda i: (i, 0)),
                  pl.BlockSpec((tm, PLE_DIM), lambda i: (i, 0)),
                  pl.BlockSpec((D, D), lambda i: (0, 0)),
                  pl.BlockSpec((PLE_DIM, D), lambda i: (0, 0)),
                  pl.BlockSpec((1, D), lambda i: (0, 0))],
        out_specs=pl.BlockSpec((tm, D), lambda i: (i, 0)),
        out_shape=jax.ShapeDtypeStruct((T, D), F32),
        compiler_params=_cparams(("parallel",)),
        name="final",
    )(x, peer, p, wg, wp, fn)


def _rot_cols(w):
    half = MLA_ROPE // 2
    return jnp.concatenate([-w[..., half:], w[..., :half]], axis=-1)


def _pack_w_in(w_in):
    pts, acc = [], 0
    for sz in (512, 512, 1024, 1024, 2 * GLA_RANK, MLA_Q_LORA, MLA_KV_LORA, MLA_ROPE):
        acc += sz
        pts.append(acc)
    q, k, v, r, a_lr, c_q, c_kv, k_r, gate = jnp.split(w_in, pts, axis=-1)
    z = lambda n: jnp.zeros((D_MODEL, n), w_in.dtype)
    packed = jnp.concatenate([gate, v, r, q, k, c_q, z(CQ_PAD - MLA_Q_LORA), c_kv, k_r, _rot_cols(k_r),
                              a_lr, z(128 - 2 * GLA_RANK)], axis=-1)
    assert packed.shape[-1] == P_COLS
    return packed.astype(BF16)


def _pack_w_uq(w_uq):
    w = w_uq.reshape(MLA_Q_LORA, MLA_HEADS, MLA_DQK)
    pe = w[..., MLA_NOPE:]
    w = jnp.concatenate([w, _rot_cols(pe)], axis=-1).reshape(MLA_Q_LORA, MLA_HEADS * 256)
    return jnp.concatenate([w, jnp.zeros((CQ_PAD - MLA_Q_LORA, MLA_HEADS * 256), w.dtype)], axis=0).astype(BF16)


def _tile(n, pref):
    return pref if n % pref == 0 else n


def kernel(x_prompt, x_sample, p_prompt, p_sample, norm1, w_in, gla_a_w_f, gla_a_b_f, gla_a_w_b, gla_a_b_b, gla_norm, gla_w_o, mla_q_norm, mla_w_uq, mla_kv_norm, mla_w_ukv, mla_w_o, w_out, norm2, peer_w_q, peer_k1, peer_k2, peer_u, peer_v, ple_proj, ple_gate, final_norm):
    B1, S, D = x_prompt.shape
    B2 = x_sample.shape[0]
    assert x_sample.shape[1] == S and norm1.shape[0] == 1, "one layer, equal sequence lengths"
    Bt = B1 + B2
    T = Bt * S
    x = jnp.concatenate([x_prompt, x_sample], axis=0).reshape(T, D)
    p = jnp.concatenate([p_prompt[0], p_sample[0]], axis=0).reshape(T, PLE_DIM)
    row = lambda v: v.reshape(1, -1).astype(F32)

    P = _in_proj(x, row(norm1[0]), _pack_w_in(w_in[0]), _tile(T, 1024), 1024)

    o_f, o_b = _gla(P, gla_a_w_f[0], row(gla_a_b_f[0]), gla_a_w_b[0], row(gla_a_b_b[0]), Bt, S, _tile(S, 256))

    pos = jnp.arange(S, dtype=F32)
    inv = ROPE_THETA ** (-jnp.arange(MLA_ROPE // 2, dtype=F32) * 2.0 / MLA_ROPE)
    ang = pos[:, None] * inv[None, :]
    cos2 = jnp.concatenate([jnp.cos(ang), jnp.cos(ang)], axis=-1)
    sin2 = jnp.concatenate([jnp.sin(ang), jnp.sin(ang)], axis=-1)
    qn = jnp.concatenate([mla_q_norm[0], jnp.zeros((CQ_PAD - MLA_Q_LORA,), F32)]).reshape(1, CQ_PAD)
    Q, K, V = _mla_prep(P, cos2, sin2, qn, _pack_w_uq(mla_w_uq[0]), row(mla_kv_norm[0]),
                        mla_w_ukv[0].astype(BF16), Bt, S, _tile(S, 512))
    o_att = _attention(Q, K, V, _tile(S, 1024), _tile(S, 1024)).reshape(T, MLA_HEADS * MLA_V)

    gn = jnp.tile(gla_norm[0], GLA_HEADS).reshape(1, D)
    x1, h2t, qp = _merge(x, o_f, o_b, P, o_att, gn, gla_w_o[0].astype(BF16), mla_w_o[0].astype(BF16),
                        w_out[0].astype(BF16), row(norm2[0]), peer_w_q[0].astype(BF16), _tile(T, 256))

    r1, f1, k2r, f2 = _peer_topk(qp, peer_k1[0].astype(BF16), peer_k2[0].astype(BF16), _tile(T, 256))
    peer = _peer_main(h2t, peer_u[0].astype(BF16), peer_v[0].T.astype(BF16), r1, f1, k2r, f2,
                      _tile(T, 1024), 8)

    y = _final(x1, peer, p, ple_gate[0].astype(BF16), ple_proj[0].astype(BF16), row(final_norm), _tile(T, 512))
    y = y.reshape(Bt, S, D)
    return (y[:B1], y[B1:])
```

```python
import functools

import jax
import jax.numpy as jnp
from jax import lax
from jax.experimental import pallas as pl
from jax.experimental.pallas import tpu as pltpu

F32 = jnp.float32
BF16 = jnp.bfloat16
EPS = 1e-6

D_MODEL = 1024
PLE_DIM = 256
GLA_HEADS, GLA_DK, GLA_DV, GLA_RANK, GLA_GATE_NORM, GLA_CHUNK = 4, 128, 256, 16, 16.0, 64
MLA_HEADS, MLA_Q_LORA, MLA_KV_LORA, MLA_NOPE, MLA_ROPE, MLA_V = 8, 384, 256, 128, 64, 128
MLA_DQK = MLA_NOPE + MLA_ROPE
MLA_QK_PAD = 256
ROPE_THETA = 10000.0
LOG2E = 1.4426950408889634
PEER_HEADS, PEER_NKEYS, PEER_HALF, PEER_TOPK = 8, 128, 128, 16
PEER_N = PEER_NKEYS * PEER_NKEYS

P_COLS = 6144
COL_GATE, COL_V, COL_R, COL_Q, COL_K, COL_CQ, COL_CKV, COL_KR, COL_ALR = (
    0, 2048, 3072, 4096, 4608, 5120, 5632, 5888, 6016)
CQ_PAD = 512

VMEM_LIMIT = 56 * 1024 * 1024


def _cparams(sem):
    return pltpu.CompilerParams(dimension_semantics=sem, vmem_limit_bytes=VMEM_LIMIT)


def _dot(a, b):
    return jnp.dot(a, b, preferred_element_type=F32)


def _dot_nt(a, b):
    return lax.dot_general(a, b, (((1,), (1,)), ((), ())), preferred_element_type=F32)


def _dot_tn(a, b):
    return lax.dot_general(a, b, (((0,), (0,)), ((), ())), preferred_element_type=F32)


def _sigmoid(x):
    return 1.0 / (1.0 + jnp.exp(-x))


def _pair_specs(block, n1):
    return [pl.BlockSpec(block, lambda i, *_: (jnp.minimum(i, n1 - 1), 0)),
            pl.BlockSpec(block, lambda i, *_: (jnp.maximum(i - n1, 0), 0))]


def _pair_tile(T1, T2, pref):
    t = pref
    while T1 % t or T2 % t:
        t //= 2
    return t


def _in_proj_kernel(xp_ref, xs_ref, g_ref, w_ref, o_ref, h_ref, *, n1):
    @pl.when(pl.program_id(1) == 0)
    def _():
        x = jnp.where(pl.program_id(0) < n1, xp_ref[...], xs_ref[...])
        ms = jnp.mean(x * x, axis=-1, keepdims=True)
        h_ref[...] = (x * lax.rsqrt(ms + EPS) * g_ref[...]).astype(BF16)

    o_ref[...] = _dot(h_ref[...], w_ref[...]).astype(o_ref.dtype)


def _in_proj(xp, xs, g, w, tm, tn):
    n1, n2 = xp.shape[0] // tm, xs.shape[0] // tm
    T = xp.shape[0] + xs.shape[0]
    return pl.pallas_call(
        functools.partial(_in_proj_kernel, n1=n1),
        grid=(n1 + n2, P_COLS // tn),
        in_specs=_pair_specs((tm, D_MODEL), n1) + [
            pl.BlockSpec((1, D_MODEL), lambda i, j: (0, 0)),
            pl.BlockSpec((D_MODEL, tn), lambda i, j: (0, j))],
        out_specs=pl.BlockSpec((tm, tn), lambda i, j: (i, j)),
        out_shape=jax.ShapeDtypeStruct((T, P_COLS), BF16),
        scratch_shapes=[pltpu.VMEM((tm, D_MODEL), BF16)],
        compiler_params=_cparams(("parallel", "arbitrary")),
        name="in_proj",
    )(xp, xs, g, w)


def _gla_direction(q_ref, k_ref, v_ref, a_ref, aw_ref, ab_ref, st_ref, o_ref, a_off, backward, R):
    C = GLA_CHUNK
    nc = R // C
    row = lax.broadcasted_iota(jnp.int32, (R, R), 0)
    col = lax.broadcasted_iota(jnp.int32, (R, R), 1)
    same = (row // C) == (col // C)
    cum = same & ((col >= row) if backward else (col <= row))
    att_mask = same & ((col > row) if backward else (col <= row))
    tri = jnp.where(cum, 1.0, 0.0).astype(BF16)
    blk = jnp.where(same, 1.0, 0.0).astype(BF16)
    a = a_ref[:, a_off:a_off + GLA_RANK].astype(BF16)
    order = range(nc - 1, -1, -1) if backward else range(nc)
    for h in range(GLA_HEADS):
        ks = slice(h * GLA_DK, (h + 1) * GLA_DK)
        vs = slice(h * GLA_DV, (h + 1) * GLA_DV)
        z = _dot(a, aw_ref[:, ks].astype(BF16)) + ab_ref[:, ks]
        la = (jnp.minimum(z, 0.0) - jnp.log(1.0 + jnp.exp(-jnp.abs(z)))) / GLA_GATE_NORM
        la_hi = la.astype(BF16)
        la_lo = (la - la_hi.astype(F32)).astype(BF16)
        b = _dot(tri, la_hi) + _dot(tri, la_lo)
        tot = _dot(blk, la_hi) + _dot(blk, la_lo)
        qh = q_ref[:, ks].astype(F32) * (GLA_DK ** -0.5)
        kh = k_ref[:, ks].astype(F32)
        vh = v_ref[:, vs].astype(BF16)
        q_e = (qh * jnp.exp(b)).astype(BF16)
        k_e = (kh * jnp.exp(-b)).astype(BF16)
        k_s = (kh * jnp.exp(tot - b)).astype(BF16)
        dec = jnp.exp(tot)
        att = jnp.where(att_mask, _dot_nt(q_e, k_e), 0.0).astype(BF16)
        o_intra = _dot(att, vh)
        for c in order:
            rs = slice(c * C, (c + 1) * C)
            st = st_ref[h]
            o_ref[rs, vs] = o_intra[rs] + _dot_nt(q_e[rs], st.astype(BF16))
            st_ref[h] = dec[c * C:c * C + 1, :] * st + _dot_tn(vh[rs], k_s[rs])


def _gla_kernel(qf_ref, kf_ref, vf_ref, af_ref, qb_ref, kb_ref, vb_ref, ab_ref,
                awf_ref, abf_ref, awb_ref, abb_ref, of_ref, ob_ref, stf_ref, stb_ref, *, R):
    @pl.when(pl.program_id(1) == 0)
    def _():
        stf_ref[...] = jnp.zeros_like(stf_ref)
        stb_ref[...] = jnp.zeros_like(stb_ref)

    _gla_direction(qf_ref, kf_ref, vf_ref, af_ref, awf_ref, abf_ref, stf_ref, of_ref, 0, False, R)
    _gla_direction(qb_ref, kb_ref, vb_ref, ab_ref, awb_ref, abb_ref, stb_ref, ob_ref, GLA_RANK, True, R)


def _gla(P, awf, abf, awb, abb, Bt, S, R):
    T = Bt * S
    NB = S // R
    HK, HV = GLA_HEADS * GLA_DK, GLA_HEADS * GLA_DV
    fwd = lambda b, n: b * NB + n
    bwd = lambda b, n: b * NB + (NB - 1 - n)

    def specs(rowf):
        return [pl.BlockSpec((R, HK), lambda b, n: (rowf(b, n), COL_Q // HK)),
                pl.BlockSpec((R, HK), lambda b, n: (rowf(b, n), COL_K // HK)),
                pl.BlockSpec((R, HV), lambda b, n: (rowf(b, n), COL_V // HV)),
                pl.BlockSpec((R, 128), lambda b, n: (rowf(b, n), COL_ALR // 128))]

    wspec = lambda shape: pl.BlockSpec(shape, lambda b, n: (0, 0))
    return pl.pallas_call(
        functools.partial(_gla_kernel, R=R),
        grid=(Bt, NB),
        in_specs=specs(fwd) + specs(bwd) + [wspec((GLA_RANK, HK)), wspec((1, HK)),
                                            wspec((GLA_RANK, HK)), wspec((1, HK))],
        out_specs=[pl.BlockSpec((R, HV), lambda b, n: (fwd(b, n), 0)),
                   pl.BlockSpec((R, HV), lambda b, n: (bwd(b, n), 0))],
        out_shape=[jax.ShapeDtypeStruct((T, HV), F32), jax.ShapeDtypeStruct((T, HV), F32)],
        scratch_shapes=[pltpu.VMEM((GLA_HEADS, GLA_DV, GLA_DK), F32),
                        pltpu.VMEM((GLA_HEADS, GLA_DV, GLA_DK), F32)],
        compiler_params=_cparams(("parallel", "arbitrary")),
        name="gla",
    )(P, P, P, P, P, P, P, P, awf, abf, awb, abb)


def _mla_prep_kernel(cq_ref, ckv_ref, kr_ref, cos_ref, sin_ref, qn_ref, wuq_ref, kvn_ref, wukv_ref,
                     q_out, k_out, v_out):
    cq = cq_ref[...].astype(F32)
    ms = jnp.sum(cq * cq, axis=-1, keepdims=True) * (1.0 / MLA_Q_LORA)
    hq = (cq * lax.rsqrt(ms + EPS) * qn_ref[...]).astype(BF16)
    qq = _dot(hq, wuq_ref[...])
    ckv = ckv_ref[...].astype(F32)
    ms = jnp.mean(ckv * ckv, axis=-1, keepdims=True)
    hkv = (ckv * lax.rsqrt(ms + EPS) * kvn_ref[...]).astype(BF16)
    kv = _dot(hkv, wukv_ref[...])
    cos = cos_ref[...]
    sin = sin_ref[...]
    kr = kr_ref[...].astype(F32)
    k_pe = (kr[:, :MLA_ROPE] * cos + kr[:, MLA_ROPE:] * sin).astype(BF16)
    scale = MLA_DQK ** -0.5 * LOG2E
    zpad = jnp.zeros((cq.shape[0], MLA_QK_PAD - MLA_DQK), BF16)
    for h in range(MLA_HEADS):
        o = h * 256
        q_pe = qq[:, o + 128:o + 192] * cos + qq[:, o + 192:o + 256] * sin
        q_out[0, h, :, 0:MLA_NOPE] = (qq[:, o:o + MLA_NOPE] * scale).astype(BF16)
        q_out[0, h, :, MLA_NOPE:MLA_DQK] = (q_pe * scale).astype(BF16)
        q_out[0, h, :, MLA_DQK:MLA_QK_PAD] = zpad
        k_out[0, h, :, 0:MLA_NOPE] = kv[:, o:o + MLA_NOPE].astype(BF16)
        k_out[0, h, :, MLA_NOPE:MLA_DQK] = k_pe
        k_out[0, h, :, MLA_DQK:MLA_QK_PAD] = zpad
        v_out[0, h] = kv[:, o + MLA_NOPE:o + 256].astype(BF16)


def _mla_prep(P, cos2, sin2, qn, wuq, kvn, wukv, Bt, S, ts):
    NS = S // ts
    row = lambda b, i: b * NS + i
    wspec = lambda shape: pl.BlockSpec(shape, lambda b, i: (0, 0))
    H = MLA_HEADS
    return pl.pallas_call(
        _mla_prep_kernel,
        grid=(Bt, NS),
        in_specs=[pl.BlockSpec((ts, CQ_PAD), lambda b, i: (row(b, i), COL_CQ // CQ_PAD)),
                  pl.BlockSpec((ts, MLA_KV_LORA), lambda b, i: (row(b, i), COL_CKV // MLA_KV_LORA)),
                  pl.BlockSpec((ts, 128), lambda b, i: (row(b, i), COL_KR // 128)),
                  pl.BlockSpec((ts, MLA_ROPE), lambda b, i: (i, 0)),
                  pl.BlockSpec((ts, MLA_ROPE), lambda b, i: (i, 0)),
                  wspec((1, CQ_PAD)), wspec((CQ_PAD, H * 256)),
                  wspec((1, MLA_KV_LORA)), wspec((MLA_KV_LORA, H * 256))],
        out_specs=[pl.BlockSpec((1, H, ts, MLA_QK_PAD), lambda b, i: (b, 0, i, 0)),
                   pl.BlockSpec((1, H, ts, MLA_QK_PAD), lambda b, i: (b, 0, i, 0)),
                   pl.BlockSpec((1, H, ts, MLA_V), lambda b, i: (b, 0, i, 0))],
        out_shape=[jax.ShapeDtypeStruct((Bt, H, S, MLA_QK_PAD), BF16),
                   jax.ShapeDtypeStruct((Bt, H, S, MLA_QK_PAD), BF16),
                   jax.ShapeDtypeStruct((Bt, H, S, MLA_V), BF16)],
        compiler_params=_cparams(("parallel", "parallel")),
        name="mla_prep",
    )(P, P, P, cos2, sin2, qn, wuq, kvn, wukv)


ATTN_HEADS_PER_STEP = 2


def _attn_kernel(q_ref, k_ref, v_ref, o_ref, m_ref, l_ref, acc_ref):
    kv = pl.program_id(3)
    nl = k_ref.shape[2] // 128

    @pl.when(kv == 0)
    def _():
        m_ref[...] = jnp.full_like(m_ref, -jnp.inf)
        l_ref[...] = jnp.zeros_like(l_ref)
        acc_ref[...] = jnp.zeros_like(acc_ref)

    for h in range(ATTN_HEADS_PER_STEP):
        s = _dot_nt(q_ref[0, h], k_ref[0, h])
        m_prev = m_ref[h]
        m_new = jnp.maximum(m_prev, jnp.max(s, axis=-1, keepdims=True))
        alpha = jnp.exp2(m_prev - m_new)
        ps = [jnp.exp2(s[:, c * 128:(c + 1) * 128] - m_new) for c in range(nl)]
        psum = ps[0]
        for c in range(1, nl):
            psum = psum + ps[c]
        p = jnp.concatenate([x.astype(BF16) for x in ps], axis=-1)
        l_ref[h] = alpha * l_ref[h] + psum
        acc_ref[h] = alpha * acc_ref[h] + _dot(p, v_ref[0, h])
        m_ref[h] = m_new

    @pl.when(kv == pl.num_programs(3) - 1)
    def _():
        for h in range(ATTN_HEADS_PER_STEP):
            l = jnp.sum(l_ref[h], axis=-1, keepdims=True)
            o_ref[0, :, h * MLA_V:(h + 1) * MLA_V] = (acc_ref[h] / l).astype(o_ref.dtype)


def _attention(Q, K, V, tq, tk):
    Bt, H, S, _ = Q.shape
    G = ATTN_HEADS_PER_STEP
    return pl.pallas_call(
        _attn_kernel,
        grid=(Bt, H // G, S // tq, S // tk),
        in_specs=[pl.BlockSpec((1, G, tq, MLA_QK_PAD), lambda b, h, i, j: (b, h, i, 0)),
                  pl.BlockSpec((1, G, tk, MLA_QK_PAD), lambda b, h, i, j: (b, h, j, 0)),
                  pl.BlockSpec((1, G, tk, MLA_V), lambda b, h, i, j: (b, h, j, 0))],
        out_specs=pl.BlockSpec((1, tq, G * MLA_V), lambda b, h, i, j: (b, i, h)),
        out_shape=jax.ShapeDtypeStruct((Bt, S, H * MLA_V), BF16),
        scratch_shapes=[pltpu.VMEM((G, tq, 128), F32), pltpu.VMEM((G, tq, 128), F32),
                        pltpu.VMEM((G, tq, MLA_V), F32)],
        compiler_params=_cparams(("parallel", "parallel", "parallel", "arbitrary")),
        name="attn",
    )(Q, K, V)


def _merge_kernel(xp_ref, xs_ref, of_ref, ob_ref, r_ref, gate_ref, oat_ref, gn_ref, wgo_ref, wmo_ref, wout_ref,
                  n2_ref, wq_ref, x1_ref, h2t_ref, qp_ref, *, n1):
    o = of_ref[...] + ob_ref[...]
    gn = gn_ref[...]
    parts = []
    for h in range(GLA_HEADS):
        vs = slice(h * GLA_DV, (h + 1) * GLA_DV)
        oh = o[:, vs]
        ms = jnp.mean(oh * oh, axis=-1, keepdims=True)
        parts.append(oh * lax.rsqrt(ms + EPS) * gn[:, vs])
    on = jnp.concatenate(parts, axis=-1)
    r = r_ref[...].astype(F32)
    ya = _dot((on * (r * _sigmoid(r))).astype(BF16), wgo_ref[...])
    yb = _dot(oat_ref[...], wmo_ref[...])
    g = _sigmoid(gate_ref[...].astype(F32))
    mix = g[:, :D_MODEL] * ya + g[:, D_MODEL:] * yb
    x = jnp.where(pl.program_id(0) < n1, xp_ref[...], xs_ref[...])
    x1 = x + _dot(mix.astype(BF16), wout_ref[...])
    x1_ref[...] = x1
    ms = jnp.mean(x1 * x1, axis=-1, keepdims=True)
    h2 = x1 * lax.rsqrt(ms + EPS) * n2_ref[...]
    h2t_ref[...] = h2.T.astype(BF16)
    qp_ref[...] = _dot(h2.astype(BF16), wq_ref[...]).astype(BF16)


def _merge(xp, xs, o_f, o_b, P, o_att, gn, wgo, wmo, wout, n2, wq, tm):
    T = xp.shape[0] + xs.shape[0]
    n1 = xp.shape[0] // tm
    D = D_MODEL
    QW = PEER_HEADS * 2 * PEER_HALF
    tile = lambda w, c=0: pl.BlockSpec((tm, w), lambda i: (i, c))
    wspec = lambda shape: pl.BlockSpec(shape, lambda i: (0, 0))
    return pl.pallas_call(
        functools.partial(_merge_kernel, n1=n1),
        grid=(T // tm,),
        in_specs=_pair_specs((tm, D), n1) + [
            tile(D), tile(D), tile(D, COL_R // D), tile(2 * D, COL_GATE // (2 * D)), tile(D),
            wspec((1, D)), wspec((D, D)), wspec((D, D)), wspec((D, D)), wspec((1, D)), wspec((D, QW))],
        out_specs=[tile(D), pl.BlockSpec((D, tm), lambda i: (0, i)), tile(QW)],
        out_shape=[jax.ShapeDtypeStruct((T, D), F32), jax.ShapeDtypeStruct((D, T), BF16),
                   jax.ShapeDtypeStruct((T, QW), BF16)],
        compiler_params=_cparams(("parallel",)),
        name="merge",
    )(xp, xs, o_f, o_b, P, P, o_att, gn, wgo, wmo, wout, n2, wq)


def _extract16(cur, tie_index):
    n, t = cur.shape
    slot = lax.broadcasted_iota(jnp.int32, (PEER_TOPK, t), 0)
    vals = jnp.zeros((PEER_TOPK, t), F32)
    rank = jnp.full((n, t), float(PEER_TOPK), F32)
    for a in range(PEER_TOPK):
        m = jnp.max(cur, axis=0, keepdims=True)
        sel = cur == m
        if tie_index is not None:
            first = jnp.min(jnp.where(sel, tie_index, float(n * n)), axis=0, keepdims=True)
            sel = tie_index == first
        vals = jnp.where(slot == a, m, vals)
        rank = jnp.where(sel, float(a), rank)
        cur = jnp.where(sel, -jnp.inf, cur)
    return vals, rank


def _taken_error(rank):
    taken = jnp.sum(jnp.where(rank < float(PEER_TOPK), 1.0, 0.0), axis=0, keepdims=True)
    return jnp.abs(taken - float(PEER_TOPK))


_CAND_AB = [
    [(0, b) for b in range(8)],
    [(0, b) for b in range(8, 16)],
    [(1, b) for b in range(8)],
    [(2, b) for b in range(5)] + [(4, b) for b in range(3)],
    [(3, b) for b in range(4)] + [(5, 0), (5, 1), (6, 0), (6, 1)],
    [(7, 0), (7, 1)] + [(a, 0) for a in range(8, 14)],
    [(14, 0), (15, 0)] + [None] * 6,
]
assert sorted(p for t in _CAND_AB for p in t if p) == sorted(
    (a, b) for a in range(PEER_TOPK) for b in range(PEER_TOPK) if (a + 1) * (b + 1) <= PEER_TOPK)


def _candidate_tiles(t1, t2):
    tt = t1.shape[1]
    row = lax.broadcasted_iota(jnp.int32, (8, tt), 0)
    lo1, hi1, lo2, hi2 = t1[0:8], t1[8:16], t2[0:8], t2[8:16]
    r1 = lambda a: t1[a:a + 1, :]
    up = lambda x, k: pltpu.roll(x, k, 0)
    hi1r = up(hi1, 2)
    b0 = t2[0:1, :]
    tiles = [
        r1(0) + lo2,
        r1(0) + hi2,
        r1(1) + lo2,
        jnp.where(row < 5, r1(2) + lo2, r1(4) + up(lo2, 5)),
        jnp.where(row < 4, r1(3) + lo2, jnp.where(row < 6, r1(5) + up(lo2, 4), r1(6) + up(lo2, 6))),
        jnp.where(row < 2, r1(7) + lo2, hi1r + b0),
        jnp.where(row < 2, hi1r + b0, -jnp.inf),
    ]
    return jnp.concatenate(tiles, axis=0)


def _row_constants(values, tt):
    n = len(values)
    row = lax.broadcasted_iota(jnp.int32, (n, tt), 0)
    out = jnp.full((n, tt), float(values[-1]), F32)
    for r in range(n - 1):
        out = jnp.where(row == r, float(values[r]), out)
    return out


def _peer_topk_kernel(qp_ref, k1_ref, k2_ref, r1_ref, f1_ref, k2r_ref, f2_ref, s_sc, t_sc, rk_sc, cnt_sc, z_sc):
    K, H = PEER_TOPK, PEER_HEADS
    tt = qp_ref.shape[0]
    k1 = k1_ref[...]
    k2 = k2_ref[...]
    for h in range(H):
        o = h * 2 * PEER_HALF
        s_sc[2 * h] = _dot_nt(k1, qp_ref[:, o:o + PEER_HALF])
        s_sc[2 * h + 1] = _dot_nt(k2, qp_ref[:, o + PEER_HALF:o + 2 * PEER_HALF])

    def stage1(exact):
        err = jnp.zeros((1, tt), F32)
        idx = lax.broadcasted_iota(jnp.int32, (PEER_NKEYS, tt), 0).astype(F32) if exact else None
        for i in range(2 * H):
            vals, rank = _extract16(s_sc[i], idx)
            t_sc[i] = vals
            rk_sc[i] = rank
            if not exact:
                err = jnp.maximum(err, _taken_error(rank))
        return err

    err1 = stage1(False)

    @pl.when(jnp.max(err1) > 0.0)
    def _():
        stage1(True)

    pairs = [p for tile in _CAND_AB for p in tile]
    slot = lax.broadcasted_iota(jnp.int32, (K, tt), 0)

    def stage2(exact):
        err = jnp.zeros((1, tt), F32)
        flat = _row_constants([K * K if p is None else p[0] * K + p[1] for p in pairs], tt) if exact else None
        for h in range(H):
            cand = _candidate_tiles(t_sc[2 * h], t_sc[2 * h + 1])
            _, rank = _extract16(cand, flat)
            chosen = rank < float(K)
            ch = jnp.where(chosen, 1.0, 0.0)
            z = jnp.sum(jnp.where(chosen, jnp.exp(cand - cand[0:1, :]), 0.0), axis=0, keepdims=True)
            z_sc[h] = jnp.broadcast_to(z, (8, tt))
            cnt = jnp.zeros((K, tt), F32)
            for a in range(K):
                rows = [r for r, p in enumerate(pairs) if p is not None and p[0] == a]
                ca = jnp.sum(ch[rows[0]:rows[-1] + 1, :], axis=0, keepdims=True)
                cnt = jnp.where(slot == a, ca, cnt)
            cnt_sc[h] = cnt
            if not exact:
                err = jnp.maximum(err, _taken_error(rank))
        return err

    err2 = stage2(False)

    @pl.when(jnp.max(err2) > 0.0)
    def _():
        stage2(True)

    for h in range(H):
        s1, s2 = s_sc[2 * h], s_sc[2 * h + 1]
        t1, t2 = t_sc[2 * h], t_sc[2 * h + 1]
        rank1 = rk_sc[2 * h]
        cnt = cnt_sc[h]
        r1 = jnp.zeros(rank1.shape, F32)
        for a in range(K):
            r1 = jnp.where(rank1 == float(a), cnt[a:a + 1, :], r1)
        r1_ref[h] = r1
        f1_ref[h] = jnp.exp(s1 - t1[0:1, :]) * (0.5 / z_sc[h, 0:1, :])
        k2r_ref[h] = rk_sc[2 * h + 1].astype(BF16)
        f2_ref[h] = jnp.exp(s2 - t2[0:1, :]).astype(BF16)


def _peer_topk(qp, k1, k2, tt):
    T = qp.shape[0]
    H, N = PEER_HEADS, PEER_NKEYS
    out = jax.ShapeDtypeStruct((H, N, T), F32)
    outb = jax.ShapeDtypeStruct((H, N, T), BF16)
    ospec = pl.BlockSpec((H, N, tt), lambda i: (0, 0, i))
    return pl.pallas_call(
        _peer_topk_kernel,
        grid=(T // tt,),
        in_specs=[pl.BlockSpec((tt, H * 2 * PEER_HALF), lambda i: (i, 0)),
                  pl.BlockSpec((N, PEER_HALF), lambda i: (0, 0)),
                  pl.BlockSpec((N, PEER_HALF), lambda i: (0, 0))],
        out_specs=[ospec, ospec, ospec, ospec],
        out_shape=[out, out, outb, outb],
        scratch_shapes=[pltpu.VMEM((2 * H, N, tt), F32), pltpu.VMEM((2 * H, PEER_TOPK, tt), F32),
                        pltpu.VMEM((2 * H, N, tt), F32), pltpu.VMEM((H, PEER_TOPK, tt), F32),
                        pltpu.VMEM((H, 8, tt), F32)],
        compiler_params=_cparams(("parallel",)),
        name="peer_topk",
    )(qp, k1, k2)


def _peer_main_kernel(h2t_ref, u_ref, vt_ref, r1_ref, f1_ref, k2r_ref, f2_ref, o_ref,
                      acc_ref, s_ref, g_ref, *, ec):
    j = pl.program_id(1)
    N = PEER_NKEYS
    tt = s_ref.shape[2]
    lw = 256

    @pl.when(j == 0)
    def _():
        acc_ref[...] = jnp.zeros_like(acc_ref)
        s_ref[1] = jnp.zeros(s_ref.shape[1:], s_ref.dtype)

    def step(rd, wr):
        s_ref[wr] = _dot(u_ref[...], h2t_ref[...])
        for e in range(ec):
            rows = slice(e * N, (e + 1) * N)
            for c in range(tt // lw):
                cs = slice(c * lw, (c + 1) * lw)
                w = jnp.zeros((N // 16, 16, lw), BF16)
                for h in range(PEER_HEADS):
                    r1 = jnp.broadcast_to(r1_ref[h, e:e + 1, cs], (16, lw)).astype(BF16)[None]
                    f1 = jnp.broadcast_to(f1_ref[h, e:e + 1, cs], (16, lw)).astype(BF16)[None]
                    k2 = k2r_ref[h, :, cs].reshape(N // 16, 16, lw)
                    f2 = f2_ref[h, :, cs].reshape(N // 16, 16, lw)
                    w = w + jnp.where(k2 < r1, f2, jnp.zeros((), BF16)) * f1
                se = s_ref[rd, rows, cs]
                act = se * (1.0 + lax.erf(se * (2.0 ** -0.5)))
                g_ref[rows, cs] = act.astype(BF16) * w.reshape(N, lw)
        acc_ref[...] += _dot(vt_ref[...], g_ref[...])

    @pl.when(j % 2 == 0)
    def _():
        step(1, 0)

    @pl.when(j % 2 == 1)
    def _():
        step(0, 1)

    @pl.when(j == pl.num_programs(1) - 1)
    def _():
        o_ref[...] = acc_ref[...].T


def _peer_main(h2t, u, vt, r1, f1, k2r, f2, tt, ec):
    T = h2t.shape[1]
    D, H, N = D_MODEL, PEER_HEADS, PEER_NKEYS
    nc = N // ec
    lag = lambda j, k: jnp.clip(j - k, 0, nc - 1)
    return pl.pallas_call(
        functools.partial(_peer_main_kernel, ec=ec),
        grid=(T // tt, nc + 1),
        in_specs=[pl.BlockSpec((D, tt), lambda i, j: (0, i)),
                  pl.BlockSpec((ec * N, D), lambda i, j: (lag(j, 0), 0)),
                  pl.BlockSpec((D, ec * N), lambda i, j: (0, lag(j, 1))),
                  pl.BlockSpec((H, ec, tt), lambda i, j: (0, lag(j, 1), i)),
                  pl.BlockSpec((H, ec, tt), lambda i, j: (0, lag(j, 1), i)),
                  pl.BlockSpec((H, N, tt), lambda i, j: (0, 0, i)),
                  pl.BlockSpec((H, N, tt), lambda i, j: (0, 0, i))],
        out_specs=pl.BlockSpec((tt, D), lambda i, j: (i, 0)),
        out_shape=jax.ShapeDtypeStruct((T, D), F32),
        scratch_shapes=[pltpu.VMEM((D, tt), F32), pltpu.VMEM((2, ec * N, tt), F32),
                        pltpu.VMEM((ec * N, tt), BF16)],
        compiler_params=_cparams(("parallel", "arbitrary")),
        name="peer_main",
    )(h2t, u, vt, r1, f1, k2r, f2)


def _final_kernel(x_ref, peer_ref, pp_ref, ps_ref, wg_ref, wp_ref, fn_ref, yp_ref, ys_ref, *, n1):
    i = pl.program_id(0)
    x = x_ref[...] + peer_ref[...]
    gate = _sigmoid(_dot(x.astype(BF16), wg_ref[...]))
    p = jnp.where(i < n1, pp_ref[...], ps_ref[...])
    x = x + gate * _dot(p.astype(BF16), wp_ref[...])
    ms = jnp.mean(x * x, axis=-1, keepdims=True)
    y = x * lax.rsqrt(ms + EPS) * fn_ref[...]

    @pl.when(i < n1)
    def _():
        yp_ref[...] = y

    @pl.when(i >= n1)
    def _():
        ys_ref[...] = y


def _final(x, peer, pp, ps, wg, wp, fn, tm):
    T = x.shape[0]
    n1 = pp.shape[0] // tm
    D = D_MODEL
    return pl.pallas_call(
        functools.partial(_final_kernel, n1=n1),
        grid=(T // tm,),
        in_specs=[pl.BlockSpec((tm, D), lambda i: (i, 0)),
                  pl.BlockSpec((tm, D), lambda i: (i, 0))] + _pair_specs((tm, PLE_DIM), n1) + [
                  pl.BlockSpec((D, D), lambda i: (0, 0)),
                  pl.BlockSpec((PLE_DIM, D), lambda i: (0, 0)),
                  pl.BlockSpec((1, D), lambda i: (0, 0))],
        out_specs=_pair_specs((tm, D), n1),
        out_shape=[jax.ShapeDtypeStruct((pp.shape[0], D), F32), jax.ShapeDtypeStruct((ps.shape[0], D), F32)],
        compiler_params=_cparams(("arbitrary",)),
        name="final",
    )(x, peer, pp, ps, wg, wp, fn)


def _rot_cols(w):
    half = MLA_ROPE // 2
    return jnp.concatenate([-w[..., half:], w[..., :half]], axis=-1)


def _pack_w_in(w_in):
    pts, acc = [], 0
    for sz in (512, 512, 1024, 1024, 2 * GLA_RANK, MLA_Q_LORA, MLA_KV_LORA, MLA_ROPE):
        acc += sz
        pts.append(acc)
    q, k, v, r, a_lr, c_q, c_kv, k_r, gate = jnp.split(w_in, pts, axis=-1)
    z = lambda n: jnp.zeros((D_MODEL, n), w_in.dtype)
    packed = jnp.concatenate([gate, v, r, q, k, c_q, z(CQ_PAD - MLA_Q_LORA), c_kv, k_r, _rot_cols(k_r),
                              a_lr, z(128 - 2 * GLA_RANK)], axis=-1)
    assert packed.shape[-1] == P_COLS
    return packed.astype(BF16)


def _pack_w_uq(w_uq):
    w = w_uq.reshape(MLA_Q_LORA, MLA_HEADS, MLA_DQK)
    pe = w[..., MLA_NOPE:]
    w = jnp.concatenate([w, _rot_cols(pe)], axis=-1).reshape(MLA_Q_LORA, MLA_HEADS * 256)
    return jnp.concatenate([w, jnp.zeros((CQ_PAD - MLA_Q_LORA, MLA_HEADS * 256), w.dtype)], axis=0).astype(BF16)


def _tile(n, pref):
    return pref if n % pref == 0 else n


def kernel(x_prompt, x_sample, p_prompt, p_sample, norm1, w_in, gla_a_w_f, gla_a_b_f, gla_a_w_b, gla_a_b_b, gla_norm, gla_w_o, mla_q_norm, mla_w_uq, mla_kv_norm, mla_w_ukv, mla_w_o, w_out, norm2, peer_w_q, peer_k1, peer_k2, peer_u, peer_v, ple_proj, ple_gate, final_norm):
    B1, S, D = x_prompt.shape
    B2 = x_sample.shape[0]
    assert x_sample.shape[1] == S and norm1.shape[0] == 1, "one layer, equal sequence lengths"
    Bt = B1 + B2
    T = Bt * S
    T1, T2 = B1 * S, B2 * S
    xp, xs = x_prompt.reshape(T1, D), x_sample.reshape(T2, D)
    row = lambda v: v.reshape(1, -1).astype(F32)

    P = _in_proj(xp, xs, row(norm1[0]), _pack_w_in(w_in[0]), _pair_tile(T1, T2, 1024), 1024)

    o_f, o_b = _gla(P, gla_a_w_f[0], row(gla_a_b_f[0]), gla_a_w_b[0], row(gla_a_b_b[0]), Bt, S, _tile(S, 256))

    pos = jnp.arange(S, dtype=F32)
    inv = ROPE_THETA ** (-jnp.arange(MLA_ROPE // 2, dtype=F32) * 2.0 / MLA_ROPE)
    ang = pos[:, None] * inv[None, :]
    cos2 = jnp.concatenate([jnp.cos(ang), jnp.cos(ang)], axis=-1)
    sin2 = jnp.concatenate([jnp.sin(ang), jnp.sin(ang)], axis=-1)
    qn = jnp.concatenate([mla_q_norm[0], jnp.zeros((CQ_PAD - MLA_Q_LORA,), F32)]).reshape(1, CQ_PAD)
    Q, K, V = _mla_prep(P, cos2, sin2, qn, _pack_w_uq(mla_w_uq[0]), row(mla_kv_norm[0]),
                        mla_w_ukv[0].astype(BF16), Bt, S, _tile(S, 512))
    o_att = _attention(Q, K, V, _tile(S, 1024), _tile(S, 1024)).reshape(T, MLA_HEADS * MLA_V)

    gn = jnp.tile(gla_norm[0], GLA_HEADS).reshape(1, D)
    x1, h2t, qp = _merge(xp, xs, o_f, o_b, P, o_att, gn, gla_w_o[0].astype(BF16), mla_w_o[0].astype(BF16),
                        w_out[0].astype(BF16), row(norm2[0]), peer_w_q[0].astype(BF16), _pair_tile(T1, T2, 256))

    r1, f1, k2r, f2 = _peer_topk(qp, peer_k1[0].astype(BF16), peer_k2[0].astype(BF16), _tile(T, 256))
    peer = _peer_main(h2t, peer_u[0].astype(BF16), peer_v[0].T.astype(BF16), r1, f1, k2r, f2,
                      _tile(T, 1024), 8)

    yp, ys = _final(x1, peer, p_prompt[0].reshape(T1, PLE_DIM), p_sample[0].reshape(T2, PLE_DIM),
                    ple_gate[0].astype(BF16), ple_proj[0].astype(BF16), row(final_norm), _pair_tile(T1, T2, 512))
    return (yp.reshape(B1, S, D), ys.reshape(B2, S, D))
```

```python
import functools

import jax
import jax.numpy as jnp
from jax import lax
from jax.experimental import pallas as pl
from jax.experimental.pallas import tpu as pltpu

F32 = jnp.float32
BF16 = jnp.bfloat16
EPS = 1e-6

D_MODEL = 1024
PLE_DIM = 256
GLA_HEADS, GLA_DK, GLA_DV, GLA_RANK, GLA_GATE_NORM, GLA_CHUNK = 4, 128, 256, 16, 16.0, 64
MLA_HEADS, MLA_Q_LORA, MLA_KV_LORA, MLA_NOPE, MLA_ROPE, MLA_V = 8, 384, 256, 128, 64, 128
MLA_DQK = MLA_NOPE + MLA_ROPE
MLA_QK_PAD = 256
ROPE_THETA = 10000.0
LOG2E = 1.4426950408889634
PEER_HEADS, PEER_NKEYS, PEER_HALF, PEER_TOPK = 8, 128, 128, 16
PEER_N = PEER_NKEYS * PEER_NKEYS

P_COLS = 6144
COL_GATE, COL_V, COL_R, COL_Q, COL_K, COL_CQ, COL_CKV, COL_KR, COL_ALR = (
    0, 2048, 3072, 4096, 4608, 5120, 5632, 5888, 6016)
CQ_PAD = 512

VMEM_LIMIT = 56 * 1024 * 1024


def _cparams(sem):
    return pltpu.CompilerParams(dimension_semantics=sem, vmem_limit_bytes=VMEM_LIMIT)


def _dot(a, b):
    return jnp.dot(a, b, preferred_element_type=F32)


def _dot_nt(a, b):
    return lax.dot_general(a, b, (((1,), (1,)), ((), ())), preferred_element_type=F32)


def _dot_tn(a, b):
    return lax.dot_general(a, b, (((0,), (0,)), ((), ())), preferred_element_type=F32)


def _sigmoid(x):
    return 1.0 / (1.0 + jnp.exp(-x))


def _pair_specs(block, n1):
    return [pl.BlockSpec(block, lambda i, *_: (jnp.minimum(i, n1 - 1), 0)),
            pl.BlockSpec(block, lambda i, *_: (jnp.maximum(i - n1, 0), 0))]


def _pair_tile(T1, T2, pref):
    t = pref
    while T1 % t or T2 % t:
        t //= 2
    return t


def _in_proj_kernel(xp_ref, xs_ref, g_ref, w_ref, o_ref, h_ref, *, n1):
    @pl.when(pl.program_id(1) == 0)
    def _():
        x = jnp.where(pl.program_id(0) < n1, xp_ref[...], xs_ref[...])
        ms = jnp.mean(x * x, axis=-1, keepdims=True)
        h_ref[...] = (x * lax.rsqrt(ms + EPS) * g_ref[...]).astype(BF16)

    o_ref[...] = _dot(h_ref[...], w_ref[...]).astype(o_ref.dtype)


def _in_proj(xp, xs, g, w, tm, tn):
    n1, n2 = xp.shape[0] // tm, xs.shape[0] // tm
    T = xp.shape[0] + xs.shape[0]
    return pl.pallas_call(
        functools.partial(_in_proj_kernel, n1=n1),
        grid=(n1 + n2, P_COLS // tn),
        in_specs=_pair_specs((tm, D_MODEL), n1) + [
            pl.BlockSpec((1, D_MODEL), lambda i, j: (0, 0)),
            pl.BlockSpec((D_MODEL, tn), lambda i, j: (0, j))],
        out_specs=pl.BlockSpec((tm, tn), lambda i, j: (i, j)),
        out_shape=jax.ShapeDtypeStruct((T, P_COLS), BF16),
        scratch_shapes=[pltpu.VMEM((tm, D_MODEL), BF16)],
        compiler_params=_cparams(("parallel", "arbitrary")),
        name="in_proj",
    )(xp, xs, g, w)


def _gla_direction(q_ref, k_ref, v_ref, a_ref, aw_ref, ab_ref, st_ref, o_ref, a_off, backward, R):
    C = GLA_CHUNK
    nc = R // C
    row = lax.broadcasted_iota(jnp.int32, (R, R), 0)
    col = lax.broadcasted_iota(jnp.int32, (R, R), 1)
    same = (row // C) == (col // C)
    cum = same & ((col >= row) if backward else (col <= row))
    att_mask = same & ((col > row) if backward else (col <= row))
    tri = jnp.where(cum, 1.0, 0.0).astype(BF16)
    blk = jnp.where(same, 1.0, 0.0).astype(BF16)
    a = a_ref[:, a_off:a_off + GLA_RANK].astype(BF16)
    order = range(nc - 1, -1, -1) if backward else range(nc)
    for h in range(GLA_HEADS):
        ks = slice(h * GLA_DK, (h + 1) * GLA_DK)
        vs = slice(h * GLA_DV, (h + 1) * GLA_DV)
        z = _dot(a, aw_ref[:, ks].astype(BF16)) + ab_ref[:, ks]
        la = (jnp.minimum(z, 0.0) - jnp.log(1.0 + jnp.exp(-jnp.abs(z)))) / GLA_GATE_NORM
        la_hi = la.astype(BF16)
        la_lo = (la - la_hi.astype(F32)).astype(BF16)
        b = _dot(tri, la_hi) + _dot(tri, la_lo)
        tot = _dot(blk, la_hi) + _dot(blk, la_lo)
        qh = q_ref[:, ks].astype(F32) * (GLA_DK ** -0.5)
        kh = k_ref[:, ks].astype(F32)
        vh = v_ref[:, vs].astype(BF16)
        q_e = (qh * jnp.exp(b)).astype(BF16)
        k_e = (kh * jnp.exp(-b)).astype(BF16)
        k_s = (kh * jnp.exp(tot - b)).astype(BF16)
        dec = jnp.exp(tot)
        att = jnp.where(att_mask, _dot_nt(q_e, k_e), 0.0).astype(BF16)
        o_intra = _dot(att, vh)
        for c in order:
            rs = slice(c * C, (c + 1) * C)
            st = st_ref[h]
            o_ref[rs, vs] = o_intra[rs] + _dot_nt(q_e[rs], st.astype(BF16))
            st_ref[h] = dec[c * C:c * C + 1, :] * st + _dot_tn(vh[rs], k_s[rs])


def _gla_kernel(qf_ref, kf_ref, vf_ref, af_ref, qb_ref, kb_ref, vb_ref, ab_ref,
                awf_ref, abf_ref, awb_ref, abb_ref, of_ref, ob_ref, stf_ref, stb_ref, *, R):
    @pl.when(pl.program_id(1) == 0)
    def _():
        stf_ref[...] = jnp.zeros_like(stf_ref)
        stb_ref[...] = jnp.zeros_like(stb_ref)

    _gla_direction(qf_ref, kf_ref, vf_ref, af_ref, awf_ref, abf_ref, stf_ref, of_ref, 0, False, R)
    _gla_direction(qb_ref, kb_ref, vb_ref, ab_ref, awb_ref, abb_ref, stb_ref, ob_ref, GLA_RANK, True, R)


def _gla(P, awf, abf, awb, abb, Bt, S, R):
    T = Bt * S
    NB = S // R
    HK, HV = GLA_HEADS * GLA_DK, GLA_HEADS * GLA_DV
    fwd = lambda b, n: b * NB + n
    bwd = lambda b, n: b * NB + (NB - 1 - n)

    def specs(rowf):
        return [pl.BlockSpec((R, HK), lambda b, n: (rowf(b, n), COL_Q // HK)),
                pl.BlockSpec((R, HK), lambda b, n: (rowf(b, n), COL_K // HK)),
                pl.BlockSpec((R, HV), lambda b, n: (rowf(b, n), COL_V // HV)),
                pl.BlockSpec((R, 128), lambda b, n: (rowf(b, n), COL_ALR // 128))]

    wspec = lambda shape: pl.BlockSpec(shape, lambda b, n: (0, 0))
    return pl.pallas_call(
        functools.partial(_gla_kernel, R=R),
        grid=(Bt, NB),
        in_specs=specs(fwd) + specs(bwd) + [wspec((GLA_RANK, HK)), wspec((1, HK)),
                                            wspec((GLA_RANK, HK)), wspec((1, HK))],
        out_specs=[pl.BlockSpec((R, HV), lambda b, n: (fwd(b, n), 0)),
                   pl.BlockSpec((R, HV), lambda b, n: (bwd(b, n), 0))],
        out_shape=[jax.ShapeDtypeStruct((T, HV), F32), jax.ShapeDtypeStruct((T, HV), F32)],
        scratch_shapes=[pltpu.VMEM((GLA_HEADS, GLA_DV, GLA_DK), F32),
                        pltpu.VMEM((GLA_HEADS, GLA_DV, GLA_DK), F32)],
        compiler_params=_cparams(("parallel", "arbitrary")),
        name="gla",
    )(P, P, P, P, P, P, P, P, awf, abf, awb, abb)


def _mla_prep_kernel(cq_ref, ckv_ref, kr_ref, cos_ref, sin_ref, qn_ref, wuq_ref, kvn_ref, wukv_ref,
                     q_out, k_out, v_out):
    cq = cq_ref[...].astype(F32)
    ms = jnp.sum(cq * cq, axis=-1, keepdims=True) * (1.0 / MLA_Q_LORA)
    hq = (cq * lax.rsqrt(ms + EPS) * qn_ref[...]).astype(BF16)
    qq = _dot(hq, wuq_ref[...])
    ckv = ckv_ref[...].astype(F32)
    ms = jnp.mean(ckv * ckv, axis=-1, keepdims=True)
    hkv = (ckv * lax.rsqrt(ms + EPS) * kvn_ref[...]).astype(BF16)
    kv = _dot(hkv, wukv_ref[...])
    cos = cos_ref[...]
    sin = sin_ref[...]
    kr = kr_ref[...].astype(F32)
    k_pe = (kr[:, :MLA_ROPE] * cos + kr[:, MLA_ROPE:] * sin).astype(BF16)
    scale = MLA_DQK ** -0.5 * LOG2E
    zpad = jnp.zeros((cq.shape[0], MLA_QK_PAD - MLA_DQK), BF16)
    for h in range(MLA_HEADS):
        o = h * 256
        q_pe = qq[:, o + 128:o + 192] * cos + qq[:, o + 192:o + 256] * sin
        q_out[0, h, :, 0:MLA_NOPE] = (qq[:, o:o + MLA_NOPE] * scale).astype(BF16)
        q_out[0, h, :, MLA_NOPE:MLA_DQK] = (q_pe * scale).astype(BF16)
        q_out[0, h, :, MLA_DQK:MLA_QK_PAD] = zpad
        k_out[0, h, :, 0:MLA_NOPE] = kv[:, o:o + MLA_NOPE].astype(BF16)
        k_out[0, h, :, MLA_NOPE:MLA_DQK] = k_pe
        k_out[0, h, :, MLA_DQK:MLA_QK_PAD] = zpad
        v_out[0, h] = kv[:, o + MLA_NOPE:o + 256].astype(BF16)


def _mla_prep(P, cos2, sin2, qn, wuq, kvn, wukv, Bt, S, ts):
    NS = S // ts
    row = lambda b, i: b * NS + i
    wspec = lambda shape: pl.BlockSpec(shape, lambda b, i: (0, 0))
    H = MLA_HEADS
    return pl.pallas_call(
        _mla_prep_kernel,
        grid=(Bt, NS),
        in_specs=[pl.BlockSpec((ts, CQ_PAD), lambda b, i: (row(b, i), COL_CQ // CQ_PAD)),
                  pl.BlockSpec((ts, MLA_KV_LORA), lambda b, i: (row(b, i), COL_CKV // MLA_KV_LORA)),
                  pl.BlockSpec((ts, 128), lambda b, i: (row(b, i), COL_KR // 128)),
                  pl.BlockSpec((ts, MLA_ROPE), lambda b, i: (i, 0)),
                  pl.BlockSpec((ts, MLA_ROPE), lambda b, i: (i, 0)),
                  wspec((1, CQ_PAD)), wspec((CQ_PAD, H * 256)),
                  wspec((1, MLA_KV_LORA)), wspec((MLA_KV_LORA, H * 256))],
        out_specs=[pl.BlockSpec((1, H, ts, MLA_QK_PAD), lambda b, i: (b, 0, i, 0)),
                   pl.BlockSpec((1, H, ts, MLA_QK_PAD), lambda b, i: (b, 0, i, 0)),
                   pl.BlockSpec((1, H, ts, MLA_V), lambda b, i: (b, 0, i, 0))],
        out_shape=[jax.ShapeDtypeStruct((Bt, H, S, MLA_QK_PAD), BF16),
                   jax.ShapeDtypeStruct((Bt, H, S, MLA_QK_PAD), BF16),
                   jax.ShapeDtypeStruct((Bt, H, S, MLA_V), BF16)],
        compiler_params=_cparams(("parallel", "parallel")),
        name="mla_prep",
    )(P, P, P, cos2, sin2, qn, wuq, kvn, wukv)


ATTN_HEADS_PER_STEP = 2


def _attn_kernel(q_ref, k_ref, v_ref, o_ref, m_ref, l_ref, acc_ref):
    kv = pl.program_id(3)
    nl = k_ref.shape[2] // 128

    @pl.when(kv == 0)
    def _():
        m_ref[...] = jnp.full_like(m_ref, -jnp.inf)
        l_ref[...] = jnp.zeros_like(l_ref)
        acc_ref[...] = jnp.zeros_like(acc_ref)

    for h in range(ATTN_HEADS_PER_STEP):
        s = _dot_nt(q_ref[0, h], k_ref[0, h])
        m_prev = m_ref[h]
        m_new = jnp.maximum(m_prev, jnp.max(s, axis=-1, keepdims=True))
        alpha = jnp.exp2(m_prev - m_new)
        ps = [jnp.exp2(s[:, c * 128:(c + 1) * 128] - m_new) for c in range(nl)]
        psum = ps[0]
        for c in range(1, nl):
            psum = psum + ps[c]
        p = jnp.concatenate([x.astype(BF16) for x in ps], axis=-1)
        l_ref[h] = alpha * l_ref[h] + psum
        acc_ref[h] = alpha * acc_ref[h] + _dot(p, v_ref[0, h])
        m_ref[h] = m_new

    @pl.when(kv == pl.num_programs(3) - 1)
    def _():
        for h in range(ATTN_HEADS_PER_STEP):
            l = jnp.sum(l_ref[h], axis=-1, keepdims=True)
            o_ref[0, :, h * MLA_V:(h + 1) * MLA_V] = (acc_ref[h] / l).astype(o_ref.dtype)


def _attention(Q, K, V, tq, tk):
    Bt, H, S, _ = Q.shape
    G = ATTN_HEADS_PER_STEP
    return pl.pallas_call(
        _attn_kernel,
        grid=(Bt, H // G, S // tq, S // tk),
        in_specs=[pl.BlockSpec((1, G, tq, MLA_QK_PAD), lambda b, h, i, j: (b, h, i, 0)),
                  pl.BlockSpec((1, G, tk, MLA_QK_PAD), lambda b, h, i, j: (b, h, j, 0)),
                  pl.BlockSpec((1, G, tk, MLA_V), lambda b, h, i, j: (b, h, j, 0))],
        out_specs=pl.BlockSpec((1, tq, G * MLA_V), lambda b, h, i, j: (b, i, h)),
        out_shape=jax.ShapeDtypeStruct((Bt, S, H * MLA_V), BF16),
        scratch_shapes=[pltpu.VMEM((G, tq, 128), F32), pltpu.VMEM((G, tq, 128), F32),
                        pltpu.VMEM((G, tq, MLA_V), F32)],
        compiler_params=_cparams(("parallel", "parallel", "parallel", "arbitrary")),
        name="attn",
    )(Q, K, V)


def _merge_kernel(xp_ref, xs_ref, of_ref, ob_ref, r_ref, gate_ref, oat_ref, gn_ref, wgo_ref, wmo_ref, wout_ref,
                  n2_ref, wq_ref, x1_ref, h2t_ref, qp_ref, *, n1):
    o = of_ref[...] + ob_ref[...]
    gn = gn_ref[...]
    parts = []
    for h in range(GLA_HEADS):
        vs = slice(h * GLA_DV, (h + 1) * GLA_DV)
        oh = o[:, vs]
        ms = jnp.mean(oh * oh, axis=-1, keepdims=True)
        parts.append(oh * lax.rsqrt(ms + EPS) * gn[:, vs])
    on = jnp.concatenate(parts, axis=-1)
    r = r_ref[...].astype(F32)
    ya = _dot((on * (r * _sigmoid(r))).astype(BF16), wgo_ref[...])
    yb = _dot(oat_ref[...], wmo_ref[...])
    g = _sigmoid(gate_ref[...].astype(F32))
    mix = g[:, :D_MODEL] * ya + g[:, D_MODEL:] * yb
    x = jnp.where(pl.program_id(0) < n1, xp_ref[...], xs_ref[...])
    x1 = x + _dot(mix.astype(BF16), wout_ref[...])
    x1_ref[...] = x1
    ms = jnp.mean(x1 * x1, axis=-1, keepdims=True)
    h2 = x1 * lax.rsqrt(ms + EPS) * n2_ref[...]
    h2t_ref[...] = h2.T.astype(BF16)
    qp_ref[...] = _dot(h2.astype(BF16), wq_ref[...]).astype(BF16)


def _merge(xp, xs, o_f, o_b, P, o_att, gn, wgo, wmo, wout, n2, wq, tm):
    T = xp.shape[0] + xs.shape[0]
    n1 = xp.shape[0] // tm
    D = D_MODEL
    QW = PEER_HEADS * 2 * PEER_HALF
    tile = lambda w, c=0: pl.BlockSpec((tm, w), lambda i: (i, c))
    wspec = lambda shape: pl.BlockSpec(shape, lambda i: (0, 0))
    return pl.pallas_call(
        functools.partial(_merge_kernel, n1=n1),
        grid=(T // tm,),
        in_specs=_pair_specs((tm, D), n1) + [
            tile(D), tile(D), tile(D, COL_R // D), tile(2 * D, COL_GATE // (2 * D)), tile(D),
            wspec((1, D)), wspec((D, D)), wspec((D, D)), wspec((D, D)), wspec((1, D)), wspec((D, QW))],
        out_specs=[tile(D), pl.BlockSpec((D, tm), lambda i: (0, i)), tile(QW)],
        out_shape=[jax.ShapeDtypeStruct((T, D), F32), jax.ShapeDtypeStruct((D, T), BF16),
                   jax.ShapeDtypeStruct((T, QW), BF16)],
        compiler_params=_cparams(("parallel",)),
        name="merge",
    )(xp, xs, o_f, o_b, P, P, o_att, gn, wgo, wmo, wout, n2, wq)


def _extract16(cur, tie_index):
    n, t = cur.shape
    slot = lax.broadcasted_iota(jnp.int32, (PEER_TOPK, t), 0)
    vals = jnp.zeros((PEER_TOPK, t), F32)
    rank = jnp.full((n, t), float(PEER_TOPK), F32)
    for a in range(PEER_TOPK):
        m = jnp.max(cur, axis=0, keepdims=True)
        sel = cur == m
        if tie_index is not None:
            first = jnp.min(jnp.where(sel, tie_index, float(n * n)), axis=0, keepdims=True)
            sel = tie_index == first
        vals = jnp.where(slot == a, m, vals)
        rank = jnp.where(sel, float(a), rank)
        cur = jnp.where(sel, -jnp.inf, cur)
    return vals, rank


def _taken_error(rank):
    taken = jnp.sum(jnp.where(rank < float(PEER_TOPK), 1.0, 0.0), axis=0, keepdims=True)
    return jnp.abs(taken - float(PEER_TOPK))


_CAND_AB = [
    [(0, b) for b in range(8)],
    [(0, b) for b in range(8, 16)],
    [(1, b) for b in range(8)],
    [(2, b) for b in range(5)] + [(4, b) for b in range(3)],
    [(3, b) for b in range(4)] + [(5, 0), (5, 1), (6, 0), (6, 1)],
    [(7, 0), (7, 1)] + [(a, 0) for a in range(8, 14)],
    [(14, 0), (15, 0)] + [None] * 6,
]
assert sorted(p for t in _CAND_AB for p in t if p) == sorted(
    (a, b) for a in range(PEER_TOPK) for b in range(PEER_TOPK) if (a + 1) * (b + 1) <= PEER_TOPK)


def _candidate_tiles(t1, t2):
    tt = t1.shape[1]
    row = lax.broadcasted_iota(jnp.int32, (8, tt), 0)
    lo1, hi1, lo2, hi2 = t1[0:8], t1[8:16], t2[0:8], t2[8:16]
    r1 = lambda a: t1[a:a + 1, :]
    up = lambda x, k: pltpu.roll(x, k, 0)
    hi1r = up(hi1, 2)
    b0 = t2[0:1, :]
    tiles = [
        r1(0) + lo2,
        r1(0) + hi2,
        r1(1) + lo2,
        jnp.where(row < 5, r1(2) + lo2, r1(4) + up(lo2, 5)),
        jnp.where(row < 4, r1(3) + lo2, jnp.where(row < 6, r1(5) + up(lo2, 4), r1(6) + up(lo2, 6))),
        jnp.where(row < 2, r1(7) + lo2, hi1r + b0),
        jnp.where(row < 2, hi1r + b0, -jnp.inf),
    ]
    return jnp.concatenate(tiles, axis=0)


def _row_constants(values, tt):
    n = len(values)
    row = lax.broadcasted_iota(jnp.int32, (n, tt), 0)
    out = jnp.full((n, tt), float(values[-1]), F32)
    for r in range(n - 1):
        out = jnp.where(row == r, float(values[r]), out)
    return out


def _peer_topk_kernel(qp_ref, k1_ref, k2_ref, r1_ref, f1_ref, k2r_ref, f2_ref, s_sc, t_sc, rk_sc, cnt_sc, z_sc):
    K, H = PEER_TOPK, PEER_HEADS
    tt = qp_ref.shape[0]
    k1 = k1_ref[...]
    k2 = k2_ref[...]
    for h in range(H):
        o = h * 2 * PEER_HALF
        s_sc[2 * h] = _dot_nt(k1, qp_ref[:, o:o + PEER_HALF])
        s_sc[2 * h + 1] = _dot_nt(k2, qp_ref[:, o + PEER_HALF:o + 2 * PEER_HALF])

    def stage1(exact):
        err = jnp.zeros((1, tt), F32)
        idx = lax.broadcasted_iota(jnp.int32, (PEER_NKEYS, tt), 0).astype(F32) if exact else None
        for i in range(2 * H):
            vals, rank = _extract16(s_sc[i], idx)
            t_sc[i] = vals
            rk_sc[i] = rank
            if not exact:
                err = jnp.maximum(err, _taken_error(rank))
        return err

    err1 = stage1(False)

    @pl.when(jnp.max(err1) > 0.0)
    def _():
        stage1(True)

    pairs = [p for tile in _CAND_AB for p in tile]
    slot = lax.broadcasted_iota(jnp.int32, (K, tt), 0)

    def stage2(exact):
        err = jnp.zeros((1, tt), F32)
        flat = _row_constants([K * K if p is None else p[0] * K + p[1] for p in pairs], tt) if exact else None
        for h in range(H):
            cand = _candidate_tiles(t_sc[2 * h], t_sc[2 * h + 1])
            _, rank = _extract16(cand, flat)
            chosen = rank < float(K)
            ch = jnp.where(chosen, 1.0, 0.0)
            z = jnp.sum(jnp.where(chosen, jnp.exp(cand - cand[0:1, :]), 0.0), axis=0, keepdims=True)
            z_sc[h] = jnp.broadcast_to(z, (8, tt))
            cnt = jnp.zeros((K, tt), F32)
            for a in range(K):
                rows = [r for r, p in enumerate(pairs) if p is not None and p[0] == a]
                ca = jnp.sum(ch[rows[0]:rows[-1] + 1, :], axis=0, keepdims=True)
                cnt = jnp.where(slot == a, ca, cnt)
            cnt_sc[h] = cnt
            if not exact:
                err = jnp.maximum(err, _taken_error(rank))
        return err

    err2 = stage2(False)

    @pl.when(jnp.max(err2) > 0.0)
    def _():
        stage2(True)

    for h in range(H):
        s1, s2 = s_sc[2 * h], s_sc[2 * h + 1]
        t1, t2 = t_sc[2 * h], t_sc[2 * h + 1]
        rank1 = rk_sc[2 * h]
        cnt = cnt_sc[h]
        r1 = jnp.zeros(rank1.shape, F32)
        for a in range(K):
            r1 = jnp.where(rank1 == float(a), cnt[a:a + 1, :], r1)
        r1_ref[h] = r1
        f1_ref[h] = jnp.exp(s1 - t1[0:1, :]) * (0.5 / z_sc[h, 0:1, :])
        k2r_ref[h] = rk_sc[2 * h + 1].astype(BF16)
        f2_ref[h] = jnp.exp(s2 - t2[0:1, :]).astype(BF16)


def _peer_topk(qp, k1, k2, tt):
    T = qp.shape[0]
    H, N = PEER_HEADS, PEER_NKEYS
    out = jax.ShapeDtypeStruct((H, N, T), F32)
    outb = jax.ShapeDtypeStruct((H, N, T), BF16)
    ospec = pl.BlockSpec((H, N, tt), lambda i: (0, 0, i))
    return pl.pallas_call(
        _peer_topk_kernel,
        grid=(T // tt,),
        in_specs=[pl.BlockSpec((tt, H * 2 * PEER_HALF), lambda i: (i, 0)),
                  pl.BlockSpec((N, PEER_HALF), lambda i: (0, 0)),
                  pl.BlockSpec((N, PEER_HALF), lambda i: (0, 0))],
        out_specs=[ospec, ospec, ospec, ospec],
        out_shape=[out, out, outb, outb],
        scratch_shapes=[pltpu.VMEM((2 * H, N, tt), F32), pltpu.VMEM((2 * H, PEER_TOPK, tt), F32),
                        pltpu.VMEM((2 * H, N, tt), F32), pltpu.VMEM((H, PEER_TOPK, tt), F32),
                        pltpu.VMEM((H, 8, tt), F32)],
        compiler_params=_cparams(("parallel",)),
        name="peer_topk",
    )(qp, k1, k2)


def _peer_main_kernel(h2t_ref, u_ref, vt_ref, r1_ref, f1_ref, k2r_ref, f2_ref, o_ref,
                      acc_ref, s_ref, g_ref, *, ec):
    j = pl.program_id(1)
    N = PEER_NKEYS
    tt = s_ref.shape[2]
    lw = 256

    @pl.when(j == 0)
    def _():
        acc_ref[...] = jnp.zeros_like(acc_ref)
        s_ref[1] = jnp.zeros(s_ref.shape[1:], s_ref.dtype)

    def step(rd, wr):
        th = tt // 2
        for half in range(2):
            hs = slice(half * th, (half + 1) * th)
            s_ref[wr, :, hs] = _dot(pltpu.bitcast(u_ref[...], BF16), h2t_ref[:, hs])
            for e in range(ec):
                rows = slice(e * N, (e + 1) * N)
                for c in range(th // lw):
                    cs = slice(half * th + c * lw, half * th + (c + 1) * lw)
                    w = jnp.zeros((N // 16, 16, lw), BF16)
                    for h in range(PEER_HEADS):
                        r1 = jnp.broadcast_to(r1_ref[h, e:e + 1, cs], (16, lw)).astype(BF16)[None]
                        f1 = jnp.broadcast_to(f1_ref[h, e:e + 1, cs], (16, lw)).astype(BF16)[None]
                        k2 = k2r_ref[h, :, cs].reshape(N // 16, 16, lw)
                        f2 = f2_ref[h, :, cs].reshape(N // 16, 16, lw)
                        w = w + jnp.where(k2 < r1, f2, jnp.zeros((), BF16)) * f1
                    se = s_ref[rd, rows, cs]
                    act = se * (1.0 + lax.erf(se * (2.0 ** -0.5)))
                    g_ref[rows, cs] = act.astype(BF16) * w.reshape(N, lw)
            acc_ref[:, hs] += _dot(pltpu.bitcast(vt_ref[...], BF16), g_ref[:, hs])

    @pl.when(j % 2 == 0)
    def _():
        step(1, 0)

    @pl.when(j % 2 == 1)
    def _():
        step(0, 1)

    @pl.when(j == pl.num_programs(1) - 1)
    def _():
        o_ref[...] = acc_ref[...].T


def _peer_main(h2t, u, vt, r1, f1, k2r, f2, tt, ec):
    T = h2t.shape[1]
    D, H, N = D_MODEL, PEER_HEADS, PEER_NKEYS
    nc = N // ec
    lag = lambda j, k: jnp.clip(j - k, 0, nc - 1)
    return pl.pallas_call(
        functools.partial(_peer_main_kernel, ec=ec),
        grid=(T // tt, nc + 1),
        in_specs=[pl.BlockSpec((D, tt), lambda i, j: (0, i)),
                  pl.BlockSpec((ec * N // 2, D), lambda i, j: (lag(j, 0), 0)),
                  pl.BlockSpec((D // 2, ec * N), lambda i, j: (0, lag(j, 1))),
                  pl.BlockSpec((H, ec, tt), lambda i, j: (0, lag(j, 1), i)),
                  pl.BlockSpec((H, ec, tt), lambda i, j: (0, lag(j, 1), i)),
                  pl.BlockSpec((H, N, tt), lambda i, j: (0, 0, i)),
                  pl.BlockSpec((H, N, tt), lambda i, j: (0, 0, i))],
        out_specs=pl.BlockSpec((tt, D), lambda i, j: (i, 0)),
        out_shape=jax.ShapeDtypeStruct((T, D), F32),
        scratch_shapes=[pltpu.VMEM((D, tt), F32), pltpu.VMEM((2, ec * N, tt), F32),
                        pltpu.VMEM((ec * N, tt), BF16)],
        compiler_params=_cparams(("parallel", "arbitrary")),
        name="peer_main",
    )(h2t, u, vt, r1, f1, k2r, f2)


def _final_kernel(x_ref, peer_ref, pp_ref, ps_ref, wg_ref, wp_ref, fn_ref, yp_ref, ys_ref, *, n1):
    i = pl.program_id(0)
    x = x_ref[...] + peer_ref[...]
    gate = _sigmoid(_dot(x.astype(BF16), wg_ref[...]))
    p = jnp.where(i < n1, pp_ref[...], ps_ref[...])
    x = x + gate * _dot(p.astype(BF16), wp_ref[...])
    ms = jnp.mean(x * x, axis=-1, keepdims=True)
    y = x * lax.rsqrt(ms + EPS) * fn_ref[...]

    @pl.when(i < n1)
    def _():
        yp_ref[...] = y

    @pl.when(i >= n1)
    def _():
        ys_ref[...] = y


def _final(x, peer, pp, ps, wg, wp, fn, tm):
    T = x.shape[0]
    n1 = pp.shape[0] // tm
    D = D_MODEL
    return pl.pallas_call(
        functools.partial(_final_kernel, n1=n1),
        grid=(T // tm,),
        in_specs=[pl.BlockSpec((tm, D), lambda i: (i, 0)),
                  pl.BlockSpec((tm, D), lambda i: (i, 0))] + _pair_specs((tm, PLE_DIM), n1) + [
                  pl.BlockSpec((D, D), lambda i: (0, 0)),
                  pl.BlockSpec((PLE_DIM, D), lambda i: (0, 0)),
                  pl.BlockSpec((1, D), lambda i: (0, 0))],
        out_specs=_pair_specs((tm, D), n1),
        out_shape=[jax.ShapeDtypeStruct((pp.shape[0], D), F32), jax.ShapeDtypeStruct((ps.shape[0], D), F32)],
        compiler_params=_cparams(("arbitrary",)),
        name="final",
    )(x, peer, pp, ps, wg, wp, fn)


def _rot_cols(w):
    half = MLA_ROPE // 2
    return jnp.concatenate([-w[..., half:], w[..., :half]], axis=-1)


def _pack_w_in(w_in):
    pts, acc = [], 0
    for sz in (512, 512, 1024, 1024, 2 * GLA_RANK, MLA_Q_LORA, MLA_KV_LORA, MLA_ROPE):
        acc += sz
        pts.append(acc)
    q, k, v, r, a_lr, c_q, c_kv, k_r, gate = jnp.split(w_in, pts, axis=-1)
    z = lambda n: jnp.zeros((D_MODEL, n), w_in.dtype)
    packed = jnp.concatenate([gate, v, r, q, k, c_q, z(CQ_PAD - MLA_Q_LORA), c_kv, k_r, _rot_cols(k_r),
                              a_lr, z(128 - 2 * GLA_RANK)], axis=-1)
    assert packed.shape[-1] == P_COLS
    return packed.astype(BF16)


def _pack_w_uq(w_uq):
    w = w_uq.reshape(MLA_Q_LORA, MLA_HEADS, MLA_DQK)
    pe = w[..., MLA_NOPE:]
    w = jnp.concatenate([w, _rot_cols(pe)], axis=-1).reshape(MLA_Q_LORA, MLA_HEADS * 256)
    return jnp.concatenate([w, jnp.zeros((CQ_PAD - MLA_Q_LORA, MLA_HEADS * 256), w.dtype)], axis=0).astype(BF16)


def _pack_row_pairs(w):
    u = lax.bitcast_convert_type(w.astype(BF16), jnp.uint16).astype(jnp.uint32)
    return (u[1::2] << 16) | u[0::2]


def _tile(n, pref):
    return pref if n % pref == 0 else n


def kernel(x_prompt, x_sample, p_prompt, p_sample, norm1, w_in, gla_a_w_f, gla_a_b_f, gla_a_w_b, gla_a_b_b, gla_norm, gla_w_o, mla_q_norm, mla_w_uq, mla_kv_norm, mla_w_ukv, mla_w_o, w_out, norm2, peer_w_q, peer_k1, peer_k2, peer_u, peer_v, ple_proj, ple_gate, final_norm):
    B1, S, D = x_prompt.shape
    B2 = x_sample.shape[0]
    assert x_sample.shape[1] == S and norm1.shape[0] == 1, "one layer, equal sequence lengths"
    Bt = B1 + B2
    T = Bt * S
    T1, T2 = B1 * S, B2 * S
    xp, xs = x_prompt.reshape(T1, D), x_sample.reshape(T2, D)
    row = lambda v: v.reshape(1, -1).astype(F32)

    P = _in_proj(xp, xs, row(norm1[0]), _pack_w_in(w_in[0]), _pair_tile(T1, T2, 1024), 1024)

    o_f, o_b = _gla(P, gla_a_w_f[0], row(gla_a_b_f[0]), gla_a_w_b[0], row(gla_a_b_b[0]), Bt, S, _tile(S, 256))

    pos = jnp.arange(S, dtype=F32)
    inv = ROPE_THETA ** (-jnp.arange(MLA_ROPE // 2, dtype=F32) * 2.0 / MLA_ROPE)
    ang = pos[:, None] * inv[None, :]
    cos2 = jnp.concatenate([jnp.cos(ang), jnp.cos(ang)], axis=-1)
    sin2 = jnp.concatenate([jnp.sin(ang), jnp.sin(ang)], axis=-1)
    qn = jnp.concatenate([mla_q_norm[0], jnp.zeros((CQ_PAD - MLA_Q_LORA,), F32)]).reshape(1, CQ_PAD)
    Q, K, V = _mla_prep(P, cos2, sin2, qn, _pack_w_uq(mla_w_uq[0]), row(mla_kv_norm[0]),
                        mla_w_ukv[0].astype(BF16), Bt, S, _tile(S, 512))
    o_att = _attention(Q, K, V, _tile(S, 1024), _tile(S, 1024)).reshape(T, MLA_HEADS * MLA_V)

    gn = jnp.tile(gla_norm[0], GLA_HEADS).reshape(1, D)
    x1, h2t, qp = _merge(xp, xs, o_f, o_b, P, o_att, gn, gla_w_o[0].astype(BF16), mla_w_o[0].astype(BF16),
                        w_out[0].astype(BF16), row(norm2[0]), peer_w_q[0].astype(BF16), _pair_tile(T1, T2, 256))

    r1, f1, k2r, f2 = _peer_topk(qp, peer_k1[0].astype(BF16), peer_k2[0].astype(BF16), _tile(T, 256))
    peer = _peer_main(h2t, _pack_row_pairs(peer_u[0]), _pack_row_pairs(peer_v[0].T), r1, f1, k2r, f2,
                      _tile(T, 1024), 8)

    yp, ys = _final(x1, peer, p_prompt[0].reshape(T1, PLE_DIM), p_sample[0].reshape(T2, PLE_DIM),
                    ple_gate[0].astype(BF16), ple_proj[0].astype(BF16), row(final_norm), _pair_tile(T1, T2, 512))
    return (yp.reshape(B1, S, D), ys.reshape(B2, S, D))
```

```python
import functools

import jax
import jax.numpy as jnp
from jax import lax
from jax.experimental import pallas as pl
from jax.experimental.pallas import tpu as pltpu

F32 = jnp.float32
BF16 = jnp.bfloat16
EPS = 1e-6

D_MODEL = 1024
PLE_DIM = 256
GLA_HEADS, GLA_DK, GLA_DV, GLA_RANK, GLA_GATE_NORM, GLA_CHUNK = 4, 128, 256, 16, 16.0, 64
MLA_HEADS, MLA_Q_LORA, MLA_KV_LORA, MLA_NOPE, MLA_ROPE, MLA_V = 8, 384, 256, 128, 64, 128
MLA_DQK = MLA_NOPE + MLA_ROPE
MLA_QK_PAD = 256
ROPE_THETA = 10000.0
LOG2E = 1.4426950408889634
PEER_HEADS, PEER_NKEYS, PEER_HALF, PEER_TOPK = 8, 128, 128, 16
PEER_N = PEER_NKEYS * PEER_NKEYS

P_COLS = 6144
COL_GATE, COL_V, COL_R, COL_Q, COL_K, COL_CQ, COL_CKV, COL_KR, COL_ALR = (
    0, 2048, 3072, 4096, 4608, 5120, 5632, 5888, 6016)
CQ_PAD = 512

VMEM_LIMIT = 56 * 1024 * 1024


def _cparams(sem):
    return pltpu.CompilerParams(dimension_semantics=sem, vmem_limit_bytes=VMEM_LIMIT)


def _dot(a, b):
    return jnp.dot(a, b, preferred_element_type=F32)


def _dot_nt(a, b):
    return lax.dot_general(a, b, (((1,), (1,)), ((), ())), preferred_element_type=F32)


def _dot_tn(a, b):
    return lax.dot_general(a, b, (((0,), (0,)), ((), ())), preferred_element_type=F32)


def _sigmoid(x):
    return 1.0 / (1.0 + jnp.exp(-x))


def _pair_specs(block, n1):
    return [pl.BlockSpec(block, lambda i, *_: (jnp.minimum(i, n1 - 1), 0)),
            pl.BlockSpec(block, lambda i, *_: (jnp.maximum(i - n1, 0), 0))]


def _pair_tile(T1, T2, pref):
    t = pref
    while T1 % t or T2 % t:
        t //= 2
    return t


def _in_proj_kernel(xp_ref, xs_ref, g_ref, w_ref, o_ref, h_ref, *, n1):
    @pl.when(pl.program_id(1) == 0)
    def _():
        x = jnp.where(pl.program_id(0) < n1, xp_ref[...], xs_ref[...])
        ms = jnp.mean(x * x, axis=-1, keepdims=True)
        h_ref[...] = (x * lax.rsqrt(ms + EPS) * g_ref[...]).astype(BF16)

    o_ref[...] = _dot(h_ref[...], w_ref[...]).astype(o_ref.dtype)


def _in_proj(xp, xs, g, w, tm, tn):
    n1, n2 = xp.shape[0] // tm, xs.shape[0] // tm
    T = xp.shape[0] + xs.shape[0]
    return pl.pallas_call(
        functools.partial(_in_proj_kernel, n1=n1),
        grid=(n1 + n2, P_COLS // tn),
        in_specs=_pair_specs((tm, D_MODEL), n1) + [
            pl.BlockSpec((1, D_MODEL), lambda i, j: (0, 0)),
            pl.BlockSpec((D_MODEL, tn), lambda i, j: (0, j))],
        out_specs=pl.BlockSpec((tm, tn), lambda i, j: (i, j)),
        out_shape=jax.ShapeDtypeStruct((T, P_COLS), BF16),
        scratch_shapes=[pltpu.VMEM((tm, D_MODEL), BF16)],
        compiler_params=_cparams(("parallel", "arbitrary")),
        name="in_proj",
    )(xp, xs, g, w)


def _gla_direction(q_ref, k_ref, v_ref, a_ref, aw_ref, ab_ref, st_ref, o_ref, a_off, backward, R):
    C = GLA_CHUNK
    nc = R // C
    row = lax.broadcasted_iota(jnp.int32, (R, R), 0)
    col = lax.broadcasted_iota(jnp.int32, (R, R), 1)
    same = (row // C) == (col // C)
    cum = same & ((col >= row) if backward else (col <= row))
    att_mask = same & ((col > row) if backward else (col <= row))
    tri = jnp.where(cum, 1.0, 0.0).astype(BF16)
    blk = jnp.where(same, 1.0, 0.0).astype(BF16)
    a = a_ref[:, a_off:a_off + GLA_RANK].astype(BF16)
    order = range(nc - 1, -1, -1) if backward else range(nc)
    for h in range(GLA_HEADS):
        ks = slice(h * GLA_DK, (h + 1) * GLA_DK)
        vs = slice(h * GLA_DV, (h + 1) * GLA_DV)
        z = _dot(a, aw_ref[:, ks].astype(BF16)) + ab_ref[:, ks]
        la = (jnp.minimum(z, 0.0) - jnp.log(1.0 + jnp.exp(-jnp.abs(z)))) / GLA_GATE_NORM
        la_hi = la.astype(BF16)
        la_lo = (la - la_hi.astype(F32)).astype(BF16)
        b = _dot(tri, la_hi) + _dot(tri, la_lo)
        tot = _dot(blk, la_hi) + _dot(blk, la_lo)
        qh = q_ref[:, ks].astype(F32) * (GLA_DK ** -0.5)
        kh = k_ref[:, ks].astype(F32)
        vh = v_ref[:, vs].astype(BF16)
        q_e = (qh * jnp.exp(b)).astype(BF16)
        k_e = (kh * jnp.exp(-b)).astype(BF16)
        k_s = (kh * jnp.exp(tot - b)).astype(BF16)
        dec = jnp.exp(tot)
        att = jnp.where(att_mask, _dot_nt(q_e, k_e), 0.0).astype(BF16)
        o_intra = _dot(att, vh)
        for c in order:
            rs = slice(c * C, (c + 1) * C)
            st = st_ref[h]
            o_ref[rs, vs] = o_intra[rs] + _dot_nt(q_e[rs], st.astype(BF16))
            st_ref[h] = dec[c * C:c * C + 1, :] * st + _dot_tn(vh[rs], k_s[rs])


def _gla_kernel(qf_ref, kf_ref, vf_ref, af_ref, qb_ref, kb_ref, vb_ref, ab_ref,
                awf_ref, abf_ref, awb_ref, abb_ref, of_ref, ob_ref, stf_ref, stb_ref, *, R):
    @pl.when(pl.program_id(1) == 0)
    def _():
        stf_ref[...] = jnp.zeros_like(stf_ref)
        stb_ref[...] = jnp.zeros_like(stb_ref)

    _gla_direction(qf_ref, kf_ref, vf_ref, af_ref, awf_ref, abf_ref, stf_ref, of_ref, 0, False, R)
    _gla_direction(qb_ref, kb_ref, vb_ref, ab_ref, awb_ref, abb_ref, stb_ref, ob_ref, GLA_RANK, True, R)


def _gla(P, awf, abf, awb, abb, Bt, S, R):
    T = Bt * S
    NB = S // R
    HK, HV = GLA_HEADS * GLA_DK, GLA_HEADS * GLA_DV
    fwd = lambda b, n: b * NB + n
    bwd = lambda b, n: b * NB + (NB - 1 - n)

    def specs(rowf):
        return [pl.BlockSpec((R, HK), lambda b, n: (rowf(b, n), COL_Q // HK)),
                pl.BlockSpec((R, HK), lambda b, n: (rowf(b, n), COL_K // HK)),
                pl.BlockSpec((R, HV), lambda b, n: (rowf(b, n), COL_V // HV)),
                pl.BlockSpec((R, 128), lambda b, n: (rowf(b, n), COL_ALR // 128))]

    wspec = lambda shape: pl.BlockSpec(shape, lambda b, n: (0, 0))
    return pl.pallas_call(
        functools.partial(_gla_kernel, R=R),
        grid=(Bt, NB),
        in_specs=specs(fwd) + specs(bwd) + [wspec((GLA_RANK, HK)), wspec((1, HK)),
                                            wspec((GLA_RANK, HK)), wspec((1, HK))],
        out_specs=[pl.BlockSpec((R, HV), lambda b, n: (fwd(b, n), 0)),
                   pl.BlockSpec((R, HV), lambda b, n: (bwd(b, n), 0))],
        out_shape=[jax.ShapeDtypeStruct((T, HV), F32), jax.ShapeDtypeStruct((T, HV), F32)],
        scratch_shapes=[pltpu.VMEM((GLA_HEADS, GLA_DV, GLA_DK), F32),
                        pltpu.VMEM((GLA_HEADS, GLA_DV, GLA_DK), F32)],
        compiler_params=_cparams(("parallel", "arbitrary")),
        name="gla",
    )(P, P, P, P, P, P, P, P, awf, abf, awb, abb)


def _mla_prep_kernel(cq_ref, ckv_ref, kr_ref, cos_ref, sin_ref, qn_ref, wuq_ref, kvn_ref, wukv_ref,
                     q_out, k_out, v_out):
    cq = cq_ref[...].astype(F32)
    ms = jnp.sum(cq * cq, axis=-1, keepdims=True) * (1.0 / MLA_Q_LORA)
    hq = (cq * lax.rsqrt(ms + EPS) * qn_ref[...]).astype(BF16)
    qq = _dot(hq, wuq_ref[...])
    ckv = ckv_ref[...].astype(F32)
    ms = jnp.mean(ckv * ckv, axis=-1, keepdims=True)
    hkv = (ckv * lax.rsqrt(ms + EPS) * kvn_ref[...]).astype(BF16)
    kv = _dot(hkv, wukv_ref[...])
    cos = cos_ref[...]
    sin = sin_ref[...]
    kr = kr_ref[...].astype(F32)
    k_pe = (kr[:, :MLA_ROPE] * cos + kr[:, MLA_ROPE:] * sin).astype(BF16)
    scale = MLA_DQK ** -0.5 * LOG2E
    zpad = jnp.zeros((cq.shape[0], MLA_QK_PAD - MLA_DQK), BF16)
    for h in range(MLA_HEADS):
        o = h * 256
        q_pe = qq[:, o + 128:o + 192] * cos + qq[:, o + 192:o + 256] * sin
        q_out[0, h, :, 0:MLA_NOPE] = (qq[:, o:o + MLA_NOPE] * scale).astype(BF16)
        q_out[0, h, :, MLA_NOPE:MLA_DQK] = (q_pe * scale).astype(BF16)
        q_out[0, h, :, MLA_DQK:MLA_QK_PAD] = zpad
        k_out[0, h, :, 0:MLA_NOPE] = kv[:, o:o + MLA_NOPE].astype(BF16)
        k_out[0, h, :, MLA_NOPE:MLA_DQK] = k_pe
        k_out[0, h, :, MLA_DQK:MLA_QK_PAD] = zpad
        v_out[0, h] = kv[:, o + MLA_NOPE:o + 256].astype(BF16)


def _mla_prep(P, cos2, sin2, qn, wuq, kvn, wukv, Bt, S, ts):
    NS = S // ts
    row = lambda b, i: b * NS + i
    wspec = lambda shape: pl.BlockSpec(shape, lambda b, i: (0, 0))
    H = MLA_HEADS
    return pl.pallas_call(
        _mla_prep_kernel,
        grid=(Bt, NS),
        in_specs=[pl.BlockSpec((ts, CQ_PAD), lambda b, i: (row(b, i), COL_CQ // CQ_PAD)),
                  pl.BlockSpec((ts, MLA_KV_LORA), lambda b, i: (row(b, i), COL_CKV // MLA_KV_LORA)),
                  pl.BlockSpec((ts, 128), lambda b, i: (row(b, i), COL_KR // 128)),
                  pl.BlockSpec((ts, MLA_ROPE), lambda b, i: (i, 0)),
                  pl.BlockSpec((ts, MLA_ROPE), lambda b, i: (i, 0)),
                  wspec((1, CQ_PAD)), wspec((CQ_PAD, H * 256)),
                  wspec((1, MLA_KV_LORA)), wspec((MLA_KV_LORA, H * 256))],
        out_specs=[pl.BlockSpec((1, H, ts, MLA_QK_PAD), lambda b, i: (b, 0, i, 0)),
                   pl.BlockSpec((1, H, ts, MLA_QK_PAD), lambda b, i: (b, 0, i, 0)),
                   pl.BlockSpec((1, H, ts, MLA_V), lambda b, i: (b, 0, i, 0))],
        out_shape=[jax.ShapeDtypeStruct((Bt, H, S, MLA_QK_PAD), BF16),
                   jax.ShapeDtypeStruct((Bt, H, S, MLA_QK_PAD), BF16),
                   jax.ShapeDtypeStruct((Bt, H, S, MLA_V), BF16)],
        compiler_params=_cparams(("parallel", "parallel")),
        name="mla_prep",
    )(P, P, P, cos2, sin2, qn, wuq, kvn, wukv)


ATTN_HEADS_PER_STEP = 2


def _attn_kernel(q_ref, k_ref, v_ref, o_ref, m_ref, l_ref, acc_ref):
    kv = pl.program_id(3)
    nl = k_ref.shape[2] // 128

    @pl.when(kv == 0)
    def _():
        m_ref[...] = jnp.full_like(m_ref, -jnp.inf)
        l_ref[...] = jnp.zeros_like(l_ref)
        acc_ref[...] = jnp.zeros_like(acc_ref)

    for h in range(ATTN_HEADS_PER_STEP):
        s = _dot_nt(q_ref[0, h], k_ref[0, h])
        m_prev = m_ref[h]
        m_new = jnp.maximum(m_prev, jnp.max(s, axis=-1, keepdims=True))
        alpha = jnp.exp2(m_prev - m_new)
        ps = [jnp.exp2(s[:, c * 128:(c + 1) * 128] - m_new) for c in range(nl)]
        psum = ps[0]
        for c in range(1, nl):
            psum = psum + ps[c]
        p = jnp.concatenate([x.astype(BF16) for x in ps], axis=-1)
        l_ref[h] = alpha * l_ref[h] + psum
        acc_ref[h] = alpha * acc_ref[h] + _dot(p, v_ref[0, h])
        m_ref[h] = m_new

    @pl.when(kv == pl.num_programs(3) - 1)
    def _():
        for h in range(ATTN_HEADS_PER_STEP):
            l = jnp.sum(l_ref[h], axis=-1, keepdims=True)
            o_ref[0, :, h * MLA_V:(h + 1) * MLA_V] = (acc_ref[h] / l).astype(o_ref.dtype)


def _attention(Q, K, V, tq, tk):
    Bt, H, S, _ = Q.shape
    G = ATTN_HEADS_PER_STEP
    return pl.pallas_call(
        _attn_kernel,
        grid=(Bt, H // G, S // tq, S // tk),
        in_specs=[pl.BlockSpec((1, G, tq, MLA_QK_PAD), lambda b, h, i, j: (b, h, i, 0)),
                  pl.BlockSpec((1, G, tk, MLA_QK_PAD), lambda b, h, i, j: (b, h, j, 0)),
                  pl.BlockSpec((1, G, tk, MLA_V), lambda b, h, i, j: (b, h, j, 0))],
        out_specs=pl.BlockSpec((1, tq, G * MLA_V), lambda b, h, i, j: (b, i, h)),
        out_shape=jax.ShapeDtypeStruct((Bt, S, H * MLA_V), BF16),
        scratch_shapes=[pltpu.VMEM((G, tq, 128), F32), pltpu.VMEM((G, tq, 128), F32),
                        pltpu.VMEM((G, tq, MLA_V), F32)],
        compiler_params=_cparams(("parallel", "parallel", "parallel", "arbitrary")),
        name="attn",
    )(Q, K, V)


def _merge_kernel(xp_ref, xs_ref, of_ref, ob_ref, r_ref, gate_ref, oat_ref, gn_ref, wgo_ref, wmo_ref, wout_ref,
                  n2_ref, wq_ref, x1_ref, h2t_ref, qp_ref, *, n1):
    o = of_ref[...] + ob_ref[...]
    gn = gn_ref[...]
    parts = []
    for h in range(GLA_HEADS):
        vs = slice(h * GLA_DV, (h + 1) * GLA_DV)
        oh = o[:, vs]
        ms = jnp.mean(oh * oh, axis=-1, keepdims=True)
        parts.append(oh * lax.rsqrt(ms + EPS) * gn[:, vs])
    on = jnp.concatenate(parts, axis=-1)
    r = r_ref[...].astype(F32)
    ya = _dot((on * (r * _sigmoid(r))).astype(BF16), wgo_ref[...])
    yb = _dot(oat_ref[...], wmo_ref[...])
    g = _sigmoid(gate_ref[...].astype(F32))
    mix = g[:, :D_MODEL] * ya + g[:, D_MODEL:] * yb
    x = jnp.where(pl.program_id(0) < n1, xp_ref[...], xs_ref[...])
    x1 = x + _dot(mix.astype(BF16), wout_ref[...])
    x1_ref[...] = x1
    ms = jnp.mean(x1 * x1, axis=-1, keepdims=True)
    h2 = x1 * lax.rsqrt(ms + EPS) * n2_ref[...]
    h2t_ref[...] = h2.T.astype(BF16)
    qp_ref[...] = _dot(h2.astype(BF16), wq_ref[...]).astype(BF16)


def _merge(xp, xs, o_f, o_b, P, o_att, gn, wgo, wmo, wout, n2, wq, tm):
    T = xp.shape[0] + xs.shape[0]
    n1 = xp.shape[0] // tm
    D = D_MODEL
    QW = PEER_HEADS * 2 * PEER_HALF
    tile = lambda w, c=0: pl.BlockSpec((tm, w), lambda i: (i, c))
    wspec = lambda shape: pl.BlockSpec(shape, lambda i: (0, 0))
    return pl.pallas_call(
        functools.partial(_merge_kernel, n1=n1),
        grid=(T // tm,),
        in_specs=_pair_specs((tm, D), n1) + [
            tile(D), tile(D), tile(D, COL_R // D), tile(2 * D, COL_GATE // (2 * D)), tile(D),
            wspec((1, D)), wspec((D, D)), wspec((D, D)), wspec((D, D)), wspec((1, D)), wspec((D, QW))],
        out_specs=[tile(D), pl.BlockSpec((D, tm), lambda i: (0, i)), tile(QW)],
        out_shape=[jax.ShapeDtypeStruct((T, D), F32), jax.ShapeDtypeStruct((D, T), BF16),
                   jax.ShapeDtypeStruct((T, QW), BF16)],
        compiler_params=_cparams(("parallel",)),
        name="merge",
    )(xp, xs, o_f, o_b, P, P, o_att, gn, wgo, wmo, wout, n2, wq)


def _extract16(cur, tie_index):
    n, t = cur.shape
    slot = lax.broadcasted_iota(jnp.int32, (PEER_TOPK, t), 0)
    vals = jnp.zeros((PEER_TOPK, t), F32)
    rank = jnp.full((n, t), float(PEER_TOPK), F32)
    for a in range(PEER_TOPK):
        m = jnp.max(cur, axis=0, keepdims=True)
        sel = cur == m
        if tie_index is not None:
            first = jnp.min(jnp.where(sel, tie_index, float(n * n)), axis=0, keepdims=True)
            sel = tie_index == first
        vals = jnp.where(slot == a, m, vals)
        rank = jnp.where(sel, float(a), rank)
        cur = jnp.where(sel, -jnp.inf, cur)
    return vals, rank


def _taken_error(rank):
    taken = jnp.sum(jnp.where(rank < float(PEER_TOPK), 1.0, 0.0), axis=0, keepdims=True)
    return jnp.abs(taken - float(PEER_TOPK))


_CAND_AB = [
    [(0, b) for b in range(8)],
    [(0, b) for b in range(8, 16)],
    [(1, b) for b in range(8)],
    [(2, b) for b in range(5)] + [(4, b) for b in range(3)],
    [(3, b) for b in range(4)] + [(5, 0), (5, 1), (6, 0), (6, 1)],
    [(7, 0), (7, 1)] + [(a, 0) for a in range(8, 14)],
    [(14, 0), (15, 0)] + [None] * 6,
]
assert sorted(p for t in _CAND_AB for p in t if p) == sorted(
    (a, b) for a in range(PEER_TOPK) for b in range(PEER_TOPK) if (a + 1) * (b + 1) <= PEER_TOPK)


def _candidate_tiles(t1, t2):
    tt = t1.shape[1]
    row = lax.broadcasted_iota(jnp.int32, (8, tt), 0)
    lo1, hi1, lo2, hi2 = t1[0:8], t1[8:16], t2[0:8], t2[8:16]
    r1 = lambda a: t1[a:a + 1, :]
    up = lambda x, k: pltpu.roll(x, k, 0)
    hi1r = up(hi1, 2)
    b0 = t2[0:1, :]
    tiles = [
        r1(0) + lo2,
        r1(0) + hi2,
        r1(1) + lo2,
        jnp.where(row < 5, r1(2) + lo2, r1(4) + up(lo2, 5)),
        jnp.where(row < 4, r1(3) + lo2, jnp.where(row < 6, r1(5) + up(lo2, 4), r1(6) + up(lo2, 6))),
        jnp.where(row < 2, r1(7) + lo2, hi1r + b0),
        jnp.where(row < 2, hi1r + b0, -jnp.inf),
    ]
    return jnp.concatenate(tiles, axis=0)


def _row_constants(values, tt):
    n = len(values)
    row = lax.broadcasted_iota(jnp.int32, (n, tt), 0)
    out = jnp.full((n, tt), float(values[-1]), F32)
    for r in range(n - 1):
        out = jnp.where(row == r, float(values[r]), out)
    return out


def _peer_topk_kernel(qp_ref, k1_ref, k2_ref, r1_ref, f1_ref, k2r_ref, f2_ref, s_sc, t_sc, rk_sc, cnt_sc, z_sc):
    K, H = PEER_TOPK, PEER_HEADS
    tt = qp_ref.shape[0]
    k1 = k1_ref[...]
    k2 = k2_ref[...]
    for h in range(H):
        o = h * 2 * PEER_HALF
        s_sc[2 * h] = _dot_nt(k1, qp_ref[:, o:o + PEER_HALF])
        s_sc[2 * h + 1] = _dot_nt(k2, qp_ref[:, o + PEER_HALF:o + 2 * PEER_HALF])

    def stage1(exact):
        err = jnp.zeros((1, tt), F32)
        if exact:
            pos = lax.broadcasted_iota(jnp.int32, (PEER_NKEYS, tt), 0)
            key_index = [pos.astype(F32), ((pos % 2) * (PEER_NKEYS // 2) + pos // 2).astype(F32)]
        for i in range(2 * H):
            vals, rank = _extract16(s_sc[i], key_index[i % 2] if exact else None)
            t_sc[i] = vals
            rk_sc[i] = rank
            if not exact:
                err = jnp.maximum(err, _taken_error(rank))
        return err

    err1 = stage1(False)

    @pl.when(jnp.max(err1) > 0.0)
    def _():
        stage1(True)

    pairs = [p for tile in _CAND_AB for p in tile]
    slot = lax.broadcasted_iota(jnp.int32, (K, tt), 0)

    def stage2(exact):
        err = jnp.zeros((1, tt), F32)
        flat = _row_constants([K * K if p is None else p[0] * K + p[1] for p in pairs], tt) if exact else None
        for h in range(H):
            cand = _candidate_tiles(t_sc[2 * h], t_sc[2 * h + 1])
            _, rank = _extract16(cand, flat)
            chosen = rank < float(K)
            ch = jnp.where(chosen, 1.0, 0.0)
            z = jnp.sum(jnp.where(chosen, jnp.exp(cand - cand[0:1, :]), 0.0), axis=0, keepdims=True)
            z_sc[h] = jnp.broadcast_to(z, (8, tt))
            cnt = jnp.zeros((K, tt), F32)
            for a in range(K):
                rows = [r for r, p in enumerate(pairs) if p is not None and p[0] == a]
                ca = jnp.sum(ch[rows[0]:rows[-1] + 1, :], axis=0, keepdims=True)
                cnt = jnp.where(slot == a, ca, cnt)
            cnt_sc[h] = cnt
            if not exact:
                err = jnp.maximum(err, _taken_error(rank))
        return err

    err2 = stage2(False)

    @pl.when(jnp.max(err2) > 0.0)
    def _():
        stage2(True)

    for h in range(H):
        s1, s2 = s_sc[2 * h], s_sc[2 * h + 1]
        t1, t2 = t_sc[2 * h], t_sc[2 * h + 1]
        rank1 = rk_sc[2 * h]
        cnt = cnt_sc[h]
        r1 = jnp.zeros(rank1.shape, F32)
        for a in range(K):
            r1 = jnp.where(rank1 == float(a), cnt[a:a + 1, :], r1)
        r1_ref[h] = r1
        f1_ref[h] = jnp.exp(s1 - t1[0:1, :]) * (0.5 / z_sc[h, 0:1, :])
        k2r_ref[h] = rk_sc[2 * h + 1].astype(BF16)
        f2_ref[h] = jnp.exp(s2 - t2[0:1, :]).astype(BF16)


def _peer_topk(qp, k1, k2, tt):
    T = qp.shape[0]
    H, N = PEER_HEADS, PEER_NKEYS
    out = jax.ShapeDtypeStruct((H, N, T), F32)
    outb = jax.ShapeDtypeStruct((H, N, T), BF16)
    ospec = pl.BlockSpec((H, N, tt), lambda i: (0, 0, i))
    return pl.pallas_call(
        _peer_topk_kernel,
        grid=(T // tt,),
        in_specs=[pl.BlockSpec((tt, H * 2 * PEER_HALF), lambda i: (i, 0)),
                  pl.BlockSpec((N, PEER_HALF), lambda i: (0, 0)),
                  pl.BlockSpec((N, PEER_HALF), lambda i: (0, 0))],
        out_specs=[ospec, ospec, ospec, ospec],
        out_shape=[out, out, outb, outb],
        scratch_shapes=[pltpu.VMEM((2 * H, N, tt), F32), pltpu.VMEM((2 * H, PEER_TOPK, tt), F32),
                        pltpu.VMEM((2 * H, N, tt), F32), pltpu.VMEM((H, PEER_TOPK, tt), F32),
                        pltpu.VMEM((H, 8, tt), F32)],
        compiler_params=_cparams(("parallel",)),
        name="peer_topk",
    )(qp, k1, k2)


def _peer_main_kernel(h2t_ref, u_ref, vt_ref, r1_ref, f1_ref, k2r_ref, f2_ref, o_ref,
                      acc_ref, s_ref, g_ref, *, ec):
    j = pl.program_id(1)
    N = PEER_NKEYS
    tt = s_ref.shape[2]
    lw = 256

    @pl.when(j == 0)
    def _():
        acc_ref[...] = jnp.zeros_like(acc_ref)
        s_ref[1] = jnp.zeros(s_ref.shape[1:], s_ref.dtype)

    def step(rd, wr):
        th = tt // 2
        for half in range(2):
            hs = slice(half * th, (half + 1) * th)
            s_ref[wr, :, hs] = _dot(pltpu.bitcast(u_ref[...], BF16), h2t_ref[:, hs])
            for e in range(ec):
                rows = slice(e * N, (e + 1) * N)
                for c in range(th // lw):
                    cs = slice(half * th + c * lw, half * th + (c + 1) * lw)
                    w = jnp.zeros((N // 16, 16, lw), BF16)
                    for h in range(PEER_HEADS):
                        r1 = jnp.broadcast_to(r1_ref[h, e:e + 1, cs], (16, lw)).astype(BF16)[None]
                        f1 = jnp.broadcast_to(f1_ref[h, e:e + 1, cs], (16, lw)).astype(BF16)[None]
                        k2 = k2r_ref[h, :, cs].reshape(N // 16, 16, lw)
                        f2 = f2_ref[h, :, cs].reshape(N // 16, 16, lw)
                        w = w + jnp.where(k2 < r1, f2, jnp.zeros((), BF16)) * f1
                    se = s_ref[rd, rows, cs]
                    act = se * (1.0 + lax.erf(se * (2.0 ** -0.5)))
                    g_ref[rows, cs] = act.astype(BF16) * w.reshape(N, lw)
            acc_ref[:, hs] += _dot(pltpu.bitcast(vt_ref[...], BF16), g_ref[:, hs])

    @pl.when(j % 2 == 0)
    def _():
        step(1, 0)

    @pl.when(j % 2 == 1)
    def _():
        step(0, 1)

    @pl.when(j == pl.num_programs(1) - 1)
    def _():
        o_ref[...] = acc_ref[...].T


def _peer_main(h2t, u, vt, r1, f1, k2r, f2, tt, ec):
    T = h2t.shape[1]
    D, H, N = D_MODEL, PEER_HEADS, PEER_NKEYS
    nc = N // ec
    lag = lambda j, k: jnp.clip(j - k, 0, nc - 1)
    return pl.pallas_call(
        functools.partial(_peer_main_kernel, ec=ec),
        grid=(T // tt, nc + 1),
        in_specs=[pl.BlockSpec((D, tt), lambda i, j: (0, i)),
                  pl.BlockSpec((ec * N // 2, D), lambda i, j: (lag(j, 0), 0)),
                  pl.BlockSpec((D // 2, ec * N), lambda i, j: (0, lag(j, 1))),
                  pl.BlockSpec((H, ec, tt), lambda i, j: (0, lag(j, 1), i)),
                  pl.BlockSpec((H, ec, tt), lambda i, j: (0, lag(j, 1), i)),
                  pl.BlockSpec((H, N, tt), lambda i, j: (0, 0, i)),
                  pl.BlockSpec((H, N, tt), lambda i, j: (0, 0, i))],
        out_specs=pl.BlockSpec((tt, D), lambda i, j: (i, 0)),
        out_shape=jax.ShapeDtypeStruct((T, D), F32),
        scratch_shapes=[pltpu.VMEM((D, tt), F32), pltpu.VMEM((2, ec * N, tt), F32),
                        pltpu.VMEM((ec * N, tt), BF16)],
        compiler_params=_cparams(("parallel", "arbitrary")),
        name="peer_main",
    )(h2t, u, vt, r1, f1, k2r, f2)


def _final_kernel(x_ref, peer_ref, pp_ref, ps_ref, wg_ref, wp_ref, fn_ref, yp_ref, ys_ref, *, n1):
    i = pl.program_id(0)
    x = x_ref[...] + peer_ref[...]
    gate = _sigmoid(_dot(x.astype(BF16), wg_ref[...]))
    p = jnp.where(i < n1, pp_ref[...], ps_ref[...])
    x = x + gate * _dot(p.astype(BF16), wp_ref[...])
    ms = jnp.mean(x * x, axis=-1, keepdims=True)
    y = x * lax.rsqrt(ms + EPS) * fn_ref[...]

    @pl.when(i < n1)
    def _():
        yp_ref[...] = y

    @pl.when(i >= n1)
    def _():
        ys_ref[...] = y


def _final(x, peer, pp, ps, wg, wp, fn, tm):
    T = x.shape[0]
    n1 = pp.shape[0] // tm
    D = D_MODEL
    return pl.pallas_call(
        functools.partial(_final_kernel, n1=n1),
        grid=(T // tm,),
        in_specs=[pl.BlockSpec((tm, D), lambda i: (i, 0)),
                  pl.BlockSpec((tm, D), lambda i: (i, 0))] + _pair_specs((tm, PLE_DIM), n1) + [
                  pl.BlockSpec((D, D), lambda i: (0, 0)),
                  pl.BlockSpec((PLE_DIM, D), lambda i: (0, 0)),
                  pl.BlockSpec((1, D), lambda i: (0, 0))],
        out_specs=_pair_specs((tm, D), n1),
        out_shape=[jax.ShapeDtypeStruct((pp.shape[0], D), F32), jax.ShapeDtypeStruct((ps.shape[0], D), F32)],
        compiler_params=_cparams(("arbitrary",)),
        name="final",
    )(x, peer, pp, ps, wg, wp, fn)


def _rot_cols(w):
    half = MLA_ROPE // 2
    return jnp.concatenate([-w[..., half:], w[..., :half]], axis=-1)


def _pack_w_in(w_in):
    pts, acc = [], 0
    for sz in (512, 512, 1024, 1024, 2 * GLA_RANK, MLA_Q_LORA, MLA_KV_LORA, MLA_ROPE):
        acc += sz
        pts.append(acc)
    q, k, v, r, a_lr, c_q, c_kv, k_r, gate = jnp.split(w_in, pts, axis=-1)
    z = lambda n: jnp.zeros((D_MODEL, n), w_in.dtype)
    packed = jnp.concatenate([gate, v, r, q, k, c_q, z(CQ_PAD - MLA_Q_LORA), c_kv, k_r, _rot_cols(k_r),
                              a_lr, z(128 - 2 * GLA_RANK)], axis=-1)
    assert packed.shape[-1] == P_COLS
    return packed.astype(BF16)


def _pack_w_uq(w_uq):
    w = w_uq.reshape(MLA_Q_LORA, MLA_HEADS, MLA_DQK)
    pe = w[..., MLA_NOPE:]
    w = jnp.concatenate([w, _rot_cols(pe)], axis=-1).reshape(MLA_Q_LORA, MLA_HEADS * 256)
    return jnp.concatenate([w, jnp.zeros((CQ_PAD - MLA_Q_LORA, MLA_HEADS * 256), w.dtype)], axis=0).astype(BF16)


def _interleave_keys(w):
    lead, c = w.shape[:-2], w.shape[-1]
    half = PEER_NKEYS // 2
    return jnp.swapaxes(w.reshape(*lead, 2, half, c), -3, -2).reshape(*lead, PEER_NKEYS, c)


def _pack_u(u):
    half = PEER_NKEYS // 2
    h = lax.bitcast_convert_type(u.astype(BF16).reshape(PEER_NKEYS, 2, half, D_MODEL), jnp.uint16).astype(jnp.uint32)
    return ((h[:, 1] << 16) | h[:, 0]).reshape(PEER_N // 2, D_MODEL)


def _pack_vt(v):
    vi = _interleave_keys(v.reshape(PEER_NKEYS, PEER_NKEYS, D_MODEL)).reshape(PEER_N, D_MODEL // 2, 2)
    return lax.bitcast_convert_type(vi.astype(BF16), jnp.uint32).T


def _tile(n, pref):
    return pref if n % pref == 0 else n


def kernel(x_prompt, x_sample, p_prompt, p_sample, norm1, w_in, gla_a_w_f, gla_a_b_f, gla_a_w_b, gla_a_b_b, gla_norm, gla_w_o, mla_q_norm, mla_w_uq, mla_kv_norm, mla_w_ukv, mla_w_o, w_out, norm2, peer_w_q, peer_k1, peer_k2, peer_u, peer_v, ple_proj, ple_gate, final_norm):
    B1, S, D = x_prompt.shape
    B2 = x_sample.shape[0]
    assert x_sample.shape[1] == S and norm1.shape[0] == 1, "one layer, equal sequence lengths"
    Bt = B1 + B2
    T = Bt * S
    T1, T2 = B1 * S, B2 * S
    xp, xs = x_prompt.reshape(T1, D), x_sample.reshape(T2, D)
    row = lambda v: v.reshape(1, -1).astype(F32)

    P = _in_proj(xp, xs, row(norm1[0]), _pack_w_in(w_in[0]), _pair_tile(T1, T2, 1024), 1024)

    o_f, o_b = _gla(P, gla_a_w_f[0], row(gla_a_b_f[0]), gla_a_w_b[0], row(gla_a_b_b[0]), Bt, S, _tile(S, 256))

    pos = jnp.arange(S, dtype=F32)
    inv = ROPE_THETA ** (-jnp.arange(MLA_ROPE // 2, dtype=F32) * 2.0 / MLA_ROPE)
    ang = pos[:, None] * inv[None, :]
    cos2 = jnp.concatenate([jnp.cos(ang), jnp.cos(ang)], axis=-1)
    sin2 = jnp.concatenate([jnp.sin(ang), jnp.sin(ang)], axis=-1)
    qn = jnp.concatenate([mla_q_norm[0], jnp.zeros((CQ_PAD - MLA_Q_LORA,), F32)]).reshape(1, CQ_PAD)
    Q, K, V = _mla_prep(P, cos2, sin2, qn, _pack_w_uq(mla_w_uq[0]), row(mla_kv_norm[0]),
                        mla_w_ukv[0].astype(BF16), Bt, S, _tile(S, 512))
    o_att = _attention(Q, K, V, _tile(S, 1024), _tile(S, 1024)).reshape(T, MLA_HEADS * MLA_V)

    gn = jnp.tile(gla_norm[0], GLA_HEADS).reshape(1, D)
    x1, h2t, qp = _merge(xp, xs, o_f, o_b, P, o_att, gn, gla_w_o[0].astype(BF16), mla_w_o[0].astype(BF16),
                        w_out[0].astype(BF16), row(norm2[0]), peer_w_q[0].astype(BF16), _pair_tile(T1, T2, 256))

    r1, f1, k2r, f2 = _peer_topk(qp, peer_k1[0].astype(BF16), _interleave_keys(peer_k2[0]).astype(BF16),
                                 _tile(T, 256))
    peer = _peer_main(h2t, _pack_u(peer_u[0]), _pack_vt(peer_v[0]), r1, f1, k2r, f2,
                      _tile(T, 1024), 8)

    yp, ys = _final(x1, peer, p_prompt[0].reshape(T1, PLE_DIM), p_sample[0].reshape(T2, PLE_DIM),
                    ple_gate[0].astype(BF16), ple_proj[0].astype(BF16), row(final_norm), _pair_tile(T1, T2, 512))
    return (yp.reshape(B1, S, D), ys.reshape(B2, S, D))
```

```python
import functools

import jax
import jax.numpy as jnp
from jax import lax
from jax.experimental import pallas as pl
from jax.experimental.pallas import tpu as pltpu

F32 = jnp.float32
BF16 = jnp.bfloat16
EPS = 1e-6

D_MODEL = 1024
PLE_DIM = 256
GLA_HEADS, GLA_DK, GLA_DV, GLA_RANK, GLA_GATE_NORM, GLA_CHUNK = 4, 128, 256, 16, 16.0, 64
MLA_HEADS, MLA_Q_LORA, MLA_KV_LORA, MLA_NOPE, MLA_ROPE, MLA_V = 8, 384, 256, 128, 64, 128
MLA_DQK = MLA_NOPE + MLA_ROPE
MLA_QK_PAD = 256
ROPE_THETA = 10000.0
LOG2E = 1.4426950408889634
PEER_HEADS, PEER_NKEYS, PEER_HALF, PEER_TOPK = 8, 128, 128, 16
PEER_N = PEER_NKEYS * PEER_NKEYS

P_COLS = 6144
COL_GATE, COL_V, COL_R, COL_Q, COL_K, COL_CQ, COL_CKV, COL_KR, COL_ALR = (
    0, 2048, 3072, 4096, 4608, 5120, 5632, 5888, 6016)
CQ_PAD = 512

VMEM_LIMIT = 56 * 1024 * 1024


def _cparams(sem):
    return pltpu.CompilerParams(dimension_semantics=sem, vmem_limit_bytes=VMEM_LIMIT)


def _dot(a, b):
    return jnp.dot(a, b, preferred_element_type=F32)


def _dot_nt(a, b):
    return lax.dot_general(a, b, (((1,), (1,)), ((), ())), preferred_element_type=F32)


def _dot_tn(a, b):
    return lax.dot_general(a, b, (((0,), (0,)), ((), ())), preferred_element_type=F32)


def _sigmoid(x):
    return 1.0 / (1.0 + jnp.exp(-x))


def _pair_specs(block, n1):
    return [pl.BlockSpec(block, lambda i, *_: (jnp.minimum(i, n1 - 1), 0)),
            pl.BlockSpec(block, lambda i, *_: (jnp.maximum(i - n1, 0), 0))]


def _pair_tile(T1, T2, pref):
    t = pref
    while T1 % t or T2 % t:
        t //= 2
    return t


def _in_proj_kernel(xp_ref, xs_ref, g_ref, w_ref, o_ref, h_ref, *, n1):
    @pl.when(pl.program_id(1) == 0)
    def _():
        x = jnp.where(pl.program_id(0) < n1, xp_ref[...], xs_ref[...])
        ms = jnp.mean(x * x, axis=-1, keepdims=True)
        h_ref[...] = (x * lax.rsqrt(ms + EPS) * g_ref[...]).astype(BF16)

    o_ref[...] = _dot(h_ref[...], w_ref[...]).astype(o_ref.dtype)


def _in_proj(xp, xs, g, w, tm, tn):
    n1, n2 = xp.shape[0] // tm, xs.shape[0] // tm
    T = xp.shape[0] + xs.shape[0]
    return pl.pallas_call(
        functools.partial(_in_proj_kernel, n1=n1),
        grid=(n1 + n2, P_COLS // tn),
        in_specs=_pair_specs((tm, D_MODEL), n1) + [
            pl.BlockSpec((1, D_MODEL), lambda i, j: (0, 0)),
            pl.BlockSpec((D_MODEL, tn), lambda i, j: (0, j))],
        out_specs=pl.BlockSpec((tm, tn), lambda i, j: (i, j)),
        out_shape=jax.ShapeDtypeStruct((T, P_COLS), BF16),
        scratch_shapes=[pltpu.VMEM((tm, D_MODEL), BF16)],
        compiler_params=_cparams(("parallel", "arbitrary")),
        name="in_proj",
    )(xp, xs, g, w)


def _gla_direction(q_ref, k_ref, v_ref, a_ref, aw_ref, ab_ref, st_ref, o_ref, a_off, backward, R):
    C = GLA_CHUNK
    nc = R // C
    row = lax.broadcasted_iota(jnp.int32, (R, R), 0)
    col = lax.broadcasted_iota(jnp.int32, (R, R), 1)
    same = (row // C) == (col // C)
    cum = same & ((col >= row) if backward else (col <= row))
    att_mask = same & ((col > row) if backward else (col <= row))
    tri = jnp.where(cum, 1.0, 0.0).astype(BF16)
    blk = jnp.where(same, 1.0, 0.0).astype(BF16)
    a = a_ref[:, a_off:a_off + GLA_RANK].astype(BF16)
    order = range(nc - 1, -1, -1) if backward else range(nc)
    for h in range(GLA_HEADS):
        ks = slice(h * GLA_DK, (h + 1) * GLA_DK)
        vs = slice(h * GLA_DV, (h + 1) * GLA_DV)
        z = _dot(a, aw_ref[:, ks].astype(BF16)) + ab_ref[:, ks]
        la = (jnp.minimum(z, 0.0) - jnp.log(1.0 + jnp.exp(-jnp.abs(z)))) / GLA_GATE_NORM
        la_hi = la.astype(BF16)
        la_lo = (la - la_hi.astype(F32)).astype(BF16)
        b = _dot(tri, la_hi) + _dot(tri, la_lo)
        tot = _dot(blk, la_hi) + _dot(blk, la_lo)
        qh = q_ref[:, ks].astype(F32) * (GLA_DK ** -0.5)
        kh = k_ref[:, ks].astype(F32)
        vh = v_ref[:, vs].astype(BF16)
        q_e = (qh * jnp.exp(b)).astype(BF16)
        k_e = (kh * jnp.exp(-b)).astype(BF16)
        k_s = (kh * jnp.exp(tot - b)).astype(BF16)
        dec = jnp.exp(tot)
        att = jnp.where(att_mask, _dot_nt(q_e, k_e), 0.0).astype(BF16)
        o_intra = _dot(att, vh)
        for c in order:
            rs = slice(c * C, (c + 1) * C)
            st = st_ref[h]
            o_ref[rs, vs] = o_intra[rs] + _dot_nt(q_e[rs], st.astype(BF16))
            st_ref[h] = dec[c * C:c * C + 1, :] * st + _dot_tn(vh[rs], k_s[rs])


def _gla_kernel(qf_ref, kf_ref, vf_ref, af_ref, qb_ref, kb_ref, vb_ref, ab_ref,
                awf_ref, abf_ref, awb_ref, abb_ref, of_ref, ob_ref, stf_ref, stb_ref, *, R):
    @pl.when(pl.program_id(1) == 0)
    def _():
        stf_ref[...] = jnp.zeros_like(stf_ref)
        stb_ref[...] = jnp.zeros_like(stb_ref)

    _gla_direction(qf_ref, kf_ref, vf_ref, af_ref, awf_ref, abf_ref, stf_ref, of_ref, 0, False, R)
    _gla_direction(qb_ref, kb_ref, vb_ref, ab_ref, awb_ref, abb_ref, stb_ref, ob_ref, GLA_RANK, True, R)


def _gla(P, awf, abf, awb, abb, Bt, S, R):
    T = Bt * S
    NB = S // R
    HK, HV = GLA_HEADS * GLA_DK, GLA_HEADS * GLA_DV
    fwd = lambda b, n: b * NB + n
    bwd = lambda b, n: b * NB + (NB - 1 - n)

    def specs(rowf):
        return [pl.BlockSpec((R, HK), lambda b, n: (rowf(b, n), COL_Q // HK)),
                pl.BlockSpec((R, HK), lambda b, n: (rowf(b, n), COL_K // HK)),
                pl.BlockSpec((R, HV), lambda b, n: (rowf(b, n), COL_V // HV)),
                pl.BlockSpec((R, 128), lambda b, n: (rowf(b, n), COL_ALR // 128))]

    wspec = lambda shape: pl.BlockSpec(shape, lambda b, n: (0, 0))
    return pl.pallas_call(
        functools.partial(_gla_kernel, R=R),
        grid=(Bt, NB),
        in_specs=specs(fwd) + specs(bwd) + [wspec((GLA_RANK, HK)), wspec((1, HK)),
                                            wspec((GLA_RANK, HK)), wspec((1, HK))],
        out_specs=[pl.BlockSpec((R, HV), lambda b, n: (fwd(b, n), 0)),
                   pl.BlockSpec((R, HV), lambda b, n: (bwd(b, n), 0))],
        out_shape=[jax.ShapeDtypeStruct((T, HV), F32), jax.ShapeDtypeStruct((T, HV), F32)],
        scratch_shapes=[pltpu.VMEM((GLA_HEADS, GLA_DV, GLA_DK), F32),
                        pltpu.VMEM((GLA_HEADS, GLA_DV, GLA_DK), F32)],
        compiler_params=_cparams(("parallel", "arbitrary")),
        name="gla",
    )(P, P, P, P, P, P, P, P, awf, abf, awb, abb)


def _mla_prep_kernel(cq_ref, ckv_ref, kr_ref, cos_ref, sin_ref, qn_ref, wuq_ref, kvn_ref, wukv_ref,
                     q_out, k_out, v_out):
    cq = cq_ref[...].astype(F32)
    ms = jnp.sum(cq * cq, axis=-1, keepdims=True) * (1.0 / MLA_Q_LORA)
    hq = (cq * lax.rsqrt(ms + EPS) * qn_ref[...]).astype(BF16)
    qq = _dot(hq, wuq_ref[...])
    ckv = ckv_ref[...].astype(F32)
    ms = jnp.mean(ckv * ckv, axis=-1, keepdims=True)
    hkv = (ckv * lax.rsqrt(ms + EPS) * kvn_ref[...]).astype(BF16)
    kv = _dot(hkv, wukv_ref[...])
    cos = cos_ref[...]
    sin = sin_ref[...]
    kr = kr_ref[...].astype(F32)
    k_pe = (kr[:, :MLA_ROPE] * cos + kr[:, MLA_ROPE:] * sin).astype(BF16)
    scale = MLA_DQK ** -0.5 * LOG2E
    zpad = jnp.zeros((cq.shape[0], MLA_QK_PAD - MLA_DQK), BF16)
    for h in range(MLA_HEADS):
        o = h * 256
        q_pe = qq[:, o + 128:o + 192] * cos + qq[:, o + 192:o + 256] * sin
        q_out[0, h, :, 0:MLA_NOPE] = (qq[:, o:o + MLA_NOPE] * scale).astype(BF16)
        q_out[0, h, :, MLA_NOPE:MLA_DQK] = (q_pe * scale).astype(BF16)
        q_out[0, h, :, MLA_DQK:MLA_QK_PAD] = zpad
        k_out[0, h, :, 0:MLA_NOPE] = kv[:, o:o + MLA_NOPE].astype(BF16)
        k_out[0, h, :, MLA_NOPE:MLA_DQK] = k_pe
        k_out[0, h, :, MLA_DQK:MLA_QK_PAD] = zpad
        v_out[0, h] = kv[:, o + MLA_NOPE:o + 256].astype(BF16)


def _mla_prep(P, cos2, sin2, qn, wuq, kvn, wukv, Bt, S, ts):
    NS = S // ts
    row = lambda b, i: b * NS + i
    wspec = lambda shape: pl.BlockSpec(shape, lambda b, i: (0, 0))
    H = MLA_HEADS
    return pl.pallas_call(
        _mla_prep_kernel,
        grid=(Bt, NS),
        in_specs=[pl.BlockSpec((ts, CQ_PAD), lambda b, i: (row(b, i), COL_CQ // CQ_PAD)),
                  pl.BlockSpec((ts, MLA_KV_LORA), lambda b, i: (row(b, i), COL_CKV // MLA_KV_LORA)),
                  pl.BlockSpec((ts, 128), lambda b, i: (row(b, i), COL_KR // 128)),
                  pl.BlockSpec((ts, MLA_ROPE), lambda b, i: (i, 0)),
                  pl.BlockSpec((ts, MLA_ROPE), lambda b, i: (i, 0)),
                  wspec((1, CQ_PAD)), wspec((CQ_PAD, H * 256)),
                  wspec((1, MLA_KV_LORA)), wspec((MLA_KV_LORA, H * 256))],
        out_specs=[pl.BlockSpec((1, H, ts, MLA_QK_PAD), lambda b, i: (b, 0, i, 0)),
                   pl.BlockSpec((1, H, ts, MLA_QK_PAD), lambda b, i: (b, 0, i, 0)),
                   pl.BlockSpec((1, H, ts, MLA_V), lambda b, i: (b, 0, i, 0))],
        out_shape=[jax.ShapeDtypeStruct((Bt, H, S, MLA_QK_PAD), BF16),
                   jax.ShapeDtypeStruct((Bt, H, S, MLA_QK_PAD), BF16),
                   jax.ShapeDtypeStruct((Bt, H, S, MLA_V), BF16)],
        compiler_params=_cparams(("parallel", "parallel")),
        name="mla_prep",
    )(P, P, P, cos2, sin2, qn, wuq, kvn, wukv)


ATTN_HEADS_PER_STEP = 2


def _attn_kernel(q_ref, k_ref, v_ref, o_ref, m_ref, l_ref, acc_ref):
    kv = pl.program_id(3)
    nl = k_ref.shape[2] // 128

    @pl.when(kv == 0)
    def _():
        m_ref[...] = jnp.full_like(m_ref, -jnp.inf)
        l_ref[...] = jnp.zeros_like(l_ref)
        acc_ref[...] = jnp.zeros_like(acc_ref)

    for h in range(ATTN_HEADS_PER_STEP):
        s = _dot_nt(q_ref[0, h], k_ref[0, h])
        m_prev = m_ref[h]
        m_new = jnp.maximum(m_prev, jnp.max(s, axis=-1, keepdims=True))
        alpha = jnp.exp2(m_prev - m_new)
        ps = [jnp.exp2(s[:, c * 128:(c + 1) * 128] - m_new) for c in range(nl)]
        psum = ps[0]
        for c in range(1, nl):
            psum = psum + ps[c]
        p = jnp.concatenate([x.astype(BF16) for x in ps], axis=-1)
        l_ref[h] = alpha * l_ref[h] + psum
        acc_ref[h] = alpha * acc_ref[h] + _dot(p, v_ref[0, h])
        m_ref[h] = m_new

    @pl.when(kv == pl.num_programs(3) - 1)
    def _():
        for h in range(ATTN_HEADS_PER_STEP):
            l = jnp.sum(l_ref[h], axis=-1, keepdims=True)
            o_ref[0, :, h * MLA_V:(h + 1) * MLA_V] = (acc_ref[h] / l).astype(o_ref.dtype)


def _attention(Q, K, V, tq, tk):
    Bt, H, S, _ = Q.shape
    G = ATTN_HEADS_PER_STEP
    return pl.pallas_call(
        _attn_kernel,
        grid=(Bt, H // G, S // tq, S // tk),
        in_specs=[pl.BlockSpec((1, G, tq, MLA_QK_PAD), lambda b, h, i, j: (b, h, i, 0)),
                  pl.BlockSpec((1, G, tk, MLA_QK_PAD), lambda b, h, i, j: (b, h, j, 0)),
                  pl.BlockSpec((1, G, tk, MLA_V), lambda b, h, i, j: (b, h, j, 0))],
        out_specs=pl.BlockSpec((1, tq, G * MLA_V), lambda b, h, i, j: (b, i, h)),
        out_shape=jax.ShapeDtypeStruct((Bt, S, H * MLA_V), BF16),
        scratch_shapes=[pltpu.VMEM((G, tq, 128), F32), pltpu.VMEM((G, tq, 128), F32),
                        pltpu.VMEM((G, tq, MLA_V), F32)],
        compiler_params=_cparams(("parallel", "parallel", "parallel", "arbitrary")),
        name="attn",
    )(Q, K, V)


def _merge_kernel(xp_ref, xs_ref, of_ref, ob_ref, r_ref, gate_ref, oat_ref, gn_ref, wgo_ref, wmo_ref, wout_ref,
                  n2_ref, wq_ref, x1_ref, h2t_ref, qp_ref, *, n1):
    o = of_ref[...] + ob_ref[...]
    gn = gn_ref[...]
    parts = []
    for h in range(GLA_HEADS):
        vs = slice(h * GLA_DV, (h + 1) * GLA_DV)
        oh = o[:, vs]
        ms = jnp.mean(oh * oh, axis=-1, keepdims=True)
        parts.append(oh * lax.rsqrt(ms + EPS) * gn[:, vs])
    on = jnp.concatenate(parts, axis=-1)
    r = r_ref[...].astype(F32)
    ya = _dot((on * (r * _sigmoid(r))).astype(BF16), wgo_ref[...])
    yb = _dot(oat_ref[...], wmo_ref[...])
    g = _sigmoid(gate_ref[...].astype(F32))
    mix = g[:, :D_MODEL] * ya + g[:, D_MODEL:] * yb
    x = jnp.where(pl.program_id(0) < n1, xp_ref[...], xs_ref[...])
    x1 = x + _dot(mix.astype(BF16), wout_ref[...])
    x1_ref[...] = x1
    ms = jnp.mean(x1 * x1, axis=-1, keepdims=True)
    h2 = x1 * lax.rsqrt(ms + EPS) * n2_ref[...]
    h2t_ref[...] = h2.T.astype(BF16)
    qp_ref[...] = _dot(h2.astype(BF16), wq_ref[...]).astype(BF16)


def _merge(xp, xs, o_f, o_b, P, o_att, gn, wgo, wmo, wout, n2, wq, tm):
    T = xp.shape[0] + xs.shape[0]
    n1 = xp.shape[0] // tm
    D = D_MODEL
    QW = PEER_HEADS * 2 * PEER_HALF
    tile = lambda w, c=0: pl.BlockSpec((tm, w), lambda i: (i, c))
    wspec = lambda shape: pl.BlockSpec(shape, lambda i: (0, 0))
    return pl.pallas_call(
        functools.partial(_merge_kernel, n1=n1),
        grid=(T // tm,),
        in_specs=_pair_specs((tm, D), n1) + [
            tile(D), tile(D), tile(D, COL_R // D), tile(2 * D, COL_GATE // (2 * D)), tile(D),
            wspec((1, D)), wspec((D, D)), wspec((D, D)), wspec((D, D)), wspec((1, D)), wspec((D, QW))],
        out_specs=[tile(D), pl.BlockSpec((D, tm), lambda i: (0, i)), tile(QW)],
        out_shape=[jax.ShapeDtypeStruct((T, D), F32), jax.ShapeDtypeStruct((D, T), BF16),
                   jax.ShapeDtypeStruct((T, QW), BF16)],
        compiler_params=_cparams(("parallel",)),
        name="merge",
    )(xp, xs, o_f, o_b, P, P, o_att, gn, wgo, wmo, wout, n2, wq)


def _extract16(cur, tie_index):
    n, t = cur.shape
    slot = lax.broadcasted_iota(jnp.int32, (PEER_TOPK, t), 0)
    vals = jnp.zeros((PEER_TOPK, t), F32)
    rank = jnp.full((n, t), float(PEER_TOPK), F32)
    for a in range(PEER_TOPK):
        m = jnp.max(cur, axis=0, keepdims=True)
        sel = cur == m
        if tie_index is not None:
            first = jnp.min(jnp.where(sel, tie_index, float(n * n)), axis=0, keepdims=True)
            sel = tie_index == first
        vals = jnp.where(slot == a, m, vals)
        rank = jnp.where(sel, float(a), rank)
        cur = jnp.where(sel, -jnp.inf, cur)
    return vals, rank


def _taken_error(rank):
    taken = jnp.sum(jnp.where(rank < float(PEER_TOPK), 1.0, 0.0), axis=0, keepdims=True)
    return jnp.abs(taken - float(PEER_TOPK))


_CAND_AB = [
    [(0, b) for b in range(8)],
    [(0, b) for b in range(8, 16)],
    [(1, b) for b in range(8)],
    [(2, b) for b in range(5)] + [(4, b) for b in range(3)],
    [(3, b) for b in range(4)] + [(5, 0), (5, 1), (6, 0), (6, 1)],
    [(7, 0), (7, 1)] + [(a, 0) for a in range(8, 14)],
    [(14, 0), (15, 0)] + [None] * 6,
]
assert sorted(p for t in _CAND_AB for p in t if p) == sorted(
    (a, b) for a in range(PEER_TOPK) for b in range(PEER_TOPK) if (a + 1) * (b + 1) <= PEER_TOPK)


def _candidate_tiles(t1, t2):
    tt = t1.shape[1]
    row = lax.broadcasted_iota(jnp.int32, (8, tt), 0)
    lo1, hi1, lo2, hi2 = t1[0:8], t1[8:16], t2[0:8], t2[8:16]
    r1 = lambda a: t1[a:a + 1, :]
    up = lambda x, k: pltpu.roll(x, k, 0)
    hi1r = up(hi1, 2)
    b0 = t2[0:1, :]
    tiles = [
        r1(0) + lo2,
        r1(0) + hi2,
        r1(1) + lo2,
        jnp.where(row < 5, r1(2) + lo2, r1(4) + up(lo2, 5)),
        jnp.where(row < 4, r1(3) + lo2, jnp.where(row < 6, r1(5) + up(lo2, 4), r1(6) + up(lo2, 6))),
        jnp.where(row < 2, r1(7) + lo2, hi1r + b0),
        jnp.where(row < 2, hi1r + b0, -jnp.inf),
    ]
    return jnp.concatenate(tiles, axis=0)


def _row_constants(values, tt):
    n = len(values)
    row = lax.broadcasted_iota(jnp.int32, (n, tt), 0)
    out = jnp.full((n, tt), float(values[-1]), F32)
    for r in range(n - 1):
        out = jnp.where(row == r, float(values[r]), out)
    return out


def _peer_topk_kernel(qp_ref, k1_ref, k2_ref, r1_ref, f1_ref, k2r_ref, f2_ref, s_sc, t_sc, rk_sc, cnt_sc, z_sc):
    K, H = PEER_TOPK, PEER_HEADS
    tt = qp_ref.shape[0]
    k1 = k1_ref[...]
    k2 = k2_ref[...]
    for h in range(H):
        o = h * 2 * PEER_HALF
        s_sc[2 * h] = _dot_nt(k1, qp_ref[:, o:o + PEER_HALF])
        s_sc[2 * h + 1] = _dot_nt(k2, qp_ref[:, o + PEER_HALF:o + 2 * PEER_HALF])

    def stage1(exact):
        err = jnp.zeros((1, tt), F32)
        if exact:
            pos = lax.broadcasted_iota(jnp.int32, (PEER_NKEYS, tt), 0)
            key_index = [pos.astype(F32), ((pos % 2) * (PEER_NKEYS // 2) + pos // 2).astype(F32)]
        for i in range(2 * H):
            vals, rank = _extract16(s_sc[i], key_index[i % 2] if exact else None)
            t_sc[i] = vals
            rk_sc[i] = rank
            if not exact:
                err = jnp.maximum(err, _taken_error(rank))
        return err

    err1 = stage1(False)

    @pl.when(jnp.max(err1) > 0.0)
    def _():
        stage1(True)

    pairs = [p for tile in _CAND_AB for p in tile]
    slot = lax.broadcasted_iota(jnp.int32, (K, tt), 0)

    def stage2(exact):
        err = jnp.zeros((1, tt), F32)
        flat = _row_constants([K * K if p is None else p[0] * K + p[1] for p in pairs], tt) if exact else None
        for h in range(H):
            cand = _candidate_tiles(t_sc[2 * h], t_sc[2 * h + 1])
            _, rank = _extract16(cand, flat)
            chosen = rank < float(K)
            ch = jnp.where(chosen, 1.0, 0.0)
            z = jnp.sum(jnp.where(chosen, jnp.exp(cand - cand[0:1, :]), 0.0), axis=0, keepdims=True)
            z_sc[h] = jnp.broadcast_to(z, (8, tt))
            cnt = jnp.zeros((K, tt), F32)
            for a in range(K):
                rows = [r for r, p in enumerate(pairs) if p is not None and p[0] == a]
                ca = jnp.sum(ch[rows[0]:rows[-1] + 1, :], axis=0, keepdims=True)
                cnt = jnp.where(slot == a, ca, cnt)
            cnt_sc[h] = cnt
            if not exact:
                err = jnp.maximum(err, _taken_error(rank))
        return err

    err2 = stage2(False)

    @pl.when(jnp.max(err2) > 0.0)
    def _():
        stage2(True)

    for h in range(H):
        s1, s2 = s_sc[2 * h], s_sc[2 * h + 1]
        t1, t2 = t_sc[2 * h], t_sc[2 * h + 1]
        rank1 = rk_sc[2 * h]
        cnt = cnt_sc[h]
        r1 = jnp.zeros(rank1.shape, F32)
        for a in range(K):
            r1 = jnp.where(rank1 == float(a), cnt[a:a + 1, :], r1)
        r1_ref[h] = r1
        f1_ref[h] = jnp.exp(s1 - t1[0:1, :]) * (0.5 / z_sc[h, 0:1, :])
        k2r_ref[h] = rk_sc[2 * h + 1].astype(BF16)
        f2_ref[h] = jnp.exp(s2 - t2[0:1, :]).astype(BF16)


def _peer_topk(qp, k1, k2, tt):
    T = qp.shape[0]
    H, N = PEER_HEADS, PEER_NKEYS
    out = jax.ShapeDtypeStruct((H, N, T), F32)
    outb = jax.ShapeDtypeStruct((H, N, T), BF16)
    ospec = pl.BlockSpec((H, N, tt), lambda i: (0, 0, i))
    return pl.pallas_call(
        _peer_topk_kernel,
        grid=(T // tt,),
        in_specs=[pl.BlockSpec((tt, H * 2 * PEER_HALF), lambda i: (i, 0)),
                  pl.BlockSpec((N, PEER_HALF), lambda i: (0, 0)),
                  pl.BlockSpec((N, PEER_HALF), lambda i: (0, 0))],
        out_specs=[ospec, ospec, ospec, ospec],
        out_shape=[out, out, outb, outb],
        scratch_shapes=[pltpu.VMEM((2 * H, N, tt), F32), pltpu.VMEM((2 * H, PEER_TOPK, tt), F32),
                        pltpu.VMEM((2 * H, N, tt), F32), pltpu.VMEM((H, PEER_TOPK, tt), F32),
                        pltpu.VMEM((H, 8, tt), F32)],
        compiler_params=_cparams(("parallel",)),
        name="peer_topk",
    )(qp, k1, k2)


def _peer_main_kernel(h2t_ref, u_ref, vt_ref, r1_ref, f1_ref, k2r_ref, f2_ref, o_ref,
                      acc_ref, s_ref, g_ref, *, ec):
    j = pl.program_id(1)
    N = PEER_NKEYS
    tt = s_ref.shape[2]
    lw = 256

    @pl.when(j == 0)
    def _():
        acc_ref[...] = jnp.zeros_like(acc_ref)
        s_ref[1] = jnp.zeros(s_ref.shape[1:], s_ref.dtype)

    def step(rd, wr):
        th = tt // 2
        for half in range(2):
            hs = slice(half * th, (half + 1) * th)
            s_ref[wr, :, hs] = _dot(pltpu.bitcast(u_ref[...], BF16), h2t_ref[:, hs])
            for e in range(ec):
                rows = slice(e * N, (e + 1) * N)
                for c in range(th // lw):
                    cs = slice(half * th + c * lw, half * th + (c + 1) * lw)
                    w = jnp.zeros((N // 16, 16, lw), BF16)
                    for h in range(PEER_HEADS):
                        r1 = jnp.broadcast_to(r1_ref[h, e:e + 1, cs], (16, lw)).astype(BF16)[None]
                        f1 = jnp.broadcast_to(f1_ref[h, e:e + 1, cs], (16, lw)).astype(BF16)[None]
                        k2 = k2r_ref[h, :, cs].reshape(N // 16, 16, lw)
                        f2 = f2_ref[h, :, cs].reshape(N // 16, 16, lw)
                        w = w + jnp.where(k2 < r1, f2, jnp.zeros((), BF16)) * f1
                    se = s_ref[rd, rows, cs]
                    act = se * (1.0 + lax.erf(se * (2.0 ** -0.5)))
                    g_ref[rows, cs] = act.astype(BF16) * w.reshape(N, lw)
            res = _dot(pltpu.bitcast(vt_ref[...], BF16), g_ref[:, hs])
            for c in range(th // 128):
                acc_ref[half * (th // 128) + c] += res[:, c * 128:(c + 1) * 128]

    @pl.when(j % 2 == 0)
    def _():
        step(1, 0)

    @pl.when(j % 2 == 1)
    def _():
        step(0, 1)

    @pl.when(j == pl.num_programs(1) - 1)
    def _():
        nf = acc_ref.shape[1] // 2
        for c in range(acc_ref.shape[0]):
            feats = [acc_ref[c, pl.ds(s, nf, stride=2), :] for s in range(2)]
            o_ref[c * 128:(c + 1) * 128, :] = jnp.concatenate(feats, axis=0).T


def _peer_main(h2t, u, vt, r1, f1, k2r, f2, tt, ec):
    T = h2t.shape[1]
    D, H, N = D_MODEL, PEER_HEADS, PEER_NKEYS
    nc = N // ec
    lag = lambda j, k: jnp.clip(j - k, 0, nc - 1)
    return pl.pallas_call(
        functools.partial(_peer_main_kernel, ec=ec),
        grid=(T // tt, nc + 1),
        in_specs=[pl.BlockSpec((D, tt), lambda i, j: (0, i)),
                  pl.BlockSpec((ec * N // 2, D), lambda i, j: (lag(j, 0), 0)),
                  pl.BlockSpec((D // 2, ec * N), lambda i, j: (0, lag(j, 1))),
                  pl.BlockSpec((H, ec, tt), lambda i, j: (0, lag(j, 1), i)),
                  pl.BlockSpec((H, ec, tt), lambda i, j: (0, lag(j, 1), i)),
                  pl.BlockSpec((H, N, tt), lambda i, j: (0, 0, i)),
                  pl.BlockSpec((H, N, tt), lambda i, j: (0, 0, i))],
        out_specs=pl.BlockSpec((tt, D), lambda i, j: (i, 0)),
        out_shape=jax.ShapeDtypeStruct((T, D), F32),
        scratch_shapes=[pltpu.VMEM((tt // 128, D, 128), F32), pltpu.VMEM((2, ec * N, tt), F32),
                        pltpu.VMEM((ec * N, tt), BF16)],
        compiler_params=_cparams(("parallel", "arbitrary")),
        name="peer_main",
    )(h2t, u, vt, r1, f1, k2r, f2)


def _final_kernel(x_ref, peer_ref, pp_ref, ps_ref, wg_ref, wp_ref, fn_ref, yp_ref, ys_ref, *, n1):
    i = pl.program_id(0)
    x = x_ref[...] + peer_ref[...]
    gate = _sigmoid(_dot(x.astype(BF16), wg_ref[...]))
    p = jnp.where(i < n1, pp_ref[...], ps_ref[...])
    x = x + gate * _dot(p.astype(BF16), wp_ref[...])
    ms = jnp.mean(x * x, axis=-1, keepdims=True)
    y = x * lax.rsqrt(ms + EPS) * fn_ref[...]

    @pl.when(i < n1)
    def _():
        yp_ref[...] = y

    @pl.when(i >= n1)
    def _():
        ys_ref[...] = y


def _final(x, peer, pp, ps, wg, wp, fn, tm):
    T = x.shape[0]
    n1 = pp.shape[0] // tm
    D = D_MODEL
    return pl.pallas_call(
        functools.partial(_final_kernel, n1=n1),
        grid=(T // tm,),
        in_specs=[pl.BlockSpec((tm, D), lambda i: (i, 0)),
                  pl.BlockSpec((tm, D), lambda i: (i, 0))] + _pair_specs((tm, PLE_DIM), n1) + [
                  pl.BlockSpec((D, D), lambda i: (0, 0)),
                  pl.BlockSpec((PLE_DIM, D), lambda i: (0, 0)),
                  pl.BlockSpec((1, D), lambda i: (0, 0))],
        out_specs=_pair_specs((tm, D), n1),
        out_shape=[jax.ShapeDtypeStruct((pp.shape[0], D), F32), jax.ShapeDtypeStruct((ps.shape[0], D), F32)],
        compiler_params=_cparams(("arbitrary",)),
        name="final",
    )(x, peer, pp, ps, wg, wp, fn)


def _rot_cols(w):
    half = MLA_ROPE // 2
    return jnp.concatenate([-w[..., half:], w[..., :half]], axis=-1)


def _pack_w_in(w_in):
    pts, acc = [], 0
    for sz in (512, 512, 1024, 1024, 2 * GLA_RANK, MLA_Q_LORA, MLA_KV_LORA, MLA_ROPE):
        acc += sz
        pts.append(acc)
    q, k, v, r, a_lr, c_q, c_kv, k_r, gate = jnp.split(w_in, pts, axis=-1)
    z = lambda n: jnp.zeros((D_MODEL, n), w_in.dtype)
    packed = jnp.concatenate([gate, v, r, q, k, c_q, z(CQ_PAD - MLA_Q_LORA), c_kv, k_r, _rot_cols(k_r),
                              a_lr, z(128 - 2 * GLA_RANK)], axis=-1)
    assert packed.shape[-1] == P_COLS
    return packed.astype(BF16)


def _pack_w_uq(w_uq):
    w = w_uq.reshape(MLA_Q_LORA, MLA_HEADS, MLA_DQK)
    pe = w[..., MLA_NOPE:]
    w = jnp.concatenate([w, _rot_cols(pe)], axis=-1).reshape(MLA_Q_LORA, MLA_HEADS * 256)
    return jnp.concatenate([w, jnp.zeros((CQ_PAD - MLA_Q_LORA, MLA_HEADS * 256), w.dtype)], axis=0).astype(BF16)


def _interleave_keys(w):
    lead, c = w.shape[:-2], w.shape[-1]
    half = PEER_NKEYS // 2
    return jnp.swapaxes(w.reshape(*lead, 2, half, c), -3, -2).reshape(*lead, PEER_NKEYS, c)


def _pack_u(u):
    half = PEER_NKEYS // 2
    h = lax.bitcast_convert_type(u.astype(BF16).reshape(PEER_NKEYS, 2, half, D_MODEL), jnp.uint16).astype(jnp.uint32)
    return ((h[:, 1] << 16) | h[:, 0]).reshape(PEER_N // 2, D_MODEL)


def _pack_vt(v):
    vt = _interleave_keys(v.reshape(PEER_NKEYS, PEER_NKEYS, D_MODEL)).reshape(PEER_N, D_MODEL).astype(BF16).T
    h = lax.bitcast_convert_type(vt.reshape(2, D_MODEL // 2, PEER_N), jnp.uint16).astype(jnp.uint32)
    return (h[1] << 16) | h[0]


def _tile(n, pref):
    return pref if n % pref == 0 else n


def kernel(x_prompt, x_sample, p_prompt, p_sample, norm1, w_in, gla_a_w_f, gla_a_b_f, gla_a_w_b, gla_a_b_b, gla_norm, gla_w_o, mla_q_norm, mla_w_uq, mla_kv_norm, mla_w_ukv, mla_w_o, w_out, norm2, peer_w_q, peer_k1, peer_k2, peer_u, peer_v, ple_proj, ple_gate, final_norm):
    B1, S, D = x_prompt.shape
    B2 = x_sample.shape[0]
    assert x_sample.shape[1] == S and norm1.shape[0] == 1, "one layer, equal sequence lengths"
    Bt = B1 + B2
    T = Bt * S
    T1, T2 = B1 * S, B2 * S
    xp, xs = x_prompt.reshape(T1, D), x_sample.reshape(T2, D)
    row = lambda v: v.reshape(1, -1).astype(F32)

    P = _in_proj(xp, xs, row(norm1[0]), _pack_w_in(w_in[0]), _pair_tile(T1, T2, 1024), 1024)

    o_f, o_b = _gla(P, gla_a_w_f[0], row(gla_a_b_f[0]), gla_a_w_b[0], row(gla_a_b_b[0]), Bt, S, _tile(S, 256))

    pos = jnp.arange(S, dtype=F32)
    inv = ROPE_THETA ** (-jnp.arange(MLA_ROPE // 2, dtype=F32) * 2.0 / MLA_ROPE)
    ang = pos[:, None] * inv[None, :]
    cos2 = jnp.concatenate([jnp.cos(ang), jnp.cos(ang)], axis=-1)
    sin2 = jnp.concatenate([jnp.sin(ang), jnp.sin(ang)], axis=-1)
    qn = jnp.concatenate([mla_q_norm[0], jnp.zeros((CQ_PAD - MLA_Q_LORA,), F32)]).reshape(1, CQ_PAD)
    Q, K, V = _mla_prep(P, cos2, sin2, qn, _pack_w_uq(mla_w_uq[0]), row(mla_kv_norm[0]),
                        mla_w_ukv[0].astype(BF16), Bt, S, _tile(S, 512))
    o_att = _attention(Q, K, V, _tile(S, 1024), _tile(S, 1024)).reshape(T, MLA_HEADS * MLA_V)

    gn = jnp.tile(gla_norm[0], GLA_HEADS).reshape(1, D)
    x1, h2t, qp = _merge(xp, xs, o_f, o_b, P, o_att, gn, gla_w_o[0].astype(BF16), mla_w_o[0].astype(BF16),
                        w_out[0].astype(BF16), row(norm2[0]), peer_w_q[0].astype(BF16), _pair_tile(T1, T2, 256))

    r1, f1, k2r, f2 = _peer_topk(qp, peer_k1[0].astype(BF16), _interleave_keys(peer_k2[0]).astype(BF16),
                                 _tile(T, 256))
    peer = _peer_main(h2t, _pack_u(peer_u[0]), _pack_vt(peer_v[0]), r1, f1, k2r, f2,
                      _tile(T, 1024), 8)

    yp, ys = _final(x1, peer, p_prompt[0].reshape(T1, PLE_DIM), p_sample[0].reshape(T2, PLE_DIM),
                    ple_gate[0].astype(BF16), ple_proj[0].astype(BF16), row(final_norm), _pair_tile(T1, T2, 512))
    return (yp.reshape(B1, S, D), ys.reshape(B2, S, D))
```

```python
import functools
from typing import NamedTuple

import jax
import jax.numpy as jnp
from jax import lax
from jax.experimental import pallas as pl
from jax.experimental.pallas import tpu as pltpu

F32 = jnp.float32
BF16 = jnp.bfloat16
EPS = 1e-6

D_MODEL = 1024
PLE_DIM = 256
GLA_HEADS, GLA_DK, GLA_DV, GLA_RANK, GLA_GATE_NORM, GLA_CHUNK = 4, 128, 256, 16, 16.0, 64
MLA_HEADS, MLA_Q_LORA, MLA_KV_LORA, MLA_NOPE, MLA_ROPE, MLA_V = 8, 384, 256, 128, 64, 128
MLA_DQK = MLA_NOPE + MLA_ROPE
MLA_QK_PAD = 256
ROPE_THETA = 10000.0
LOG2E = 1.4426950408889634
PEER_HEADS, PEER_NKEYS, PEER_HALF, PEER_TOPK = 8, 128, 128, 16
PEER_N = PEER_NKEYS * PEER_NKEYS

P_COLS = 6144
COL_GATE, COL_V, COL_R, COL_Q, COL_K, COL_CQ, COL_CKV, COL_KR, COL_ALR = (
    0, 2048, 3072, 4096, 4608, 5120, 5632, 5888, 6016)
CQ_PAD = 512

VMEM_LIMIT = 56 * 1024 * 1024


def _cparams(sem):
    return pltpu.CompilerParams(dimension_semantics=sem, vmem_limit_bytes=VMEM_LIMIT)


def _dot(a, b):
    return jnp.dot(a, b, preferred_element_type=F32)


def _dot_nt(a, b):
    return lax.dot_general(a, b, (((1,), (1,)), ((), ())), preferred_element_type=F32)


def _dot_tn(a, b):
    return lax.dot_general(a, b, (((0,), (0,)), ((), ())), preferred_element_type=F32)


def _sigmoid(x):
    return 1.0 / (1.0 + jnp.exp(-x))


def _pair_specs(block, n1):
    return [pl.BlockSpec(block, lambda i, *_: (jnp.minimum(i, n1 - 1), 0)),
            pl.BlockSpec(block, lambda i, *_: (jnp.maximum(i - n1, 0), 0))]


def _pair_tile(T1, T2, pref):
    t = pref
    while T1 % t or T2 % t:
        t //= 2
    return t


def _in_proj_kernel(xp_ref, xs_ref, g_ref, w_ref, o_ref, h_ref, *, n1):
    @pl.when(pl.program_id(1) == 0)
    def _():
        x = jnp.where(pl.program_id(0) < n1, xp_ref[...], xs_ref[...])
        ms = jnp.mean(x * x, axis=-1, keepdims=True)
        h_ref[...] = (x * lax.rsqrt(ms + EPS) * g_ref[...]).astype(BF16)

    o_ref[...] = _dot(h_ref[...], w_ref[...]).astype(o_ref.dtype)


def _in_proj(xp, xs, g, w, tm, tn):
    n1, n2 = xp.shape[0] // tm, xs.shape[0] // tm
    T = xp.shape[0] + xs.shape[0]
    return pl.pallas_call(
        functools.partial(_in_proj_kernel, n1=n1),
        grid=(n1 + n2, P_COLS // tn),
        in_specs=_pair_specs((tm, D_MODEL), n1) + [
            pl.BlockSpec((1, D_MODEL), lambda i, j: (0, 0)),
            pl.BlockSpec((D_MODEL, tn), lambda i, j: (0, j))],
        out_specs=pl.BlockSpec((tm, tn), lambda i, j: (i, j)),
        out_shape=jax.ShapeDtypeStruct((T, P_COLS), BF16),
        scratch_shapes=[pltpu.VMEM((tm, D_MODEL), BF16)],
        compiler_params=_cparams(("parallel", "arbitrary")),
        name="in_proj",
    )(xp, xs, g, w)


def _gla_direction(q_ref, k_ref, v_ref, a_ref, aw_ref, ab_ref, st_ref, o_ref, a_off, backward, R):
    C = GLA_CHUNK
    nc = R // C
    row = lax.broadcasted_iota(jnp.int32, (R, R), 0)
    col = lax.broadcasted_iota(jnp.int32, (R, R), 1)
    same = (row // C) == (col // C)
    cum = same & ((col >= row) if backward else (col <= row))
    att_mask = same & ((col > row) if backward else (col <= row))
    tri = jnp.where(cum, 1.0, 0.0).astype(BF16)
    blk = jnp.where(same, 1.0, 0.0).astype(BF16)
    a = a_ref[:, a_off:a_off + GLA_RANK].astype(BF16)
    order = range(nc - 1, -1, -1) if backward else range(nc)
    for h in range(GLA_HEADS):
        ks = slice(h * GLA_DK, (h + 1) * GLA_DK)
        vs = slice(h * GLA_DV, (h + 1) * GLA_DV)
        z = _dot(a, aw_ref[:, ks].astype(BF16)) + ab_ref[:, ks]
        la = (jnp.minimum(z, 0.0) - jnp.log(1.0 + jnp.exp(-jnp.abs(z)))) / GLA_GATE_NORM
        la_hi = la.astype(BF16)
        la_lo = (la - la_hi.astype(F32)).astype(BF16)
        b = _dot(tri, la_hi) + _dot(tri, la_lo)
        tot = _dot(blk, la_hi) + _dot(blk, la_lo)
        qh = q_ref[:, ks].astype(F32) * (GLA_DK ** -0.5)
        kh = k_ref[:, ks].astype(F32)
        vh = v_ref[:, vs].astype(BF16)
        q_e = (qh * jnp.exp(b)).astype(BF16)
        k_e = (kh * jnp.exp(-b)).astype(BF16)
        k_s = (kh * jnp.exp(tot - b)).astype(BF16)
        dec = jnp.exp(tot)
        att = jnp.where(att_mask, _dot_nt(q_e, k_e), 0.0).astype(BF16)
        o_intra = _dot(att, vh)
        for c in order:
            rs = slice(c * C, (c + 1) * C)
            st = st_ref[h]
            o_ref[rs, vs] = o_intra[rs] + _dot_nt(q_e[rs], st.astype(BF16))
            st_ref[h] = dec[c * C:c * C + 1, :] * st + _dot_tn(vh[rs], k_s[rs])


def _gla_kernel(qf_ref, kf_ref, vf_ref, af_ref, qb_ref, kb_ref, vb_ref, ab_ref,
                awf_ref, abf_ref, awb_ref, abb_ref, of_ref, ob_ref, stf_ref, stb_ref, *, R):
    @pl.when(pl.program_id(1) == 0)
    def _():
        stf_ref[...] = jnp.zeros_like(stf_ref)
        stb_ref[...] = jnp.zeros_like(stb_ref)

    _gla_direction(qf_ref, kf_ref, vf_ref, af_ref, awf_ref, abf_ref, stf_ref, of_ref, 0, False, R)
    _gla_direction(qb_ref, kb_ref, vb_ref, ab_ref, awb_ref, abb_ref, stb_ref, ob_ref, GLA_RANK, True, R)


def _gla(P, awf, abf, awb, abb, Bt, S, R):
    T = Bt * S
    NB = S // R
    HK, HV = GLA_HEADS * GLA_DK, GLA_HEADS * GLA_DV
    fwd = lambda b, n: b * NB + n
    bwd = lambda b, n: b * NB + (NB - 1 - n)

    def specs(rowf):
        return [pl.BlockSpec((R, HK), lambda b, n: (rowf(b, n), COL_Q // HK)),
                pl.BlockSpec((R, HK), lambda b, n: (rowf(b, n), COL_K // HK)),
                pl.BlockSpec((R, HV), lambda b, n: (rowf(b, n), COL_V // HV)),
                pl.BlockSpec((R, 128), lambda b, n: (rowf(b, n), COL_ALR // 128))]

    wspec = lambda shape: pl.BlockSpec(shape, lambda b, n: (0, 0))
    return pl.pallas_call(
        functools.partial(_gla_kernel, R=R),
        grid=(Bt, NB),
        in_specs=specs(fwd) + specs(bwd) + [wspec((GLA_RANK, HK)), wspec((1, HK)),
                                            wspec((GLA_RANK, HK)), wspec((1, HK))],
        out_specs=[pl.BlockSpec((R, HV), lambda b, n: (fwd(b, n), 0)),
                   pl.BlockSpec((R, HV), lambda b, n: (bwd(b, n), 0))],
        out_shape=[jax.ShapeDtypeStruct((T, HV), F32), jax.ShapeDtypeStruct((T, HV), F32)],
        scratch_shapes=[pltpu.VMEM((GLA_HEADS, GLA_DV, GLA_DK), F32),
                        pltpu.VMEM((GLA_HEADS, GLA_DV, GLA_DK), F32)],
        compiler_params=_cparams(("parallel", "arbitrary")),
        name="gla",
    )(P, P, P, P, P, P, P, P, awf, abf, awb, abb)


def _mla_prep_kernel(cq_ref, ckv_ref, kr_ref, cos_ref, sin_ref, qn_ref, wuq_ref, kvn_ref, wukv_ref,
                     q_out, k_out, v_out):
    cq = cq_ref[...].astype(F32)
    ms = jnp.sum(cq * cq, axis=-1, keepdims=True) * (1.0 / MLA_Q_LORA)
    hq = (cq * lax.rsqrt(ms + EPS) * qn_ref[...]).astype(BF16)
    qq = _dot(hq, wuq_ref[...])
    ckv = ckv_ref[...].astype(F32)
    ms = jnp.mean(ckv * ckv, axis=-1, keepdims=True)
    hkv = (ckv * lax.rsqrt(ms + EPS) * kvn_ref[...]).astype(BF16)
    kv = _dot(hkv, wukv_ref[...])
    cos = cos_ref[...]
    sin = sin_ref[...]
    kr = kr_ref[...].astype(F32)
    k_pe = (kr[:, :MLA_ROPE] * cos + kr[:, MLA_ROPE:] * sin).astype(BF16)
    scale = MLA_DQK ** -0.5 * LOG2E
    zpad = jnp.zeros((cq.shape[0], MLA_QK_PAD - MLA_DQK), BF16)
    for h in range(MLA_HEADS):
        o = h * 256
        q_pe = qq[:, o + 128:o + 192] * cos + qq[:, o + 192:o + 256] * sin
        q_out[0, h, :, 0:MLA_NOPE] = (qq[:, o:o + MLA_NOPE] * scale).astype(BF16)
        q_out[0, h, :, MLA_NOPE:MLA_DQK] = (q_pe * scale).astype(BF16)
        q_out[0, h, :, MLA_DQK:MLA_QK_PAD] = zpad
        k_out[0, h, :, 0:MLA_NOPE] = kv[:, o:o + MLA_NOPE].astype(BF16)
        k_out[0, h, :, MLA_NOPE:MLA_DQK] = k_pe
        k_out[0, h, :, MLA_DQK:MLA_QK_PAD] = zpad
        v_out[0, h] = kv[:, o + MLA_NOPE:o + 256].astype(BF16)


def _mla_prep(P, cos2, sin2, qn, wuq, kvn, wukv, Bt, S, ts):
    NS = S // ts
    row = lambda b, i: b * NS + i
    wspec = lambda shape: pl.BlockSpec(shape, lambda b, i: (0, 0))
    H = MLA_HEADS
    return pl.pallas_call(
        _mla_prep_kernel,
        grid=(Bt, NS),
        in_specs=[pl.BlockSpec((ts, CQ_PAD), lambda b, i: (row(b, i), COL_CQ // CQ_PAD)),
                  pl.BlockSpec((ts, MLA_KV_LORA), lambda b, i: (row(b, i), COL_CKV // MLA_KV_LORA)),
                  pl.BlockSpec((ts, 128), lambda b, i: (row(b, i), COL_KR // 128)),
                  pl.BlockSpec((ts, MLA_ROPE), lambda b, i: (i, 0)),
                  pl.BlockSpec((ts, MLA_ROPE), lambda b, i: (i, 0)),
                  wspec((1, CQ_PAD)), wspec((CQ_PAD, H * 256)),
                  wspec((1, MLA_KV_LORA)), wspec((MLA_KV_LORA, H * 256))],
        out_specs=[pl.BlockSpec((1, H, ts, MLA_QK_PAD), lambda b, i: (b, 0, i, 0)),
                   pl.BlockSpec((1, H, ts, MLA_QK_PAD), lambda b, i: (b, 0, i, 0)),
                   pl.BlockSpec((1, H, ts, MLA_V), lambda b, i: (b, 0, i, 0))],
        out_shape=[jax.ShapeDtypeStruct((Bt, H, S, MLA_QK_PAD), BF16),
                   jax.ShapeDtypeStruct((Bt, H, S, MLA_QK_PAD), BF16),
                   jax.ShapeDtypeStruct((Bt, H, S, MLA_V), BF16)],
        compiler_params=_cparams(("parallel", "parallel")),
        name="mla_prep",
    )(P, P, P, cos2, sin2, qn, wuq, kvn, wukv)


ATTN_HEADS_PER_STEP = 2


def _attn_kernel(q_ref, k_ref, v_ref, o_ref, m_ref, l_ref, acc_ref):
    kv = pl.program_id(3)
    nl = k_ref.shape[2] // 128

    @pl.when(kv == 0)
    def _():
        m_ref[...] = jnp.full_like(m_ref, -jnp.inf)
        l_ref[...] = jnp.zeros_like(l_ref)
        acc_ref[...] = jnp.zeros_like(acc_ref)

    for h in range(ATTN_HEADS_PER_STEP):
        s = _dot_nt(q_ref[0, h], k_ref[0, h])
        m_prev = m_ref[h]
        m_new = jnp.maximum(m_prev, jnp.max(s, axis=-1, keepdims=True))
        alpha = jnp.exp2(m_prev - m_new)
        ps = [jnp.exp2(s[:, c * 128:(c + 1) * 128] - m_new) for c in range(nl)]
        psum = ps[0]
        for c in range(1, nl):
            psum = psum + ps[c]
        p = jnp.concatenate([x.astype(BF16) for x in ps], axis=-1)
        l_ref[h] = alpha * l_ref[h] + psum
        acc_ref[h] = alpha * acc_ref[h] + _dot(p, v_ref[0, h])
        m_ref[h] = m_new

    @pl.when(kv == pl.num_programs(3) - 1)
    def _():
        for h in range(ATTN_HEADS_PER_STEP):
            l = jnp.sum(l_ref[h], axis=-1, keepdims=True)
            o_ref[0, :, h * MLA_V:(h + 1) * MLA_V] = (acc_ref[h] / l).astype(o_ref.dtype)


def _attention(Q, K, V, tq, tk):
    Bt, H, S, _ = Q.shape
    G = ATTN_HEADS_PER_STEP
    return pl.pallas_call(
        _attn_kernel,
        grid=(Bt, H // G, S // tq, S // tk),
        in_specs=[pl.BlockSpec((1, G, tq, MLA_QK_PAD), lambda b, h, i, j: (b, h, i, 0)),
                  pl.BlockSpec((1, G, tk, MLA_QK_PAD), lambda b, h, i, j: (b, h, j, 0)),
                  pl.BlockSpec((1, G, tk, MLA_V), lambda b, h, i, j: (b, h, j, 0))],
        out_specs=pl.BlockSpec((1, tq, G * MLA_V), lambda b, h, i, j: (b, i, h)),
        out_shape=jax.ShapeDtypeStruct((Bt, S, H * MLA_V), BF16),
        scratch_shapes=[pltpu.VMEM((G, tq, 128), F32), pltpu.VMEM((G, tq, 128), F32),
                        pltpu.VMEM((G, tq, MLA_V), F32)],
        compiler_params=_cparams(("parallel", "parallel", "parallel", "arbitrary")),
        name="attn",
    )(Q, K, V)


def _merge_kernel(xp_ref, xs_ref, of_ref, ob_ref, r_ref, gate_ref, oat_ref, gn_ref, wgo_ref, wmo_ref, wout_ref,
                  n2_ref, wq_ref, x1_ref, h2t_ref, qp_ref, *, n1):
    o = of_ref[...] + ob_ref[...]
    gn = gn_ref[...]
    parts = []
    for h in range(GLA_HEADS):
        vs = slice(h * GLA_DV, (h + 1) * GLA_DV)
        oh = o[:, vs]
        ms = jnp.mean(oh * oh, axis=-1, keepdims=True)
        parts.append(oh * lax.rsqrt(ms + EPS) * gn[:, vs])
    on = jnp.concatenate(parts, axis=-1)
    r = r_ref[...].astype(F32)
    ya = _dot((on * (r * _sigmoid(r))).astype(BF16), wgo_ref[...])
    yb = _dot(oat_ref[...], wmo_ref[...])
    g = _sigmoid(gate_ref[...].astype(F32))
    mix = g[:, :D_MODEL] * ya + g[:, D_MODEL:] * yb
    x = jnp.where(pl.program_id(0) < n1, xp_ref[...], xs_ref[...])
    x1 = x + _dot(mix.astype(BF16), wout_ref[...])
    x1_ref[...] = x1
    ms = jnp.mean(x1 * x1, axis=-1, keepdims=True)
    h2 = x1 * lax.rsqrt(ms + EPS) * n2_ref[...]
    h2t_ref[...] = h2.T.astype(BF16)
    qp_ref[...] = _dot(h2.astype(BF16), wq_ref[...]).astype(BF16)


def _merge(xp, xs, o_f, o_b, P, o_att, gn, wgo, wmo, wout, n2, wq, tm):
    T = xp.shape[0] + xs.shape[0]
    n1 = xp.shape[0] // tm
    D = D_MODEL
    QW = PEER_HEADS * 2 * PEER_HALF
    tile = lambda w, c=0: pl.BlockSpec((tm, w), lambda i: (i, c))
    wspec = lambda shape: pl.BlockSpec(shape, lambda i: (0, 0))
    return pl.pallas_call(
        functools.partial(_merge_kernel, n1=n1),
        grid=(T // tm,),
        in_specs=_pair_specs((tm, D), n1) + [
            tile(D), tile(D), tile(D, COL_R // D), tile(2 * D, COL_GATE // (2 * D)), tile(D),
            wspec((1, D)), wspec((D, D)), wspec((D, D)), wspec((D, D)), wspec((1, D)), wspec((D, QW))],
        out_specs=[tile(D), pl.BlockSpec((D, tm), lambda i: (0, i)), tile(QW)],
        out_shape=[jax.ShapeDtypeStruct((T, D), F32), jax.ShapeDtypeStruct((D, T), BF16),
                   jax.ShapeDtypeStruct((T, QW), BF16)],
        compiler_params=_cparams(("parallel",)),
        name="merge",
    )(xp, xs, o_f, o_b, P, P, o_att, gn, wgo, wmo, wout, n2, wq)


def _extract16(cur, tie_index):
    n, t = cur.shape
    slot = lax.broadcasted_iota(jnp.int32, (PEER_TOPK, t), 0)
    vals = jnp.zeros((PEER_TOPK, t), F32)
    rank = jnp.full((n, t), float(PEER_TOPK), F32)
    for a in range(PEER_TOPK):
        m = jnp.max(cur, axis=0, keepdims=True)
        sel = cur == m
        if tie_index is not None:
            first = jnp.min(jnp.where(sel, tie_index, float(n * n)), axis=0, keepdims=True)
            sel = tie_index == first
        vals = jnp.where(slot == a, m, vals)
        rank = jnp.where(sel, float(a), rank)
        cur = jnp.where(sel, -jnp.inf, cur)
    return vals, rank


def _taken_error(rank):
    taken = jnp.sum(jnp.where(rank < float(PEER_TOPK), 1.0, 0.0), axis=0, keepdims=True)
    return jnp.abs(taken - float(PEER_TOPK))


_CAND_AB = [
    [(0, b) for b in range(8)],
    [(0, b) for b in range(8, 16)],
    [(1, b) for b in range(8)],
    [(2, b) for b in range(5)] + [(4, b) for b in range(3)],
    [(3, b) for b in range(4)] + [(5, 0), (5, 1), (6, 0), (6, 1)],
    [(7, 0), (7, 1)] + [(a, 0) for a in range(8, 14)],
    [(14, 0), (15, 0)] + [None] * 6,
]
assert sorted(p for t in _CAND_AB for p in t if p) == sorted(
    (a, b) for a in range(PEER_TOPK) for b in range(PEER_TOPK) if (a + 1) * (b + 1) <= PEER_TOPK)


def _candidate_tiles(t1, t2):
    tt = t1.shape[1]
    row = lax.broadcasted_iota(jnp.int32, (8, tt), 0)
    lo1, hi1, lo2, hi2 = t1[0:8], t1[8:16], t2[0:8], t2[8:16]
    r1 = lambda a: t1[a:a + 1, :]
    up = lambda x, k: pltpu.roll(x, k, 0)
    hi1r = up(hi1, 2)
    b0 = t2[0:1, :]
    tiles = [
        r1(0) + lo2,
        r1(0) + hi2,
        r1(1) + lo2,
        jnp.where(row < 5, r1(2) + lo2, r1(4) + up(lo2, 5)),
        jnp.where(row < 4, r1(3) + lo2, jnp.where(row < 6, r1(5) + up(lo2, 4), r1(6) + up(lo2, 6))),
        jnp.where(row < 2, r1(7) + lo2, hi1r + b0),
        jnp.where(row < 2, hi1r + b0, -jnp.inf),
    ]
    return jnp.concatenate(tiles, axis=0)


def _row_constants(values, tt):
    n = len(values)
    row = lax.broadcasted_iota(jnp.int32, (n, tt), 0)
    out = jnp.full((n, tt), float(values[-1]), F32)
    for r in range(n - 1):
        out = jnp.where(row == r, float(values[r]), out)
    return out


def _peer_topk_kernel(qp_ref, k1_ref, k2_ref, r1_ref, f1_ref, k2r_ref, f2_ref, s_sc, t_sc, rk_sc, cnt_sc, z_sc):
    K, H = PEER_TOPK, PEER_HEADS
    tt = qp_ref.shape[0]
    k1 = k1_ref[...]
    k2 = k2_ref[...]
    for h in range(H):
        o = h * 2 * PEER_HALF
        s_sc[2 * h] = _dot_nt(k1, qp_ref[:, o:o + PEER_HALF])
        s_sc[2 * h + 1] = _dot_nt(k2, qp_ref[:, o + PEER_HALF:o + 2 * PEER_HALF])

    def stage1(exact):
        err = jnp.zeros((1, tt), F32)
        if exact:
            pos = lax.broadcasted_iota(jnp.int32, (PEER_NKEYS, tt), 0)
            key_index = [pos.astype(F32), ((pos % 2) * (PEER_NKEYS // 2) + pos // 2).astype(F32)]
        for i in range(2 * H):
            vals, rank = _extract16(s_sc[i], key_index[i % 2] if exact else None)
            t_sc[i] = vals
            rk_sc[i] = rank
            if not exact:
                err = jnp.maximum(err, _taken_error(rank))
        return err

    err1 = stage1(False)

    @pl.when(jnp.max(err1) > 0.0)
    def _():
        stage1(True)

    pairs = [p for tile in _CAND_AB for p in tile]
    slot = lax.broadcasted_iota(jnp.int32, (K, tt), 0)

    def stage2(exact):
        err = jnp.zeros((1, tt), F32)
        flat = _row_constants([K * K if p is None else p[0] * K + p[1] for p in pairs], tt) if exact else None
        for h in range(H):
            cand = _candidate_tiles(t_sc[2 * h], t_sc[2 * h + 1])
            _, rank = _extract16(cand, flat)
            chosen = rank < float(K)
            ch = jnp.where(chosen, 1.0, 0.0)
            z = jnp.sum(jnp.where(chosen, jnp.exp(cand - cand[0:1, :]), 0.0), axis=0, keepdims=True)
            z_sc[h] = jnp.broadcast_to(z, (8, tt))
            cnt = jnp.zeros((K, tt), F32)
            for a in range(K):
                rows = [r for r, p in enumerate(pairs) if p is not None and p[0] == a]
                ca = jnp.sum(ch[rows[0]:rows[-1] + 1, :], axis=0, keepdims=True)
                cnt = jnp.where(slot == a, ca, cnt)
            cnt_sc[h] = cnt
            if not exact:
                err = jnp.maximum(err, _taken_error(rank))
        return err

    err2 = stage2(False)

    @pl.when(jnp.max(err2) > 0.0)
    def _():
        stage2(True)

    for h in range(H):
        s1, s2 = s_sc[2 * h], s_sc[2 * h + 1]
        t1, t2 = t_sc[2 * h], t_sc[2 * h + 1]
        rank1 = rk_sc[2 * h]
        cnt = cnt_sc[h]
        r1 = jnp.zeros(rank1.shape, F32)
        for a in range(K):
            r1 = jnp.where(rank1 == float(a), cnt[a:a + 1, :], r1)
        r1_ref[h] = r1
        f1_ref[h] = jnp.exp(s1 - t1[0:1, :]) * (0.5 / z_sc[h, 0:1, :])
        k2r_ref[h] = rk_sc[2 * h + 1].astype(BF16)
        f2_ref[h] = jnp.exp(s2 - t2[0:1, :]).astype(BF16)


def _peer_topk(qp, k1, k2, tt):
    T = qp.shape[0]
    H, N = PEER_HEADS, PEER_NKEYS
    out = jax.ShapeDtypeStruct((H, N, T), F32)
    outb = jax.ShapeDtypeStruct((H, N, T), BF16)
    ospec = pl.BlockSpec((H, N, tt), lambda i: (0, 0, i))
    return pl.pallas_call(
        _peer_topk_kernel,
        grid=(T // tt,),
        in_specs=[pl.BlockSpec((tt, H * 2 * PEER_HALF), lambda i: (i, 0)),
                  pl.BlockSpec((N, PEER_HALF), lambda i: (0, 0)),
                  pl.BlockSpec((N, PEER_HALF), lambda i: (0, 0))],
        out_specs=[ospec, ospec, ospec, ospec],
        out_shape=[out, out, outb, outb],
        scratch_shapes=[pltpu.VMEM((2 * H, N, tt), F32), pltpu.VMEM((2 * H, PEER_TOPK, tt), F32),
                        pltpu.VMEM((2 * H, N, tt), F32), pltpu.VMEM((H, PEER_TOPK, tt), F32),
                        pltpu.VMEM((H, 8, tt), F32)],
        compiler_params=_cparams(("parallel",)),
        name="peer_topk",
    )(qp, k1, k2)


PEER_PASS_LANES = 256


def _peer_main_kernel(h2t_ref, u_ref, vt_ref, r1_ref, f1_ref, k2r_ref, f2_ref, o_ref,
                      acc_ref, s_ref, g_ref, *, ec):
    j = pl.program_id(1)
    N = PEER_NKEYS
    tt = s_ref.shape[2]
    lw = PEER_PASS_LANES

    @pl.when(j == 0)
    def _():
        acc_ref[...] = jnp.zeros_like(acc_ref)
        s_ref[1] = jnp.zeros(s_ref.shape[1:], s_ref.dtype)

    def step(rd, wr):
        th = tt // 2
        for half in range(2):
            hs = slice(half * th, (half + 1) * th)
            s_ref[wr, :, hs] = _dot(pltpu.bitcast(u_ref[...], BF16), h2t_ref[:, hs])
            for e in range(ec):
                rows = slice(e * N, (e + 1) * N)
                for c in range(th // lw):
                    cs = slice(half * th + c * lw, half * th + (c + 1) * lw)
                    w = jnp.zeros((N // 16, 16, lw), BF16)
                    for h in range(PEER_HEADS):
                        r1 = jnp.broadcast_to(r1_ref[h, e:e + 1, cs], (16, lw)).astype(BF16)[None]
                        f1 = jnp.broadcast_to(f1_ref[h, e:e + 1, cs], (16, lw)).astype(BF16)[None]
                        k2 = k2r_ref[h, :, cs].reshape(N // 16, 16, lw)
                        f2 = f2_ref[h, :, cs].reshape(N // 16, 16, lw)
                        w = w + jnp.where(k2 < r1, f2, jnp.zeros((), BF16)) * f1
                    se = s_ref[rd, rows, cs]
                    act = se * (1.0 + lax.erf(se * (2.0 ** -0.5)))
                    g_ref[rows, cs] = act.astype(BF16) * w.reshape(N, lw)
            res = _dot(pltpu.bitcast(vt_ref[...], BF16), g_ref[:, hs])
            for c in range(th // 128):
                acc_ref[half * (th // 128) + c] += res[:, c * 128:(c + 1) * 128]

    @pl.when(j % 2 == 0)
    def _():
        step(1, 0)

    @pl.when(j % 2 == 1)
    def _():
        step(0, 1)

    @pl.when(j == pl.num_programs(1) - 1)
    def _():
        nf = acc_ref.shape[1] // 2
        for c in range(acc_ref.shape[0]):
            feats = [acc_ref[c, pl.ds(s, nf, stride=2), :] for s in range(2)]
            o_ref[c * 128:(c + 1) * 128, :] = jnp.concatenate(feats, axis=0).T


def _peer_main(h2t, u, vt, r1, f1, k2r, f2, tt, ec):
    T = h2t.shape[1]
    D, H, N = D_MODEL, PEER_HEADS, PEER_NKEYS
    assert tt % (2 * PEER_PASS_LANES) == 0, "a token tile is two halves of whole weight-build passes"
    nc = N // ec
    lag = lambda j, k: jnp.clip(j - k, 0, nc - 1)
    return pl.pallas_call(
        functools.partial(_peer_main_kernel, ec=ec),
        grid=(T // tt, nc + 1),
        in_specs=[pl.BlockSpec((D, tt), lambda i, j: (0, i)),
                  pl.BlockSpec((ec * N // 2, D), lambda i, j: (lag(j, 0), 0)),
                  pl.BlockSpec((D // 2, ec * N), lambda i, j: (0, lag(j, 1))),
                  pl.BlockSpec((H, ec, tt), lambda i, j: (0, lag(j, 1), i)),
                  pl.BlockSpec((H, ec, tt), lambda i, j: (0, lag(j, 1), i)),
                  pl.BlockSpec((H, N, tt), lambda i, j: (0, 0, i)),
                  pl.BlockSpec((H, N, tt), lambda i, j: (0, 0, i))],
        out_specs=pl.BlockSpec((tt, D), lambda i, j: (i, 0)),
        out_shape=jax.ShapeDtypeStruct((T, D), F32),
        scratch_shapes=[pltpu.VMEM((tt // 128, D, 128), F32), pltpu.VMEM((2, ec * N, tt), F32),
                        pltpu.VMEM((ec * N, tt), BF16)],
        compiler_params=_cparams(("parallel", "arbitrary")),
        name="peer_main",
    )(h2t, u, vt, r1, f1, k2r, f2)


def _final_kernel(x_ref, peer_ref, pp_ref, ps_ref, wg_ref, wp_ref, fn_ref, yp_ref, ys_ref, *, n1):
    i = pl.program_id(0)
    x = x_ref[...] + peer_ref[...]
    gate = _sigmoid(_dot(x.astype(BF16), wg_ref[...]))
    p = jnp.where(i < n1, pp_ref[...], ps_ref[...])
    x = x + gate * _dot(p.astype(BF16), wp_ref[...])
    ms = jnp.mean(x * x, axis=-1, keepdims=True)
    y = x * lax.rsqrt(ms + EPS) * fn_ref[...]

    @pl.when(i < n1)
    def _():
        yp_ref[...] = y

    @pl.when(i >= n1)
    def _():
        ys_ref[...] = y


def _final(x, peer, pp, ps, wg, wp, fn, tm):
    T = x.shape[0]
    n1 = pp.shape[0] // tm
    D = D_MODEL
    return pl.pallas_call(
        functools.partial(_final_kernel, n1=n1),
        grid=(T // tm,),
        in_specs=[pl.BlockSpec((tm, D), lambda i: (i, 0)),
                  pl.BlockSpec((tm, D), lambda i: (i, 0))] + _pair_specs((tm, PLE_DIM), n1) + [
                  pl.BlockSpec((D, D), lambda i: (0, 0)),
                  pl.BlockSpec((PLE_DIM, D), lambda i: (0, 0)),
                  pl.BlockSpec((1, D), lambda i: (0, 0))],
        out_specs=_pair_specs((tm, D), n1),
        out_shape=[jax.ShapeDtypeStruct((pp.shape[0], D), F32), jax.ShapeDtypeStruct((ps.shape[0], D), F32)],
        compiler_params=_cparams(("arbitrary",)),
        name="final",
    )(x, peer, pp, ps, wg, wp, fn)


def _rot_cols(w):
    half = MLA_ROPE // 2
    return jnp.concatenate([-w[..., half:], w[..., :half]], axis=-1)


def _pack_w_in(w_in):
    pts, acc = [], 0
    for sz in (512, 512, 1024, 1024, 2 * GLA_RANK, MLA_Q_LORA, MLA_KV_LORA, MLA_ROPE):
        acc += sz
        pts.append(acc)
    q, k, v, r, a_lr, c_q, c_kv, k_r, gate = jnp.split(w_in, pts, axis=-1)
    z = lambda n: jnp.zeros((D_MODEL, n), w_in.dtype)
    packed = jnp.concatenate([gate, v, r, q, k, c_q, z(CQ_PAD - MLA_Q_LORA), c_kv, k_r, _rot_cols(k_r),
                              a_lr, z(128 - 2 * GLA_RANK)], axis=-1)
    assert packed.shape[-1] == P_COLS
    return packed.astype(BF16)


def _pack_w_uq(w_uq):
    w = w_uq.reshape(MLA_Q_LORA, MLA_HEADS, MLA_DQK)
    pe = w[..., MLA_NOPE:]
    w = jnp.concatenate([w, _rot_cols(pe)], axis=-1).reshape(MLA_Q_LORA, MLA_HEADS * 256)
    return jnp.concatenate([w, jnp.zeros((CQ_PAD - MLA_Q_LORA, MLA_HEADS * 256), w.dtype)], axis=0).astype(BF16)


def _interleave_keys(w):
    lead, c = w.shape[:-2], w.shape[-1]
    half = PEER_NKEYS // 2
    return jnp.swapaxes(w.reshape(*lead, 2, half, c), -3, -2).reshape(*lead, PEER_NKEYS, c)


def _pack_u(u):
    half = PEER_NKEYS // 2
    h = lax.bitcast_convert_type(u.astype(BF16).reshape(PEER_NKEYS, 2, half, D_MODEL), jnp.uint16).astype(jnp.uint32)
    return ((h[:, 1] << 16) | h[:, 0]).reshape(PEER_N // 2, D_MODEL)


def _pack_vt(v):
    vt = _interleave_keys(v.reshape(PEER_NKEYS, PEER_NKEYS, D_MODEL)).reshape(PEER_N, D_MODEL).astype(BF16).T
    h = lax.bitcast_convert_type(vt.reshape(2, D_MODEL // 2, PEER_N), jnp.uint16).astype(jnp.uint32)
    return (h[1] << 16) | h[0]


def _tile(n, pref):
    return pref if n % pref == 0 else n


class _Tiles(NamedTuple):
    in_proj_rows: int
    gla_rows: int
    mla_prep_rows: int
    attn_q: int
    attn_kv: int
    merge_rows: int
    topk_tokens: int
    peer_tokens: int
    final_rows: int


IN_PROJ_COLS = 1024
PEER_CHUNK_KEYS = 8


def _tiles(T1, T2, S):
    T = T1 + T2
    return _Tiles(in_proj_rows=_pair_tile(T1, T2, 1024), gla_rows=_tile(S, 256), mla_prep_rows=_tile(S, 512),
                  attn_q=_tile(S, 1024), attn_kv=_tile(S, 1024), merge_rows=_pair_tile(T1, T2, 256),
                  topk_tokens=_tile(T, 256), peer_tokens=_tile(T, 1024), final_rows=_pair_tile(T1, T2, 512))


def kernel(x_prompt, x_sample, p_prompt, p_sample, norm1, w_in, gla_a_w_f, gla_a_b_f, gla_a_w_b, gla_a_b_b, gla_norm, gla_w_o, mla_q_norm, mla_w_uq, mla_kv_norm, mla_w_ukv, mla_w_o, w_out, norm2, peer_w_q, peer_k1, peer_k2, peer_u, peer_v, ple_proj, ple_gate, final_norm):
    B1, S, D = x_prompt.shape
    B2 = x_sample.shape[0]
    assert x_sample.shape[1] == S and norm1.shape[0] == 1, "one layer, equal sequence lengths"
    Bt = B1 + B2
    T = Bt * S
    T1, T2 = B1 * S, B2 * S
    xp, xs = x_prompt.reshape(T1, D), x_sample.reshape(T2, D)
    row = lambda v: v.reshape(1, -1).astype(F32)
    tiles = _tiles(T1, T2, S)

    P = _in_proj(xp, xs, row(norm1[0]), _pack_w_in(w_in[0]), tiles.in_proj_rows, IN_PROJ_COLS)

    o_f, o_b = _gla(P, gla_a_w_f[0], row(gla_a_b_f[0]), gla_a_w_b[0], row(gla_a_b_b[0]), Bt, S, tiles.gla_rows)

    pos = jnp.arange(S, dtype=F32)
    inv = ROPE_THETA ** (-jnp.arange(MLA_ROPE // 2, dtype=F32) * 2.0 / MLA_ROPE)
    ang = pos[:, None] * inv[None, :]
    cos2 = jnp.concatenate([jnp.cos(ang), jnp.cos(ang)], axis=-1)
    sin2 = jnp.concatenate([jnp.sin(ang), jnp.sin(ang)], axis=-1)
    qn = jnp.concatenate([mla_q_norm[0], jnp.zeros((CQ_PAD - MLA_Q_LORA,), F32)]).reshape(1, CQ_PAD)
    Q, K, V = _mla_prep(P, cos2, sin2, qn, _pack_w_uq(mla_w_uq[0]), row(mla_kv_norm[0]),
                        mla_w_ukv[0].astype(BF16), Bt, S, tiles.mla_prep_rows)
    o_att = _attention(Q, K, V, tiles.attn_q, tiles.attn_kv).reshape(T, MLA_HEADS * MLA_V)

    gn = jnp.tile(gla_norm[0], GLA_HEADS).reshape(1, D)
    x1, h2t, qp = _merge(xp, xs, o_f, o_b, P, o_att, gn, gla_w_o[0].astype(BF16), mla_w_o[0].astype(BF16),
                        w_out[0].astype(BF16), row(norm2[0]), peer_w_q[0].astype(BF16), tiles.merge_rows)

    r1, f1, k2r, f2 = _peer_topk(qp, peer_k1[0].astype(BF16), _interleave_keys(peer_k2[0]).astype(BF16),
                                 tiles.topk_tokens)
    peer = _peer_main(h2t, _pack_u(peer_u[0]), _pack_vt(peer_v[0]), r1, f1, k2r, f2,
                      tiles.peer_tokens, PEER_CHUNK_KEYS)

    yp, ys = _final(x1, peer, p_prompt[0].reshape(T1, PLE_DIM), p_sample[0].reshape(T2, PLE_DIM),
                    ple_gate[0].astype(BF16), ple_proj[0].astype(BF16), row(final_norm), tiles.final_rows)
    return (yp.reshape(B1, S, D), ys.reshape(B2, S, D))
```

```python
import functools
from typing import NamedTuple

import jax
import jax.numpy as jnp
from jax import lax
from jax.experimental import pallas as pl
from jax.experimental.pallas import tpu as pltpu

F32 = jnp.float32
BF16 = jnp.bfloat16
EPS = 1e-6

D_MODEL = 1024
PLE_DIM = 256
GLA_HEADS, GLA_DK, GLA_DV, GLA_RANK, GLA_GATE_NORM, GLA_CHUNK = 4, 128, 256, 16, 16.0, 64
MLA_HEADS, MLA_Q_LORA, MLA_KV_LORA, MLA_NOPE, MLA_ROPE, MLA_V = 8, 384, 256, 128, 64, 128
MLA_DQK = MLA_NOPE + MLA_ROPE
MLA_QK_PAD = 256
ROPE_THETA = 10000.0
LOG2E = 1.4426950408889634
PEER_HEADS, PEER_NKEYS, PEER_HALF, PEER_TOPK = 8, 128, 128, 16
PEER_N = PEER_NKEYS * PEER_NKEYS

P_COLS = 6144
COL_GATE, COL_V, COL_R, COL_Q, COL_K, COL_CQ, COL_CKV, COL_KR, COL_ALR = (
    0, 2048, 3072, 4096, 4608, 5120, 5632, 5888, 6016)
CQ_PAD = 512

VMEM_LIMIT = 56 * 1024 * 1024


def _cparams(sem):
    return pltpu.CompilerParams(dimension_semantics=sem, vmem_limit_bytes=VMEM_LIMIT)


def _dot(a, b):
    return jnp.dot(a, b, preferred_element_type=F32)


def _dot_nt(a, b):
    return lax.dot_general(a, b, (((1,), (1,)), ((), ())), preferred_element_type=F32)


def _dot_tn(a, b):
    return lax.dot_general(a, b, (((0,), (0,)), ((), ())), preferred_element_type=F32)


def _sigmoid(x):
    return 1.0 / (1.0 + jnp.exp(-x))


def _pair_specs(block, n1):
    return [pl.BlockSpec(block, lambda i, *_: (jnp.minimum(i, n1 - 1), 0)),
            pl.BlockSpec(block, lambda i, *_: (jnp.maximum(i - n1, 0), 0))]


def _pair_tile(T1, T2, pref):
    t = pref
    while T1 % t or T2 % t:
        t //= 2
    return t


def _in_proj_kernel(xp_ref, xs_ref, g_ref, w_ref, o_ref, h_ref, *, n1):
    @pl.when(pl.program_id(1) == 0)
    def _():
        x = jnp.where(pl.program_id(0) < n1, xp_ref[...], xs_ref[...])
        ms = jnp.mean(x * x, axis=-1, keepdims=True)
        h_ref[...] = (x * lax.rsqrt(ms + EPS) * g_ref[...]).astype(BF16)

    o_ref[...] = _dot(h_ref[...], w_ref[...]).astype(o_ref.dtype)


def _in_proj(xp, xs, g, w, tm, tn):
    n1, n2 = xp.shape[0] // tm, xs.shape[0] // tm
    T = xp.shape[0] + xs.shape[0]
    return pl.pallas_call(
        functools.partial(_in_proj_kernel, n1=n1),
        grid=(n1 + n2, P_COLS // tn),
        in_specs=_pair_specs((tm, D_MODEL), n1) + [
            pl.BlockSpec((1, D_MODEL), lambda i, j: (0, 0)),
            pl.BlockSpec((D_MODEL, tn), lambda i, j: (0, j))],
        out_specs=pl.BlockSpec((tm, tn), lambda i, j: (i, j)),
        out_shape=jax.ShapeDtypeStruct((T, P_COLS), BF16),
        scratch_shapes=[pltpu.VMEM((tm, D_MODEL), BF16)],
        compiler_params=_cparams(("parallel", "arbitrary")),
        name="in_proj",
    )(xp, xs, g, w)


def _gla_direction(q_ref, k_ref, v_ref, a_ref, aw_ref, ab_ref, st_ref, o_ref, a_off, backward, R):
    C = GLA_CHUNK
    nc = R // C
    row = lax.broadcasted_iota(jnp.int32, (R, R), 0)
    col = lax.broadcasted_iota(jnp.int32, (R, R), 1)
    same = (row // C) == (col // C)
    cum = same & ((col >= row) if backward else (col <= row))
    att_mask = same & ((col > row) if backward else (col <= row))
    tri = jnp.where(cum, 1.0, 0.0).astype(BF16)
    blk = jnp.where(same, 1.0, 0.0).astype(BF16)
    a = a_ref[:, a_off:a_off + GLA_RANK].astype(BF16)
    order = range(nc - 1, -1, -1) if backward else range(nc)
    for h in range(GLA_HEADS):
        ks = slice(h * GLA_DK, (h + 1) * GLA_DK)
        vs = slice(h * GLA_DV, (h + 1) * GLA_DV)
        z = _dot(a, aw_ref[:, ks].astype(BF16)) + ab_ref[:, ks]
        la = (jnp.minimum(z, 0.0) - jnp.log(1.0 + jnp.exp(-jnp.abs(z)))) / GLA_GATE_NORM
        la_hi = la.astype(BF16)
        la_lo = (la - la_hi.astype(F32)).astype(BF16)
        b = _dot(tri, la_hi) + _dot(tri, la_lo)
        tot = _dot(blk, la_hi) + _dot(blk, la_lo)
        qh = q_ref[:, ks].astype(F32) * (GLA_DK ** -0.5)
        kh = k_ref[:, ks].astype(F32)
        vh = v_ref[:, vs].astype(BF16)
        q_e = (qh * jnp.exp(b)).astype(BF16)
        k_e = (kh * jnp.exp(-b)).astype(BF16)
        k_s = (kh * jnp.exp(tot - b)).astype(BF16)
        dec = jnp.exp(tot)
        att = jnp.where(att_mask, _dot_nt(q_e, k_e), 0.0).astype(BF16)
        o_intra = _dot(att, vh)
        for c in order:
            rs = slice(c * C, (c + 1) * C)
            st = st_ref[h]
            o_ref[rs, vs] = o_intra[rs] + _dot_nt(q_e[rs], st.astype(BF16))
            st_ref[h] = dec[c * C:c * C + 1, :] * st + _dot_tn(vh[rs], k_s[rs])


def _gla_kernel(qf_ref, kf_ref, vf_ref, af_ref, qb_ref, kb_ref, vb_ref, ab_ref,
                awf_ref, abf_ref, awb_ref, abb_ref, of_ref, ob_ref, stf_ref, stb_ref, *, R):
    @pl.when(pl.program_id(1) == 0)
    def _():
        stf_ref[...] = jnp.zeros_like(stf_ref)
        stb_ref[...] = jnp.zeros_like(stb_ref)

    _gla_direction(qf_ref, kf_ref, vf_ref, af_ref, awf_ref, abf_ref, stf_ref, of_ref, 0, False, R)
    _gla_direction(qb_ref, kb_ref, vb_ref, ab_ref, awb_ref, abb_ref, stb_ref, ob_ref, GLA_RANK, True, R)


def _gla(P, awf, abf, awb, abb, Bt, S, R):
    T = Bt * S
    NB = S // R
    HK, HV = GLA_HEADS * GLA_DK, GLA_HEADS * GLA_DV
    fwd = lambda b, n: b * NB + n
    bwd = lambda b, n: b * NB + (NB - 1 - n)

    def specs(rowf):
        return [pl.BlockSpec((R, HK), lambda b, n: (rowf(b, n), COL_Q // HK)),
                pl.BlockSpec((R, HK), lambda b, n: (rowf(b, n), COL_K // HK)),
                pl.BlockSpec((R, HV), lambda b, n: (rowf(b, n), COL_V // HV)),
                pl.BlockSpec((R, 128), lambda b, n: (rowf(b, n), COL_ALR // 128))]

    wspec = lambda shape: pl.BlockSpec(shape, lambda b, n: (0, 0))
    return pl.pallas_call(
        functools.partial(_gla_kernel, R=R),
        grid=(Bt, NB),
        in_specs=specs(fwd) + specs(bwd) + [wspec((GLA_RANK, HK)), wspec((1, HK)),
                                            wspec((GLA_RANK, HK)), wspec((1, HK))],
        out_specs=[pl.BlockSpec((R, HV), lambda b, n: (fwd(b, n), 0)),
                   pl.BlockSpec((R, HV), lambda b, n: (bwd(b, n), 0))],
        out_shape=[jax.ShapeDtypeStruct((T, HV), F32), jax.ShapeDtypeStruct((T, HV), F32)],
        scratch_shapes=[pltpu.VMEM((GLA_HEADS, GLA_DV, GLA_DK), F32),
                        pltpu.VMEM((GLA_HEADS, GLA_DV, GLA_DK), F32)],
        compiler_params=_cparams(("parallel", "arbitrary")),
        name="gla",
    )(P, P, P, P, P, P, P, P, awf, abf, awb, abb)


def _mla_prep_kernel(cq_ref, ckv_ref, kr_ref, cos_ref, sin_ref, qn_ref, wuq_ref, kvn_ref, wukv_ref,
                     q_out, k_out, v_out):
    cq = cq_ref[...].astype(F32)
    ms = jnp.sum(cq * cq, axis=-1, keepdims=True) * (1.0 / MLA_Q_LORA)
    hq = (cq * lax.rsqrt(ms + EPS) * qn_ref[...]).astype(BF16)
    qq = _dot(hq, wuq_ref[...])
    ckv = ckv_ref[...].astype(F32)
    ms = jnp.mean(ckv * ckv, axis=-1, keepdims=True)
    hkv = (ckv * lax.rsqrt(ms + EPS) * kvn_ref[...]).astype(BF16)
    kv = _dot(hkv, wukv_ref[...])
    cos = cos_ref[...]
    sin = sin_ref[...]
    kr = kr_ref[...].astype(F32)
    k_pe = (kr[:, :MLA_ROPE] * cos + kr[:, MLA_ROPE:] * sin).astype(BF16)
    scale = MLA_DQK ** -0.5 * LOG2E
    zpad = jnp.zeros((cq.shape[0], MLA_QK_PAD - MLA_DQK), BF16)
    for h in range(MLA_HEADS):
        o = h * 256
        q_pe = qq[:, o + 128:o + 192] * cos + qq[:, o + 192:o + 256] * sin
        q_out[0, h, :, 0:MLA_NOPE] = (qq[:, o:o + MLA_NOPE] * scale).astype(BF16)
        q_out[0, h, :, MLA_NOPE:MLA_DQK] = (q_pe * scale).astype(BF16)
        q_out[0, h, :, MLA_DQK:MLA_QK_PAD] = zpad
        k_out[0, h, :, 0:MLA_NOPE] = kv[:, o:o + MLA_NOPE].astype(BF16)
        k_out[0, h, :, MLA_NOPE:MLA_DQK] = k_pe
        k_out[0, h, :, MLA_DQK:MLA_QK_PAD] = zpad
        v_out[0, h] = kv[:, o + MLA_NOPE:o + 256].astype(BF16)


def _mla_prep(P, cos2, sin2, qn, wuq, kvn, wukv, Bt, S, ts):
    NS = S // ts
    row = lambda b, i: b * NS + i
    wspec = lambda shape: pl.BlockSpec(shape, lambda b, i: (0, 0))
    H = MLA_HEADS
    return pl.pallas_call(
        _mla_prep_kernel,
        grid=(Bt, NS),
        in_specs=[pl.BlockSpec((ts, CQ_PAD), lambda b, i: (row(b, i), COL_CQ // CQ_PAD)),
                  pl.BlockSpec((ts, MLA_KV_LORA), lambda b, i: (row(b, i), COL_CKV // MLA_KV_LORA)),
                  pl.BlockSpec((ts, 128), lambda b, i: (row(b, i), COL_KR // 128)),
                  pl.BlockSpec((ts, MLA_ROPE), lambda b, i: (i, 0)),
                  pl.BlockSpec((ts, MLA_ROPE), lambda b, i: (i, 0)),
                  wspec((1, CQ_PAD)), wspec((CQ_PAD, H * 256)),
                  wspec((1, MLA_KV_LORA)), wspec((MLA_KV_LORA, H * 256))],
        out_specs=[pl.BlockSpec((1, H, ts, MLA_QK_PAD), lambda b, i: (b, 0, i, 0)),
                   pl.BlockSpec((1, H, ts, MLA_QK_PAD), lambda b, i: (b, 0, i, 0)),
                   pl.BlockSpec((1, H, ts, MLA_V), lambda b, i: (b, 0, i, 0))],
        out_shape=[jax.ShapeDtypeStruct((Bt, H, S, MLA_QK_PAD), BF16),
                   jax.ShapeDtypeStruct((Bt, H, S, MLA_QK_PAD), BF16),
                   jax.ShapeDtypeStruct((Bt, H, S, MLA_V), BF16)],
        compiler_params=_cparams(("parallel", "parallel")),
        name="mla_prep",
    )(P, P, P, cos2, sin2, qn, wuq, kvn, wukv)


ATTN_HEADS_PER_STEP = 2


def _attn_kernel(q_ref, k_ref, v_ref, o_ref, m_ref, l_ref, acc_ref):
    kv = pl.program_id(3)
    nl = k_ref.shape[2] // 128

    @pl.when(kv == 0)
    def _():
        m_ref[...] = jnp.full_like(m_ref, -jnp.inf)
        l_ref[...] = jnp.zeros_like(l_ref)
        acc_ref[...] = jnp.zeros_like(acc_ref)

    for h in range(ATTN_HEADS_PER_STEP):
        s = _dot_nt(q_ref[0, h], k_ref[0, h])
        m_prev = m_ref[h]
        m_new = jnp.maximum(m_prev, jnp.max(s, axis=-1, keepdims=True))
        alpha = jnp.exp2(m_prev - m_new)
        ps = [jnp.exp2(s[:, c * 128:(c + 1) * 128] - m_new) for c in range(nl)]
        psum = ps[0]
        for c in range(1, nl):
            psum = psum + ps[c]
        p = jnp.concatenate([x.astype(BF16) for x in ps], axis=-1)
        l_ref[h] = alpha * l_ref[h] + psum
        acc_ref[h] = alpha * acc_ref[h] + _dot(p, v_ref[0, h])
        m_ref[h] = m_new

    @pl.when(kv == pl.num_programs(3) - 1)
    def _():
        for h in range(ATTN_HEADS_PER_STEP):
            l = jnp.sum(l_ref[h], axis=-1, keepdims=True)
            o_ref[0, :, h * MLA_V:(h + 1) * MLA_V] = (acc_ref[h] / l).astype(o_ref.dtype)


def _attention(Q, K, V, tq, tk):
    Bt, H, S, _ = Q.shape
    G = ATTN_HEADS_PER_STEP
    return pl.pallas_call(
        _attn_kernel,
        grid=(Bt, H // G, S // tq, S // tk),
        in_specs=[pl.BlockSpec((1, G, tq, MLA_QK_PAD), lambda b, h, i, j: (b, h, i, 0)),
                  pl.BlockSpec((1, G, tk, MLA_QK_PAD), lambda b, h, i, j: (b, h, j, 0)),
                  pl.BlockSpec((1, G, tk, MLA_V), lambda b, h, i, j: (b, h, j, 0))],
        out_specs=pl.BlockSpec((1, tq, G * MLA_V), lambda b, h, i, j: (b, i, h)),
        out_shape=jax.ShapeDtypeStruct((Bt, S, H * MLA_V), BF16),
        scratch_shapes=[pltpu.VMEM((G, tq, 128), F32), pltpu.VMEM((G, tq, 128), F32),
                        pltpu.VMEM((G, tq, MLA_V), F32)],
        compiler_params=_cparams(("parallel", "parallel", "parallel", "arbitrary")),
        name="attn",
    )(Q, K, V)


def _merge_kernel(xp_ref, xs_ref, of_ref, ob_ref, r_ref, gate_ref, oat_ref, gn_ref, wgo_ref, wmo_ref, wout_ref,
                  n2_ref, wq_ref, x1_ref, h2t_ref, qp_ref, *, n1):
    o = of_ref[...] + ob_ref[...]
    gn = gn_ref[...]
    parts = []
    for h in range(GLA_HEADS):
        vs = slice(h * GLA_DV, (h + 1) * GLA_DV)
        oh = o[:, vs]
        ms = jnp.mean(oh * oh, axis=-1, keepdims=True)
        parts.append(oh * lax.rsqrt(ms + EPS) * gn[:, vs])
    on = jnp.concatenate(parts, axis=-1)
    r = r_ref[...].astype(F32)
    ya = _dot((on * (r * _sigmoid(r))).astype(BF16), wgo_ref[...])
    yb = _dot(oat_ref[...], wmo_ref[...])
    g = _sigmoid(gate_ref[...].astype(F32))
    mix = g[:, :D_MODEL] * ya + g[:, D_MODEL:] * yb
    x = jnp.where(pl.program_id(0) < n1, xp_ref[...], xs_ref[...])
    x1 = x + _dot(mix.astype(BF16), wout_ref[...])
    x1_ref[...] = x1
    ms = jnp.mean(x1 * x1, axis=-1, keepdims=True)
    h2 = x1 * lax.rsqrt(ms + EPS) * n2_ref[...]
    h2t_ref[...] = h2.T.astype(BF16)
    qp_ref[...] = _dot(h2.astype(BF16), wq_ref[...]).astype(BF16)


def _merge(xp, xs, o_f, o_b, P, o_att, gn, wgo, wmo, wout, n2, wq, tm):
    T = xp.shape[0] + xs.shape[0]
    n1 = xp.shape[0] // tm
    D = D_MODEL
    QW = PEER_HEADS * 2 * PEER_HALF
    tile = lambda w, c=0: pl.BlockSpec((tm, w), lambda i: (i, c))
    wspec = lambda shape: pl.BlockSpec(shape, lambda i: (0, 0))
    return pl.pallas_call(
        functools.partial(_merge_kernel, n1=n1),
        grid=(T // tm,),
        in_specs=_pair_specs((tm, D), n1) + [
            tile(D), tile(D), tile(D, COL_R // D), tile(2 * D, COL_GATE // (2 * D)), tile(D),
            wspec((1, D)), wspec((D, D)), wspec((D, D)), wspec((D, D)), wspec((1, D)), wspec((D, QW))],
        out_specs=[tile(D), pl.BlockSpec((D, tm), lambda i: (0, i)), tile(QW)],
        out_shape=[jax.ShapeDtypeStruct((T, D), F32), jax.ShapeDtypeStruct((D, T), BF16),
                   jax.ShapeDtypeStruct((T, QW), BF16)],
        compiler_params=_cparams(("parallel",)),
        name="merge",
    )(xp, xs, o_f, o_b, P, P, o_att, gn, wgo, wmo, wout, n2, wq)


def _extract16(cur, tie_index):
    n, t = cur.shape
    slot = lax.broadcasted_iota(jnp.int32, (PEER_TOPK, t), 0)
    vals = jnp.zeros((PEER_TOPK, t), F32)
    rank = jnp.full((n, t), float(PEER_TOPK), F32)
    for a in range(PEER_TOPK):
        m = jnp.max(cur, axis=0, keepdims=True)
        sel = cur == m
        if tie_index is not None:
            first = jnp.min(jnp.where(sel, tie_index, float(n * n)), axis=0, keepdims=True)
            sel = tie_index == first
        vals = jnp.where(slot == a, m, vals)
        rank = jnp.where(sel, float(a), rank)
        cur = jnp.where(sel, -jnp.inf, cur)
    return vals, rank


def _taken_error(rank):
    taken = jnp.sum(jnp.where(rank < float(PEER_TOPK), 1.0, 0.0), axis=0, keepdims=True)
    return jnp.abs(taken - float(PEER_TOPK))


_CAND_AB = [
    [(0, b) for b in range(8)],
    [(0, b) for b in range(8, 16)],
    [(1, b) for b in range(8)],
    [(2, b) for b in range(5)] + [(4, b) for b in range(3)],
    [(3, b) for b in range(4)] + [(5, 0), (5, 1), (6, 0), (6, 1)],
    [(7, 0), (7, 1)] + [(a, 0) for a in range(8, 14)],
    [(14, 0), (15, 0)] + [None] * 6,
]
assert sorted(p for t in _CAND_AB for p in t if p) == sorted(
    (a, b) for a in range(PEER_TOPK) for b in range(PEER_TOPK) if (a + 1) * (b + 1) <= PEER_TOPK)


def _candidate_tiles(t1, t2):
    tt = t1.shape[1]
    row = lax.broadcasted_iota(jnp.int32, (8, tt), 0)
    lo1, hi1, lo2, hi2 = t1[0:8], t1[8:16], t2[0:8], t2[8:16]
    r1 = lambda a: t1[a:a + 1, :]
    up = lambda x, k: pltpu.roll(x, k, 0)
    hi1r = up(hi1, 2)
    b0 = t2[0:1, :]
    tiles = [
        r1(0) + lo2,
        r1(0) + hi2,
        r1(1) + lo2,
        jnp.where(row < 5, r1(2) + lo2, r1(4) + up(lo2, 5)),
        jnp.where(row < 4, r1(3) + lo2, jnp.where(row < 6, r1(5) + up(lo2, 4), r1(6) + up(lo2, 6))),
        jnp.where(row < 2, r1(7) + lo2, hi1r + b0),
        jnp.where(row < 2, hi1r + b0, -jnp.inf),
    ]
    return jnp.concatenate(tiles, axis=0)


def _row_constants(values, tt):
    n = len(values)
    row = lax.broadcasted_iota(jnp.int32, (n, tt), 0)
    out = jnp.full((n, tt), float(values[-1]), F32)
    for r in range(n - 1):
        out = jnp.where(row == r, float(values[r]), out)
    return out


def _peer_topk_kernel(qp_ref, k1_ref, k2_ref, r1_ref, f1_ref, k2r_ref, f2_ref, s_sc, t_sc, rk_sc, cnt_sc, z_sc):
    K, H = PEER_TOPK, PEER_HEADS
    tt = qp_ref.shape[0]
    k1 = k1_ref[...]
    k2 = k2_ref[...]
    for h in range(H):
        o = h * 2 * PEER_HALF
        s_sc[2 * h] = _dot_nt(k1, qp_ref[:, o:o + PEER_HALF])
        s_sc[2 * h + 1] = _dot_nt(k2, qp_ref[:, o + PEER_HALF:o + 2 * PEER_HALF])

    def stage1(exact):
        err = jnp.zeros((1, tt), F32)
        if exact:
            pos = lax.broadcasted_iota(jnp.int32, (PEER_NKEYS, tt), 0)
            key_index = [pos.astype(F32), ((pos % 2) * (PEER_NKEYS // 2) + pos // 2).astype(F32)]
        for i in range(2 * H):
            vals, rank = _extract16(s_sc[i], key_index[i % 2] if exact else None)
            t_sc[i] = vals
            rk_sc[i] = rank
            if not exact:
                err = jnp.maximum(err, _taken_error(rank))
        return err

    err1 = stage1(False)

    @pl.when(jnp.max(err1) > 0.0)
    def _():
        stage1(True)

    pairs = [p for tile in _CAND_AB for p in tile]
    slot = lax.broadcasted_iota(jnp.int32, (K, tt), 0)

    def stage2(exact):
        err = jnp.zeros((1, tt), F32)
        flat = _row_constants([K * K if p is None else p[0] * K + p[1] for p in pairs], tt) if exact else None
        for h in range(H):
            cand = _candidate_tiles(t_sc[2 * h], t_sc[2 * h + 1])
            _, rank = _extract16(cand, flat)
            chosen = rank < float(K)
            ch = jnp.where(chosen, 1.0, 0.0)
            z = jnp.sum(jnp.where(chosen, jnp.exp(cand - cand[0:1, :]), 0.0), axis=0, keepdims=True)
            z_sc[h] = jnp.broadcast_to(z, (8, tt))
            cnt = jnp.zeros((K, tt), F32)
            for a in range(K):
                rows = [r for r, p in enumerate(pairs) if p is not None and p[0] == a]
                ca = jnp.sum(ch[rows[0]:rows[-1] + 1, :], axis=0, keepdims=True)
                cnt = jnp.where(slot == a, ca, cnt)
            cnt_sc[h] = cnt
            if not exact:
                err = jnp.maximum(err, _taken_error(rank))
        return err

    err2 = stage2(False)

    @pl.when(jnp.max(err2) > 0.0)
    def _():
        stage2(True)

    for h in range(H):
        s1, s2 = s_sc[2 * h], s_sc[2 * h + 1]
        t1, t2 = t_sc[2 * h], t_sc[2 * h + 1]
        rank1 = rk_sc[2 * h]
        cnt = cnt_sc[h]
        r1 = jnp.zeros(rank1.shape, F32)
        for a in range(K):
            r1 = jnp.where(rank1 == float(a), cnt[a:a + 1, :], r1)
        r1_ref[h] = r1
        f1_ref[h] = jnp.exp(s1 - t1[0:1, :]) * (0.5 / z_sc[h, 0:1, :])
        k2r_ref[h] = rk_sc[2 * h + 1].astype(BF16)
        f2_ref[h] = jnp.exp(s2 - t2[0:1, :]).astype(BF16)


def _peer_topk(qp, k1, k2, tt):
    T = qp.shape[0]
    H, N = PEER_HEADS, PEER_NKEYS
    out = jax.ShapeDtypeStruct((H, N, T), F32)
    outb = jax.ShapeDtypeStruct((H, N, T), BF16)
    ospec = pl.BlockSpec((H, N, tt), lambda i: (0, 0, i))
    return pl.pallas_call(
        _peer_topk_kernel,
        grid=(T // tt,),
        in_specs=[pl.BlockSpec((tt, H * 2 * PEER_HALF), lambda i: (i, 0)),
                  pl.BlockSpec((N, PEER_HALF), lambda i: (0, 0)),
                  pl.BlockSpec((N, PEER_HALF), lambda i: (0, 0))],
        out_specs=[ospec, ospec, ospec, ospec],
        out_shape=[out, out, outb, outb],
        scratch_shapes=[pltpu.VMEM((2 * H, N, tt), F32), pltpu.VMEM((2 * H, PEER_TOPK, tt), F32),
                        pltpu.VMEM((2 * H, N, tt), F32), pltpu.VMEM((H, PEER_TOPK, tt), F32),
                        pltpu.VMEM((H, 8, tt), F32)],
        compiler_params=_cparams(("parallel",)),
        name="peer_topk",
    )(qp, k1, k2)


PEER_PASS_LANES = 256


def _peer_main_kernel(h2t_ref, u_ref, vt_ref, r1_ref, f1_ref, k2r_ref, f2_ref, o_ref,
                      acc_ref, s_ref, g_ref, *, ec):
    j = pl.program_id(1)
    N = PEER_NKEYS
    tt = s_ref.shape[2]
    lw = PEER_PASS_LANES

    @pl.when(j == 0)
    def _():
        acc_ref[...] = jnp.zeros_like(acc_ref)
        s_ref[1] = jnp.zeros(s_ref.shape[1:], s_ref.dtype)

    def step(rd, wr):
        th = tt // 2
        for half in range(2):
            hs = slice(half * th, (half + 1) * th)
            s_ref[wr, :, hs] = _dot(pltpu.bitcast(u_ref[...], BF16), h2t_ref[:, hs])
            for e in range(ec):
                rows = slice(e * N, (e + 1) * N)
                for c in range(th // lw):
                    cs = slice(half * th + c * lw, half * th + (c + 1) * lw)
                    w = jnp.zeros((N // 16, 16, lw), BF16)
                    for h in range(PEER_HEADS):
                        r1 = jnp.broadcast_to(r1_ref[h, e:e + 1, cs], (16, lw)).astype(BF16)[None]
                        f1 = jnp.broadcast_to(f1_ref[h, e:e + 1, cs], (16, lw)).astype(BF16)[None]
                        k2 = k2r_ref[h, :, cs].reshape(N // 16, 16, lw)
                        f2 = f2_ref[h, :, cs].reshape(N // 16, 16, lw)
                        w = w + jnp.where(k2 < r1, f2, jnp.zeros((), BF16)) * f1
                    se = s_ref[rd, rows, cs]
                    act = se * (1.0 + lax.erf(se * (2.0 ** -0.5)))
                    g_ref[rows, cs] = act.astype(BF16) * w.reshape(N, lw)
            res = _dot(pltpu.bitcast(vt_ref[...], BF16), g_ref[:, hs])
            for c in range(th // 128):
                acc_ref[half * (th // 128) + c] += res[:, c * 128:(c + 1) * 128]

    @pl.when(j % 2 == 0)
    def _():
        step(1, 0)

    @pl.when(j % 2 == 1)
    def _():
        step(0, 1)

    @pl.when(j == pl.num_programs(1) - 1)
    def _():
        nf = acc_ref.shape[1] // 2
        for c in range(acc_ref.shape[0]):
            feats = [acc_ref[c, pl.ds(s, nf, stride=2), :] for s in range(2)]
            o_ref[c * 128:(c + 1) * 128, :] = jnp.concatenate(feats, axis=0).T


def _peer_main(h2t, u, vt, r1, f1, k2r, f2, tt, ec):
    T = h2t.shape[1]
    D, H, N = D_MODEL, PEER_HEADS, PEER_NKEYS
    assert tt % (2 * PEER_PASS_LANES) == 0, "a token tile is two halves of whole weight-build passes"
    nc = N // ec
    lag = lambda j, k: jnp.clip(j - k, 0, nc - 1)
    return pl.pallas_call(
        functools.partial(_peer_main_kernel, ec=ec),
        grid=(T // tt, nc + 1),
        in_specs=[pl.BlockSpec((D, tt), lambda i, j: (0, i)),
                  pl.BlockSpec((ec * N // 2, D), lambda i, j: (lag(j, 0), 0)),
                  pl.BlockSpec((D // 2, ec * N), lambda i, j: (0, lag(j, 1))),
                  pl.BlockSpec((H, ec, tt), lambda i, j: (0, lag(j, 1), i)),
                  pl.BlockSpec((H, ec, tt), lambda i, j: (0, lag(j, 1), i)),
                  pl.BlockSpec((H, N, tt), lambda i, j: (0, 0, i)),
                  pl.BlockSpec((H, N, tt), lambda i, j: (0, 0, i))],
        out_specs=pl.BlockSpec((tt, D), lambda i, j: (i, 0)),
        out_shape=jax.ShapeDtypeStruct((T, D), F32),
        scratch_shapes=[pltpu.VMEM((tt // 128, D, 128), F32), pltpu.VMEM((2, ec * N, tt), F32),
                        pltpu.VMEM((ec * N, tt), BF16)],
        compiler_params=_cparams(("parallel", "arbitrary")),
        name="peer_main",
    )(h2t, u, vt, r1, f1, k2r, f2)


def _final_kernel(x_ref, peer_ref, pp_ref, ps_ref, wg_ref, wp_ref, fn_ref, yp_ref, ys_ref, *, n1):
    i = pl.program_id(0)
    x = x_ref[...] + peer_ref[...]
    gate = _sigmoid(_dot(x.astype(BF16), wg_ref[...]))
    p = jnp.where(i < n1, pp_ref[...], ps_ref[...])
    x = x + gate * _dot(p.astype(BF16), wp_ref[...])
    ms = jnp.mean(x * x, axis=-1, keepdims=True)
    y = x * lax.rsqrt(ms + EPS) * fn_ref[...]

    @pl.when(i < n1)
    def _():
        yp_ref[...] = y

    @pl.when(i >= n1)
    def _():
        ys_ref[...] = y


def _final(x, peer, pp, ps, wg, wp, fn, tm):
    T = x.shape[0]
    n1 = pp.shape[0] // tm
    D = D_MODEL
    return pl.pallas_call(
        functools.partial(_final_kernel, n1=n1),
        grid=(T // tm,),
        in_specs=[pl.BlockSpec((tm, D), lambda i: (i, 0)),
                  pl.BlockSpec((tm, D), lambda i: (i, 0))] + _pair_specs((tm, PLE_DIM), n1) + [
                  pl.BlockSpec((D, D), lambda i: (0, 0)),
                  pl.BlockSpec((PLE_DIM, D), lambda i: (0, 0)),
                  pl.BlockSpec((1, D), lambda i: (0, 0))],
        out_specs=_pair_specs((tm, D), n1),
        out_shape=[jax.ShapeDtypeStruct((pp.shape[0], D), F32), jax.ShapeDtypeStruct((ps.shape[0], D), F32)],
        compiler_params=_cparams(("arbitrary",)),
        name="final",
    )(x, peer, pp, ps, wg, wp, fn)


def _rot_cols(w):
    half = MLA_ROPE // 2
    return jnp.concatenate([-w[..., half:], w[..., :half]], axis=-1)


def _pack_w_in(w_in):
    pts, acc = [], 0
    for sz in (512, 512, 1024, 1024, 2 * GLA_RANK, MLA_Q_LORA, MLA_KV_LORA, MLA_ROPE):
        acc += sz
        pts.append(acc)
    q, k, v, r, a_lr, c_q, c_kv, k_r, gate = jnp.split(w_in, pts, axis=-1)
    z = lambda n: jnp.zeros((D_MODEL, n), w_in.dtype)
    packed = jnp.concatenate([gate, v, r, q, k, c_q, z(CQ_PAD - MLA_Q_LORA), c_kv, k_r, _rot_cols(k_r),
                              a_lr, z(128 - 2 * GLA_RANK)], axis=-1)
    assert packed.shape[-1] == P_COLS
    return packed.astype(BF16)


def _pack_w_uq(w_uq):
    w = w_uq.reshape(MLA_Q_LORA, MLA_HEADS, MLA_DQK)
    pe = w[..., MLA_NOPE:]
    w = jnp.concatenate([w, _rot_cols(pe)], axis=-1).reshape(MLA_Q_LORA, MLA_HEADS * 256)
    return jnp.concatenate([w, jnp.zeros((CQ_PAD - MLA_Q_LORA, MLA_HEADS * 256), w.dtype)], axis=0).astype(BF16)


def _interleave_keys(w):
    lead, c = w.shape[:-2], w.shape[-1]
    half = PEER_NKEYS // 2
    return jnp.swapaxes(w.reshape(*lead, 2, half, c), -3, -2).reshape(*lead, PEER_NKEYS, c)


def _pack_u(u):
    half = PEER_NKEYS // 2
    h = lax.bitcast_convert_type(u.astype(BF16).reshape(PEER_NKEYS, 2, half, D_MODEL), jnp.uint16).astype(jnp.uint32)
    return ((h[:, 1] << 16) | h[:, 0]).reshape(PEER_N // 2, D_MODEL)


def _pack_vt(v):
    vb = v.astype(BF16).reshape(PEER_NKEYS, PEER_NKEYS, D_MODEL)
    h = lax.bitcast_convert_type(_interleave_keys(vb).reshape(PEER_N, D_MODEL).T, jnp.uint16).astype(jnp.uint32)
    return (h[D_MODEL // 2:] << 16) | h[:D_MODEL // 2]


def _tile(n, pref):
    return pref if n % pref == 0 else n


class _Tiles(NamedTuple):
    in_proj_rows: int
    gla_rows: int
    mla_prep_rows: int
    attn_q: int
    attn_kv: int
    merge_rows: int
    topk_tokens: int
    peer_tokens: int
    final_rows: int


IN_PROJ_COLS = 1024
PEER_CHUNK_KEYS = 8


def _tiles(T1, T2, S):
    T = T1 + T2
    return _Tiles(in_proj_rows=_pair_tile(T1, T2, 1024), gla_rows=_tile(S, 256), mla_prep_rows=_tile(S, 512),
                  attn_q=_tile(S, 1024), attn_kv=_tile(S, 1024), merge_rows=_pair_tile(T1, T2, 256),
                  topk_tokens=_tile(T, 256), peer_tokens=_tile(T, 1024), final_rows=_pair_tile(T1, T2, 512))


def kernel(x_prompt, x_sample, p_prompt, p_sample, norm1, w_in, gla_a_w_f, gla_a_b_f, gla_a_w_b, gla_a_b_b, gla_norm, gla_w_o, mla_q_norm, mla_w_uq, mla_kv_norm, mla_w_ukv, mla_w_o, w_out, norm2, peer_w_q, peer_k1, peer_k2, peer_u, peer_v, ple_proj, ple_gate, final_norm):
    B1, S, D = x_prompt.shape
    B2 = x_sample.shape[0]
    assert x_sample.shape[1] == S and norm1.shape[0] == 1, "one layer, equal sequence lengths"
    Bt = B1 + B2
    T = Bt * S
    T1, T2 = B1 * S, B2 * S
    xp, xs = x_prompt.reshape(T1, D), x_sample.reshape(T2, D)
    row = lambda v: v.reshape(1, -1).astype(F32)
    tiles = _tiles(T1, T2, S)

    P = _in_proj(xp, xs, row(norm1[0]), _pack_w_in(w_in[0]), tiles.in_proj_rows, IN_PROJ_COLS)

    o_f, o_b = _gla(P, gla_a_w_f[0], row(gla_a_b_f[0]), gla_a_w_b[0], row(gla_a_b_b[0]), Bt, S, tiles.gla_rows)

    pos = jnp.arange(S, dtype=F32)
    inv = ROPE_THETA ** (-jnp.arange(MLA_ROPE // 2, dtype=F32) * 2.0 / MLA_ROPE)
    ang = pos[:, None] * inv[None, :]
    cos2 = jnp.concatenate([jnp.cos(ang), jnp.cos(ang)], axis=-1)
    sin2 = jnp.concatenate([jnp.sin(ang), jnp.sin(ang)], axis=-1)
    qn = jnp.concatenate([mla_q_norm[0], jnp.zeros((CQ_PAD - MLA_Q_LORA,), F32)]).reshape(1, CQ_PAD)
    Q, K, V = _mla_prep(P, cos2, sin2, qn, _pack_w_uq(mla_w_uq[0]), row(mla_kv_norm[0]),
                        mla_w_ukv[0].astype(BF16), Bt, S, tiles.mla_prep_rows)
    o_att = _attention(Q, K, V, tiles.attn_q, tiles.attn_kv).reshape(T, MLA_HEADS * MLA_V)

    gn = jnp.tile(gla_norm[0], GLA_HEADS).reshape(1, D)
    x1, h2t, qp = _merge(xp, xs, o_f, o_b, P, o_att, gn, gla_w_o[0].astype(BF16), mla_w_o[0].astype(BF16),
                        w_out[0].astype(BF16), row(norm2[0]), peer_w_q[0].astype(BF16), tiles.merge_rows)

    r1, f1, k2r, f2 = _peer_topk(qp, peer_k1[0].astype(BF16), _interleave_keys(peer_k2[0]).astype(BF16),
                                 tiles.topk_tokens)
    peer = _peer_main(h2t, _pack_u(peer_u[0]), _pack_vt(peer_v[0]), r1, f1, k2r, f2,
                      tiles.peer_tokens, PEER_CHUNK_KEYS)

    yp, ys = _final(x1, peer, p_prompt[0].reshape(T1, PLE_DIM), p_sample[0].reshape(T2, PLE_DIM),
                    ple_gate[0].astype(BF16), ple_proj[0].astype(BF16), row(final_norm), tiles.final_rows)
    return (yp.reshape(B1, S, D), ys.reshape(B2, S, D))
```

```python
import functools
from typing import NamedTuple

import jax
import jax.numpy as jnp
from jax import lax
from jax.experimental import pallas as pl
from jax.experimental.pallas import tpu as pltpu

F32 = jnp.float32
BF16 = jnp.bfloat16
EPS = 1e-6

D_MODEL = 1024
PLE_DIM = 256
GLA_HEADS, GLA_DK, GLA_DV, GLA_RANK, GLA_GATE_NORM, GLA_CHUNK = 4, 128, 256, 16, 16.0, 64
MLA_HEADS, MLA_Q_LORA, MLA_KV_LORA, MLA_NOPE, MLA_ROPE, MLA_V = 8, 384, 256, 128, 64, 128
MLA_DQK = MLA_NOPE + MLA_ROPE
MLA_QK_PAD = 256
ROPE_THETA = 10000.0
LOG2E = 1.4426950408889634
PEER_HEADS, PEER_NKEYS, PEER_HALF, PEER_TOPK = 8, 128, 128, 16
PEER_N = PEER_NKEYS * PEER_NKEYS

P_COLS = 6144
COL_GATE, COL_V, COL_R, COL_Q, COL_K, COL_CQ, COL_CKV, COL_KR, COL_ALR = (
    0, 2048, 3072, 4096, 4608, 5120, 5632, 5888, 6016)
CQ_PAD = 512

VMEM_LIMIT = 56 * 1024 * 1024


def _cparams(sem):
    return pltpu.CompilerParams(dimension_semantics=sem, vmem_limit_bytes=VMEM_LIMIT)


def _dot(a, b):
    return jnp.dot(a, b, preferred_element_type=F32)


def _dot_nt(a, b):
    return lax.dot_general(a, b, (((1,), (1,)), ((), ())), preferred_element_type=F32)


def _dot_tn(a, b):
    return lax.dot_general(a, b, (((0,), (0,)), ((), ())), preferred_element_type=F32)


def _sigmoid(x):
    return 1.0 / (1.0 + jnp.exp(-x))


def _pair_specs(block, n1):
    return [pl.BlockSpec(block, lambda i, *_: (jnp.minimum(i, n1 - 1), 0)),
            pl.BlockSpec(block, lambda i, *_: (jnp.maximum(i - n1, 0), 0))]


def _pair_tile(T1, T2, pref):
    t = pref
    while T1 % t or T2 % t:
        t //= 2
    return t


def _in_proj_kernel(xp_ref, xs_ref, g_ref, w_ref, o_ref, h_ref, *, n1):
    @pl.when(pl.program_id(1) == 0)
    def _():
        x = jnp.where(pl.program_id(0) < n1, xp_ref[...], xs_ref[...])
        ms = jnp.mean(x * x, axis=-1, keepdims=True)
        h_ref[...] = (x * lax.rsqrt(ms + EPS) * g_ref[...]).astype(BF16)

    o_ref[...] = _dot(h_ref[...], w_ref[...]).astype(o_ref.dtype)


def _in_proj(xp, xs, g, w, tm, tn):
    n1, n2 = xp.shape[0] // tm, xs.shape[0] // tm
    T = xp.shape[0] + xs.shape[0]
    return pl.pallas_call(
        functools.partial(_in_proj_kernel, n1=n1),
        grid=(n1 + n2, P_COLS // tn),
        in_specs=_pair_specs((tm, D_MODEL), n1) + [
            pl.BlockSpec((1, D_MODEL), lambda i, j: (0, 0)),
            pl.BlockSpec((D_MODEL, tn), lambda i, j: (0, j))],
        out_specs=pl.BlockSpec((tm, tn), lambda i, j: (i, j)),
        out_shape=jax.ShapeDtypeStruct((T, P_COLS), BF16),
        scratch_shapes=[pltpu.VMEM((tm, D_MODEL), BF16)],
        compiler_params=_cparams(("parallel", "arbitrary")),
        name="in_proj",
    )(xp, xs, g, w)


def _gla_direction(q_ref, k_ref, v_ref, a_ref, aw_ref, ab_ref, st_ref, o_ref, a_off, backward, R):
    C = GLA_CHUNK
    nc = R // C
    row = lax.broadcasted_iota(jnp.int32, (R, R), 0)
    col = lax.broadcasted_iota(jnp.int32, (R, R), 1)
    same = (row // C) == (col // C)
    cum = same & ((col >= row) if backward else (col <= row))
    att_mask = same & ((col > row) if backward else (col <= row))
    tri = jnp.where(cum, 1.0, 0.0).astype(BF16)
    blk = jnp.where(same, 1.0, 0.0).astype(BF16)
    a = a_ref[:, a_off:a_off + GLA_RANK].astype(BF16)
    order = range(nc - 1, -1, -1) if backward else range(nc)
    for h in range(GLA_HEADS):
        ks = slice(h * GLA_DK, (h + 1) * GLA_DK)
        vs = slice(h * GLA_DV, (h + 1) * GLA_DV)
        z = _dot(a, aw_ref[:, ks].astype(BF16)) + ab_ref[:, ks]
        la = (jnp.minimum(z, 0.0) - jnp.log(1.0 + jnp.exp(-jnp.abs(z)))) / GLA_GATE_NORM
        la_hi = la.astype(BF16)
        la_lo = (la - la_hi.astype(F32)).astype(BF16)
        b = _dot(tri, la_hi) + _dot(tri, la_lo)
        tot = _dot(blk, la_hi) + _dot(blk, la_lo)
        qh = q_ref[:, ks].astype(F32) * (GLA_DK ** -0.5)
        kh = k_ref[:, ks].astype(F32)
        vh = v_ref[:, vs].astype(BF16)
        q_e = (qh * jnp.exp(b)).astype(BF16)
        k_e = (kh * jnp.exp(-b)).astype(BF16)
        k_s = (kh * jnp.exp(tot - b)).astype(BF16)
        dec = jnp.exp(tot)
        att = jnp.where(att_mask, _dot_nt(q_e, k_e), 0.0).astype(BF16)
        o_intra = _dot(att, vh)
        for c in order:
            rs = slice(c * C, (c + 1) * C)
            st = st_ref[h]
            o_ref[rs, vs] = o_intra[rs] + _dot_nt(q_e[rs], st.astype(BF16))
            st_ref[h] = dec[c * C:c * C + 1, :] * st + _dot_tn(vh[rs], k_s[rs])


def _gla_kernel(qf_ref, kf_ref, vf_ref, af_ref, qb_ref, kb_ref, vb_ref, ab_ref,
                awf_ref, abf_ref, awb_ref, abb_ref, of_ref, ob_ref, stf_ref, stb_ref, *, R):
    @pl.when(pl.program_id(1) == 0)
    def _():
        stf_ref[...] = jnp.zeros_like(stf_ref)
        stb_ref[...] = jnp.zeros_like(stb_ref)

    _gla_direction(qf_ref, kf_ref, vf_ref, af_ref, awf_ref, abf_ref, stf_ref, of_ref, 0, False, R)
    _gla_direction(qb_ref, kb_ref, vb_ref, ab_ref, awb_ref, abb_ref, stb_ref, ob_ref, GLA_RANK, True, R)


def _gla(P, awf, abf, awb, abb, Bt, S, R):
    T = Bt * S
    NB = S // R
    HK, HV = GLA_HEADS * GLA_DK, GLA_HEADS * GLA_DV
    fwd = lambda b, n: b * NB + n
    bwd = lambda b, n: b * NB + (NB - 1 - n)

    def specs(rowf):
        return [pl.BlockSpec((R, HK), lambda b, n: (rowf(b, n), COL_Q // HK)),
                pl.BlockSpec((R, HK), lambda b, n: (rowf(b, n), COL_K // HK)),
                pl.BlockSpec((R, HV), lambda b, n: (rowf(b, n), COL_V // HV)),
                pl.BlockSpec((R, 128), lambda b, n: (rowf(b, n), COL_ALR // 128))]

    wspec = lambda shape: pl.BlockSpec(shape, lambda b, n: (0, 0))
    return pl.pallas_call(
        functools.partial(_gla_kernel, R=R),
        grid=(Bt, NB),
        in_specs=specs(fwd) + specs(bwd) + [wspec((GLA_RANK, HK)), wspec((1, HK)),
                                            wspec((GLA_RANK, HK)), wspec((1, HK))],
        out_specs=[pl.BlockSpec((R, HV), lambda b, n: (fwd(b, n), 0)),
                   pl.BlockSpec((R, HV), lambda b, n: (bwd(b, n), 0))],
        out_shape=[jax.ShapeDtypeStruct((T, HV), F32), jax.ShapeDtypeStruct((T, HV), F32)],
        scratch_shapes=[pltpu.VMEM((GLA_HEADS, GLA_DV, GLA_DK), F32),
                        pltpu.VMEM((GLA_HEADS, GLA_DV, GLA_DK), F32)],
        compiler_params=_cparams(("parallel", "arbitrary")),
        name="gla",
    )(P, P, P, P, P, P, P, P, awf, abf, awb, abb)


def _mla_prep_kernel(cq_ref, ckv_ref, kr_ref, cos_ref, sin_ref, qn_ref, wuq_ref, kvn_ref, wukv_ref,
                     q_out, k_out, v_out):
    cq = cq_ref[...].astype(F32)
    ms = jnp.sum(cq * cq, axis=-1, keepdims=True) * (1.0 / MLA_Q_LORA)
    hq = (cq * lax.rsqrt(ms + EPS) * qn_ref[...]).astype(BF16)
    qq = _dot(hq, wuq_ref[...])
    ckv = ckv_ref[...].astype(F32)
    ms = jnp.mean(ckv * ckv, axis=-1, keepdims=True)
    hkv = (ckv * lax.rsqrt(ms + EPS) * kvn_ref[...]).astype(BF16)
    kv = _dot(hkv, wukv_ref[...])
    cos = cos_ref[...]
    sin = sin_ref[...]
    kr = kr_ref[...].astype(F32)
    k_pe = (kr[:, :MLA_ROPE] * cos + kr[:, MLA_ROPE:] * sin).astype(BF16)
    scale = MLA_DQK ** -0.5 * LOG2E
    zpad = jnp.zeros((cq.shape[0], MLA_QK_PAD - MLA_DQK), BF16)
    for h in range(MLA_HEADS):
        o = h * 256
        q_pe = qq[:, o + 128:o + 192] * cos + qq[:, o + 192:o + 256] * sin
        q_out[0, h, :, 0:MLA_NOPE] = (qq[:, o:o + MLA_NOPE] * scale).astype(BF16)
        q_out[0, h, :, MLA_NOPE:MLA_DQK] = (q_pe * scale).astype(BF16)
        q_out[0, h, :, MLA_DQK:MLA_QK_PAD] = zpad
        k_out[0, h, :, 0:MLA_NOPE] = kv[:, o:o + MLA_NOPE].astype(BF16)
        k_out[0, h, :, MLA_NOPE:MLA_DQK] = k_pe
        k_out[0, h, :, MLA_DQK:MLA_QK_PAD] = zpad
        v_out[0, h] = kv[:, o + MLA_NOPE:o + 256].astype(BF16)


def _mla_prep(P, cos2, sin2, qn, wuq, kvn, wukv, Bt, S, ts):
    NS = S // ts
    row = lambda b, i: b * NS + i
    wspec = lambda shape: pl.BlockSpec(shape, lambda b, i: (0, 0))
    H = MLA_HEADS
    return pl.pallas_call(
        _mla_prep_kernel,
        grid=(Bt, NS),
        in_specs=[pl.BlockSpec((ts, CQ_PAD), lambda b, i: (row(b, i), COL_CQ // CQ_PAD)),
                  pl.BlockSpec((ts, MLA_KV_LORA), lambda b, i: (row(b, i), COL_CKV // MLA_KV_LORA)),
                  pl.BlockSpec((ts, 128), lambda b, i: (row(b, i), COL_KR // 128)),
                  pl.BlockSpec((ts, MLA_ROPE), lambda b, i: (i, 0)),
                  pl.BlockSpec((ts, MLA_ROPE), lambda b, i: (i, 0)),
                  wspec((1, CQ_PAD)), wspec((CQ_PAD, H * 256)),
                  wspec((1, MLA_KV_LORA)), wspec((MLA_KV_LORA, H * 256))],
        out_specs=[pl.BlockSpec((1, H, ts, MLA_QK_PAD), lambda b, i: (b, 0, i, 0)),
                   pl.BlockSpec((1, H, ts, MLA_QK_PAD), lambda b, i: (b, 0, i, 0)),
                   pl.BlockSpec((1, H, ts, MLA_V), lambda b, i: (b, 0, i, 0))],
        out_shape=[jax.ShapeDtypeStruct((Bt, H, S, MLA_QK_PAD), BF16),
                   jax.ShapeDtypeStruct((Bt, H, S, MLA_QK_PAD), BF16),
                   jax.ShapeDtypeStruct((Bt, H, S, MLA_V), BF16)],
        compiler_params=_cparams(("parallel", "parallel")),
        name="mla_prep",
    )(P, P, P, cos2, sin2, qn, wuq, kvn, wukv)


ATTN_HEADS_PER_STEP = 2


def _attn_kernel(q_ref, k_ref, v_ref, o_ref, m_ref, l_ref, acc_ref):
    kv = pl.program_id(3)
    nl = k_ref.shape[2] // 128

    @pl.when(kv == 0)
    def _():
        m_ref[...] = jnp.full_like(m_ref, -jnp.inf)
        l_ref[...] = jnp.zeros_like(l_ref)
        acc_ref[...] = jnp.zeros_like(acc_ref)

    for h in range(ATTN_HEADS_PER_STEP):
        s = _dot_nt(q_ref[0, h], k_ref[0, h])
        m_prev = m_ref[h]
        m_new = jnp.maximum(m_prev, jnp.max(s, axis=-1, keepdims=True))
        alpha = jnp.exp2(m_prev - m_new)
        ps = [jnp.exp2(s[:, c * 128:(c + 1) * 128] - m_new) for c in range(nl)]
        psum = ps[0]
        for c in range(1, nl):
            psum = psum + ps[c]
        p = jnp.concatenate([x.astype(BF16) for x in ps], axis=-1)
        l_ref[h] = alpha * l_ref[h] + psum
        acc_ref[h] = alpha * acc_ref[h] + _dot(p, v_ref[0, h])
        m_ref[h] = m_new

    @pl.when(kv == pl.num_programs(3) - 1)
    def _():
        for h in range(ATTN_HEADS_PER_STEP):
            l = jnp.sum(l_ref[h], axis=-1, keepdims=True)
            o_ref[0, :, h * MLA_V:(h + 1) * MLA_V] = (acc_ref[h] / l).astype(o_ref.dtype)


def _attention(Q, K, V, tq, tk):
    Bt, H, S, _ = Q.shape
    G = ATTN_HEADS_PER_STEP
    return pl.pallas_call(
        _attn_kernel,
        grid=(Bt, H // G, S // tq, S // tk),
        in_specs=[pl.BlockSpec((1, G, tq, MLA_QK_PAD), lambda b, h, i, j: (b, h, i, 0)),
                  pl.BlockSpec((1, G, tk, MLA_QK_PAD), lambda b, h, i, j: (b, h, j, 0)),
                  pl.BlockSpec((1, G, tk, MLA_V), lambda b, h, i, j: (b, h, j, 0))],
        out_specs=pl.BlockSpec((1, tq, G * MLA_V), lambda b, h, i, j: (b, i, h)),
        out_shape=jax.ShapeDtypeStruct((Bt, S, H * MLA_V), BF16),
        scratch_shapes=[pltpu.VMEM((G, tq, 128), F32), pltpu.VMEM((G, tq, 128), F32),
                        pltpu.VMEM((G, tq, MLA_V), F32)],
        compiler_params=_cparams(("parallel", "parallel", "parallel", "arbitrary")),
        name="attn",
    )(Q, K, V)


def _merge_kernel(xp_ref, xs_ref, of_ref, ob_ref, r_ref, gate_ref, oat_ref, gn_ref, wgo_ref, wmo_ref, wout_ref,
                  n2_ref, wq_ref, x1_ref, h2t_ref, qp_ref, *, n1):
    o = of_ref[...] + ob_ref[...]
    gn = gn_ref[...]
    parts = []
    for h in range(GLA_HEADS):
        vs = slice(h * GLA_DV, (h + 1) * GLA_DV)
        oh = o[:, vs]
        ms = jnp.mean(oh * oh, axis=-1, keepdims=True)
        parts.append(oh * lax.rsqrt(ms + EPS) * gn[:, vs])
    on = jnp.concatenate(parts, axis=-1)
    r = r_ref[...].astype(F32)
    ya = _dot((on * (r * _sigmoid(r))).astype(BF16), wgo_ref[...])
    yb = _dot(oat_ref[...], wmo_ref[...])
    g = _sigmoid(gate_ref[...].astype(F32))
    mix = g[:, :D_MODEL] * ya + g[:, D_MODEL:] * yb
    x = jnp.where(pl.program_id(0) < n1, xp_ref[...], xs_ref[...])
    x1 = x + _dot(mix.astype(BF16), wout_ref[...])
    x1_ref[...] = x1
    ms = jnp.mean(x1 * x1, axis=-1, keepdims=True)
    h2 = x1 * lax.rsqrt(ms + EPS) * n2_ref[...]
    h2t_ref[...] = h2.T.astype(BF16)
    qp_ref[...] = _dot(h2.astype(BF16), wq_ref[...]).astype(BF16)


def _merge(xp, xs, o_f, o_b, P, o_att, gn, wgo, wmo, wout, n2, wq, tm):
    T = xp.shape[0] + xs.shape[0]
    n1 = xp.shape[0] // tm
    D = D_MODEL
    QW = PEER_HEADS * 2 * PEER_HALF
    tile = lambda w, c=0: pl.BlockSpec((tm, w), lambda i: (i, c))
    wspec = lambda shape: pl.BlockSpec(shape, lambda i: (0, 0))
    return pl.pallas_call(
        functools.partial(_merge_kernel, n1=n1),
        grid=(T // tm,),
        in_specs=_pair_specs((tm, D), n1) + [
            tile(D), tile(D), tile(D, COL_R // D), tile(2 * D, COL_GATE // (2 * D)), tile(D),
            wspec((1, D)), wspec((D, D)), wspec((D, D)), wspec((D, D)), wspec((1, D)), wspec((D, QW))],
        out_specs=[tile(D), pl.BlockSpec((D, tm), lambda i: (0, i)), tile(QW)],
        out_shape=[jax.ShapeDtypeStruct((T, D), F32), jax.ShapeDtypeStruct((D, T), BF16),
                   jax.ShapeDtypeStruct((T, QW), BF16)],
        compiler_params=_cparams(("parallel",)),
        name="merge",
    )(xp, xs, o_f, o_b, P, P, o_att, gn, wgo, wmo, wout, n2, wq)


def _extract16(cur, tie_index):
    n, t = cur.shape
    slot = lax.broadcasted_iota(jnp.int32, (PEER_TOPK, t), 0)
    vals = jnp.zeros((PEER_TOPK, t), F32)
    rank = jnp.full((n, t), float(PEER_TOPK), F32)
    for a in range(PEER_TOPK):
        m = jnp.max(cur, axis=0, keepdims=True)
        sel = cur == m
        if tie_index is not None:
            first = jnp.min(jnp.where(sel, tie_index, float(n * n)), axis=0, keepdims=True)
            sel = tie_index == first
        vals = jnp.where(slot == a, m, vals)
        rank = jnp.where(sel, float(a), rank)
        cur = jnp.where(sel, -jnp.inf, cur)
    return vals, rank


def _taken_error(rank):
    taken = jnp.sum(jnp.where(rank < float(PEER_TOPK), 1.0, 0.0), axis=0, keepdims=True)
    return jnp.abs(taken - float(PEER_TOPK))


_CAND_AB = [
    [(0, b) for b in range(8)],
    [(0, b) for b in range(8, 16)],
    [(1, b) for b in range(8)],
    [(2, b) for b in range(5)] + [(4, b) for b in range(3)],
    [(3, b) for b in range(4)] + [(5, 0), (5, 1), (6, 0), (6, 1)],
    [(7, 0), (7, 1)] + [(a, 0) for a in range(8, 14)],
    [(14, 0), (15, 0)] + [None] * 6,
]
assert sorted(p for t in _CAND_AB for p in t if p) == sorted(
    (a, b) for a in range(PEER_TOPK) for b in range(PEER_TOPK) if (a + 1) * (b + 1) <= PEER_TOPK)


def _candidate_tiles(t1, t2):
    tt = t1.shape[1]
    row = lax.broadcasted_iota(jnp.int32, (8, tt), 0)
    lo1, hi1, lo2, hi2 = t1[0:8], t1[8:16], t2[0:8], t2[8:16]
    r1 = lambda a: t1[a:a + 1, :]
    up = lambda x, k: pltpu.roll(x, k, 0)
    hi1r = up(hi1, 2)
    b0 = t2[0:1, :]
    tiles = [
        r1(0) + lo2,
        r1(0) + hi2,
        r1(1) + lo2,
        jnp.where(row < 5, r1(2) + lo2, r1(4) + up(lo2, 5)),
        jnp.where(row < 4, r1(3) + lo2, jnp.where(row < 6, r1(5) + up(lo2, 4), r1(6) + up(lo2, 6))),
        jnp.where(row < 2, r1(7) + lo2, hi1r + b0),
        jnp.where(row < 2, hi1r + b0, -jnp.inf),
    ]
    return jnp.concatenate(tiles, axis=0)


def _row_constants(values, tt):
    n = len(values)
    row = lax.broadcasted_iota(jnp.int32, (n, tt), 0)
    out = jnp.full((n, tt), float(values[-1]), F32)
    for r in range(n - 1):
        out = jnp.where(row == r, float(values[r]), out)
    return out


def _peer_topk_kernel(qp_ref, k1_ref, k2_ref, r1_ref, f1_ref, k2r_ref, f2_ref, s_sc, t_sc, rk_sc, cnt_sc, z_sc):
    K, H = PEER_TOPK, PEER_HEADS
    tt = qp_ref.shape[0]
    k1 = k1_ref[...]
    k2 = k2_ref[...]
    for h in range(H):
        o = h * 2 * PEER_HALF
        s_sc[2 * h] = _dot_nt(k1, qp_ref[:, o:o + PEER_HALF])
        s_sc[2 * h + 1] = _dot_nt(k2, qp_ref[:, o + PEER_HALF:o + 2 * PEER_HALF])

    def stage1(exact):
        err = jnp.zeros((1, tt), F32)
        if exact:
            pos = lax.broadcasted_iota(jnp.int32, (PEER_NKEYS, tt), 0)
            key_index = [pos.astype(F32), ((pos % 2) * (PEER_NKEYS // 2) + pos // 2).astype(F32)]
        for i in range(2 * H):
            vals, rank = _extract16(s_sc[i], key_index[i % 2] if exact else None)
            t_sc[i] = vals
            rk_sc[i] = rank
            if not exact:
                err = jnp.maximum(err, _taken_error(rank))
        return err

    err1 = stage1(False)

    @pl.when(jnp.max(err1) > 0.0)
    def _():
        stage1(True)

    pairs = [p for tile in _CAND_AB for p in tile]
    slot = lax.broadcasted_iota(jnp.int32, (K, tt), 0)

    def stage2(exact):
        err = jnp.zeros((1, tt), F32)
        flat = _row_constants([K * K if p is None else p[0] * K + p[1] for p in pairs], tt) if exact else None
        for h in range(H):
            cand = _candidate_tiles(t_sc[2 * h], t_sc[2 * h + 1])
            _, rank = _extract16(cand, flat)
            chosen = rank < float(K)
            ch = jnp.where(chosen, 1.0, 0.0)
            z = jnp.sum(jnp.where(chosen, jnp.exp(cand - cand[0:1, :]), 0.0), axis=0, keepdims=True)
            z_sc[h] = jnp.broadcast_to(z, (8, tt))
            cnt = jnp.zeros((K, tt), F32)
            for a in range(K):
                rows = [r for r, p in enumerate(pairs) if p is not None and p[0] == a]
                ca = jnp.sum(ch[rows[0]:rows[-1] + 1, :], axis=0, keepdims=True)
                cnt = jnp.where(slot == a, ca, cnt)
            cnt_sc[h] = cnt
            if not exact:
                err = jnp.maximum(err, _taken_error(rank))
        return err

    err2 = stage2(False)

    @pl.when(jnp.max(err2) > 0.0)
    def _():
        stage2(True)

    for h in range(H):
        s1, s2 = s_sc[2 * h], s_sc[2 * h + 1]
        t1, t2 = t_sc[2 * h], t_sc[2 * h + 1]
        rank1 = rk_sc[2 * h]
        cnt = cnt_sc[h]
        r1 = jnp.zeros(rank1.shape, F32)
        for a in range(K):
            r1 = jnp.where(rank1 == float(a), cnt[a:a + 1, :], r1)
        r1_ref[h] = r1
        f1_ref[h] = jnp.exp(s1 - t1[0:1, :]) * (0.5 / z_sc[h, 0:1, :])
        k2r_ref[h] = rk_sc[2 * h + 1].astype(BF16)
        f2_ref[h] = jnp.exp(s2 - t2[0:1, :]).astype(BF16)


def _peer_topk(qp, k1, k2, tt):
    T = qp.shape[0]
    H, N = PEER_HEADS, PEER_NKEYS
    out = jax.ShapeDtypeStruct((H, N, T), F32)
    outb = jax.ShapeDtypeStruct((H, N, T), BF16)
    ospec = pl.BlockSpec((H, N, tt), lambda i: (0, 0, i))
    return pl.pallas_call(
        _peer_topk_kernel,
        grid=(T // tt,),
        in_specs=[pl.BlockSpec((tt, H * 2 * PEER_HALF), lambda i: (i, 0)),
                  pl.BlockSpec((N, PEER_HALF), lambda i: (0, 0)),
                  pl.BlockSpec((N, PEER_HALF), lambda i: (0, 0))],
        out_specs=[ospec, ospec, ospec, ospec],
        out_shape=[out, out, outb, outb],
        scratch_shapes=[pltpu.VMEM((2 * H, N, tt), F32), pltpu.VMEM((2 * H, PEER_TOPK, tt), F32),
                        pltpu.VMEM((2 * H, N, tt), F32), pltpu.VMEM((H, PEER_TOPK, tt), F32),
                        pltpu.VMEM((H, 8, tt), F32)],
        compiler_params=_cparams(("parallel",)),
        name="peer_topk",
    )(qp, k1, k2)


PEER_PASS_LANES = 256


def _peer_main_kernel(h2t_ref, u_ref, vt_ref, r1_ref, f1_ref, k2r_ref, f2_ref, o_ref,
                      acc_ref, s_ref, g_ref, *, ec):
    j = pl.program_id(1)
    N = PEER_NKEYS
    tt = s_ref.shape[2]
    lw = PEER_PASS_LANES

    @pl.when(j == 0)
    def _():
        acc_ref[...] = jnp.zeros_like(acc_ref)
        s_ref[1] = jnp.zeros(s_ref.shape[1:], s_ref.dtype)

    def step(rd, wr):
        th = tt // 2
        for half in range(2):
            hs = slice(half * th, (half + 1) * th)
            s_ref[wr, :, hs] = _dot(pltpu.bitcast(u_ref[...], BF16), h2t_ref[:, hs])
            for e in range(ec):
                rows = slice(e * N, (e + 1) * N)
                for c in range(th // lw):
                    cs = slice(half * th + c * lw, half * th + (c + 1) * lw)
                    w = jnp.zeros((N // 16, 16, lw), BF16)
                    for h in range(PEER_HEADS):
                        r1 = jnp.broadcast_to(r1_ref[h, e:e + 1, cs], (16, lw)).astype(BF16)[None]
                        f1 = jnp.broadcast_to(f1_ref[h, e:e + 1, cs], (16, lw)).astype(BF16)[None]
                        k2 = k2r_ref[h, :, cs].reshape(N // 16, 16, lw)
                        f2 = f2_ref[h, :, cs].reshape(N // 16, 16, lw)
                        w = w + jnp.where(k2 < r1, f2, jnp.zeros((), BF16)) * f1
                    se = s_ref[rd, rows, cs]
                    act = se * (1.0 + lax.erf(se * (2.0 ** -0.5)))
                    g_ref[rows, cs] = act.astype(BF16) * w.reshape(N, lw)
            res = _dot(pltpu.bitcast(vt_ref[...], BF16), g_ref[:, hs])
            for c in range(th // 128):
                acc_ref[half * (th // 128) + c] += res[:, c * 128:(c + 1) * 128]

    @pl.when(j % 2 == 0)
    def _():
        step(1, 0)

    @pl.when(j % 2 == 1)
    def _():
        step(0, 1)

    @pl.when(j == pl.num_programs(1) - 1)
    def _():
        nf = acc_ref.shape[1] // 2
        for c in range(acc_ref.shape[0]):
            feats = [acc_ref[c, pl.ds(s, nf, stride=2), :] for s in range(2)]
            o_ref[c * 128:(c + 1) * 128, :] = jnp.concatenate(feats, axis=0).T


def _peer_main(h2t, u, vt, r1, f1, k2r, f2, tt, ec):
    T = h2t.shape[1]
    D, H, N = D_MODEL, PEER_HEADS, PEER_NKEYS
    assert tt % (2 * PEER_PASS_LANES) == 0, "a token tile is two halves of whole weight-build passes"
    nc = N // ec
    lag = lambda j, k: jnp.clip(j - k, 0, nc - 1)
    return pl.pallas_call(
        functools.partial(_peer_main_kernel, ec=ec),
        grid=(T // tt, nc + 1),
        in_specs=[pl.BlockSpec((D, tt), lambda i, j: (0, i)),
                  pl.BlockSpec((ec * N // 2, D), lambda i, j: (lag(j, 0), 0)),
                  pl.BlockSpec((D // 2, ec * N), lambda i, j: (0, lag(j, 1))),
                  pl.BlockSpec((H, ec, tt), lambda i, j: (0, lag(j, 1), i)),
                  pl.BlockSpec((H, ec, tt), lambda i, j: (0, lag(j, 1), i)),
                  pl.BlockSpec((H, N, tt), lambda i, j: (0, 0, i)),
                  pl.BlockSpec((H, N, tt), lambda i, j: (0, 0, i))],
        out_specs=pl.BlockSpec((tt, D), lambda i, j: (i, 0)),
        out_shape=jax.ShapeDtypeStruct((T, D), F32),
        scratch_shapes=[pltpu.VMEM((tt // 128, D, 128), F32), pltpu.VMEM((2, ec * N, tt), F32),
                        pltpu.VMEM((ec * N, tt), BF16)],
        compiler_params=_cparams(("parallel", "arbitrary")),
        name="peer_main",
    )(h2t, u, vt, r1, f1, k2r, f2)


def _final_kernel(x_ref, peer_ref, pp_ref, ps_ref, wg_ref, wp_ref, fn_ref, yp_ref, ys_ref, *, n1):
    i = pl.program_id(0)
    x = x_ref[...] + peer_ref[...]
    gate = _sigmoid(_dot(x.astype(BF16), wg_ref[...]))
    p = jnp.where(i < n1, pp_ref[...], ps_ref[...])
    x = x + gate * _dot(p.astype(BF16), wp_ref[...])
    ms = jnp.mean(x * x, axis=-1, keepdims=True)
    y = x * lax.rsqrt(ms + EPS) * fn_ref[...]

    @pl.when(i < n1)
    def _():
        yp_ref[...] = y

    @pl.when(i >= n1)
    def _():
        ys_ref[...] = y


def _final(x, peer, pp, ps, wg, wp, fn, tm):
    T = x.shape[0]
    n1 = pp.shape[0] // tm
    D = D_MODEL
    return pl.pallas_call(
        functools.partial(_final_kernel, n1=n1),
        grid=(T // tm,),
        in_specs=[pl.BlockSpec((tm, D), lambda i: (i, 0)),
                  pl.BlockSpec((tm, D), lambda i: (i, 0))] + _pair_specs((tm, PLE_DIM), n1) + [
                  pl.BlockSpec((D, D), lambda i: (0, 0)),
                  pl.BlockSpec((PLE_DIM, D), lambda i: (0, 0)),
                  pl.BlockSpec((1, D), lambda i: (0, 0))],
        out_specs=_pair_specs((tm, D), n1),
        out_shape=[jax.ShapeDtypeStruct((pp.shape[0], D), F32), jax.ShapeDtypeStruct((ps.shape[0], D), F32)],
        compiler_params=_cparams(("arbitrary",)),
        name="final",
    )(x, peer, pp, ps, wg, wp, fn)


def _rot_cols(w):
    half = MLA_ROPE // 2
    return jnp.concatenate([-w[..., half:], w[..., :half]], axis=-1)


def _pack_w_in(w_in):
    pts, acc = [], 0
    for sz in (512, 512, 1024, 1024, 2 * GLA_RANK, MLA_Q_LORA, MLA_KV_LORA, MLA_ROPE):
        acc += sz
        pts.append(acc)
    q, k, v, r, a_lr, c_q, c_kv, k_r, gate = jnp.split(w_in, pts, axis=-1)
    z = lambda n: jnp.zeros((D_MODEL, n), w_in.dtype)
    packed = jnp.concatenate([gate, v, r, q, k, c_q, z(CQ_PAD - MLA_Q_LORA), c_kv, k_r, _rot_cols(k_r),
                              a_lr, z(128 - 2 * GLA_RANK)], axis=-1)
    assert packed.shape[-1] == P_COLS
    return packed.astype(BF16)


def _pack_w_uq(w_uq):
    w = w_uq.reshape(MLA_Q_LORA, MLA_HEADS, MLA_DQK)
    pe = w[..., MLA_NOPE:]
    w = jnp.concatenate([w, _rot_cols(pe)], axis=-1).reshape(MLA_Q_LORA, MLA_HEADS * 256)
    return jnp.concatenate([w, jnp.zeros((CQ_PAD - MLA_Q_LORA, MLA_HEADS * 256), w.dtype)], axis=0).astype(BF16)


def _interleave_keys(w):
    lead, c = w.shape[:-2], w.shape[-1]
    half = PEER_NKEYS // 2
    return jnp.swapaxes(w.reshape(*lead, 2, half, c), -3, -2).reshape(*lead, PEER_NKEYS, c)


def _pack_u(u):
    half = PEER_NKEYS // 2
    h = lax.bitcast_convert_type(u.astype(BF16).reshape(PEER_NKEYS, 2, half, D_MODEL), jnp.uint16).astype(jnp.uint32)
    return ((h[:, 1] << 16) | h[:, 0]).reshape(PEER_N // 2, D_MODEL)


def _pack_vt(v):
    vt = _interleave_keys(v.reshape(PEER_NKEYS, PEER_NKEYS, D_MODEL)).reshape(PEER_N, D_MODEL).astype(BF16).T
    h = lax.bitcast_convert_type(vt.reshape(2, D_MODEL // 2, PEER_N), jnp.uint16).astype(jnp.uint32)
    return (h[1] << 16) | h[0]


def _tile(n, pref):
    return pref if n % pref == 0 else n


class _Tiles(NamedTuple):
    in_proj_rows: int
    gla_rows: int
    mla_prep_rows: int
    attn_q: int
    attn_kv: int
    merge_rows: int
    topk_tokens: int
    peer_tokens: int
    final_rows: int


IN_PROJ_COLS = 1024
PEER_CHUNK_KEYS = 8


def _tiles(T1, T2, S):
    T = T1 + T2
    return _Tiles(in_proj_rows=_pair_tile(T1, T2, 1024), gla_rows=_tile(S, 256), mla_prep_rows=_tile(S, 512),
                  attn_q=_tile(S, 1024), attn_kv=_tile(S, 2048), merge_rows=_pair_tile(T1, T2, 256),
                  topk_tokens=_tile(T, 256), peer_tokens=_tile(T, 1024), final_rows=_pair_tile(T1, T2, 512))


def kernel(x_prompt, x_sample, p_prompt, p_sample, norm1, w_in, gla_a_w_f, gla_a_b_f, gla_a_w_b, gla_a_b_b, gla_norm, gla_w_o, mla_q_norm, mla_w_uq, mla_kv_norm, mla_w_ukv, mla_w_o, w_out, norm2, peer_w_q, peer_k1, peer_k2, peer_u, peer_v, ple_proj, ple_gate, final_norm):
    B1, S, D = x_prompt.shape
    B2 = x_sample.shape[0]
    assert x_sample.shape[1] == S and norm1.shape[0] == 1, "one layer, equal sequence lengths"
    Bt = B1 + B2
    T = Bt * S
    T1, T2 = B1 * S, B2 * S
    xp, xs = x_prompt.reshape(T1, D), x_sample.reshape(T2, D)
    row = lambda v: v.reshape(1, -1).astype(F32)
    tiles = _tiles(T1, T2, S)

    P = _in_proj(xp, xs, row(norm1[0]), _pack_w_in(w_in[0]), tiles.in_proj_rows, IN_PROJ_COLS)

    o_f, o_b = _gla(P, gla_a_w_f[0], row(gla_a_b_f[0]), gla_a_w_b[0], row(gla_a_b_b[0]), Bt, S, tiles.gla_rows)

    pos = jnp.arange(S, dtype=F32)
    inv = ROPE_THETA ** (-jnp.arange(MLA_ROPE // 2, dtype=F32) * 2.0 / MLA_ROPE)
    ang = pos[:, None] * inv[None, :]
    cos2 = jnp.concatenate([jnp.cos(ang), jnp.cos(ang)], axis=-1)
    sin2 = jnp.concatenate([jnp.sin(ang), jnp.sin(ang)], axis=-1)
    qn = jnp.concatenate([mla_q_norm[0], jnp.zeros((CQ_PAD - MLA_Q_LORA,), F32)]).reshape(1, CQ_PAD)
    Q, K, V = _mla_prep(P, cos2, sin2, qn, _pack_w_uq(mla_w_uq[0]), row(mla_kv_norm[0]),
                        mla_w_ukv[0].astype(BF16), Bt, S, tiles.mla_prep_rows)
    o_att = _attention(Q, K, V, tiles.attn_q, tiles.attn_kv).reshape(T, MLA_HEADS * MLA_V)

    gn = jnp.tile(gla_norm[0], GLA_HEADS).reshape(1, D)
    x1, h2t, qp = _merge(xp, xs, o_f, o_b, P, o_att, gn, gla_w_o[0].astype(BF16), mla_w_o[0].astype(BF16),
                        w_out[0].astype(BF16), row(norm2[0]), peer_w_q[0].astype(BF16), tiles.merge_rows)

    r1, f1, k2r, f2 = _peer_topk(qp, peer_k1[0].astype(BF16), _interleave_keys(peer_k2[0]).astype(BF16),
                                 tiles.topk_tokens)
    peer = _peer_main(h2t, _pack_u(peer_u[0]), _pack_vt(peer_v[0]), r1, f1, k2r, f2,
                      tiles.peer_tokens, PEER_CHUNK_KEYS)

    yp, ys = _final(x1, peer, p_prompt[0].reshape(T1, PLE_DIM), p_sample[0].reshape(T2, PLE_DIM),
                    ple_gate[0].astype(BF16), ple_proj[0].astype(BF16), row(final_norm), tiles.final_rows)
    return (yp.reshape(B1, S, D), ys.reshape(B2, S, D))
```
